```python
import math
import jax, jax.numpy as jnp
from jax import lax
import numpy as np

D_MODEL = 1024
BATCH = 4
SEQ = 8192
DEPTH = 4
DEC_BATCH = 16
DEC_SEQ = 32
PAST_LEN = 1024

CHUNK = 64
N_META = 16
HEAD_DIM = 64
N_HEADS = 8
N_KV_HEADS = 2
N_GROUPS = N_HEADS // N_KV_HEADS
IDX_HEADS = 4
IDX_DIM = 64
ROT_DIM = HEAD_DIM // 4
ROPE_THETA = 500000.0
CONV_DIM = D_MODEL // 2
CONV_WIDTH = 3
D_FF = 4 * D_MODEL
TOPK_MAX = 256
Q_BLOCK = 128
EPS = 1e-6

Q_COLS = N_HEADS * HEAD_DIM
KV_COLS = N_KV_HEADS * HEAD_DIM
END_Q = Q_COLS
END_K = END_Q + KV_COLS
END_V = END_K + KV_COLS
END_IQ = END_V + IDX_HEADS * IDX_DIM
END_IK = END_IQ + IDX_DIM
END_IW = END_IK + IDX_HEADS
END_CB = END_IW + CONV_DIM
END_CC = END_CB + CONV_DIM
END_CX = END_CC + CONV_DIM
END_GA = END_CX + D_MODEL
D_IN = END_GA + D_MODEL

kernel_name = "hybrid_dsa_shortconv_stream_step"


def rms_norm(x, g):
    xf = x.astype(jnp.float32)
    y = xf * lax.rsqrt(jnp.mean(xf * xf, axis=-1, keepdims=True) + EPS)
    return (y * g.astype(jnp.float32)).astype(x.dtype)


def rope(x, pos):
    half = ROT_DIM // 2
    inv = jnp.exp(-math.log(ROPE_THETA) * jnp.arange(half, dtype=jnp.float32) * (2.0 / ROT_DIM))
    ang = pos.astype(jnp.float32)[:, None] * inv[None, :]
    cos = jnp.cos(ang)[None, :, None, :]
    sin = jnp.sin(ang)[None, :, None, :]
    xr = x[..., :ROT_DIM].astype(jnp.float32)
    x1, x2 = xr[..., :half], xr[..., half:]
    rot = jnp.concatenate([x1 * cos - x2 * sin, x2 * cos + x1 * sin], axis=-1).astype(x.dtype)
    return jnp.concatenate([rot, x[..., ROT_DIM:]], axis=-1)


def dsa_attend(q, qi, wi, q_chunk, k, v, ki, key_chunk, topk):
    f32 = jnp.float32
    s = jnp.einsum('bqhd,bld->bqhl', qi.astype(f32), ki.astype(f32)) * (IDX_DIM ** -0.5)
    score = jnp.einsum('bqh,bqhl->bql', wi.astype(f32), jax.nn.relu(s))
    adm = key_chunk[None, :] <= q_chunk[:, None]
    score = jnp.where(adm[None], score, -jnp.inf)
    _, idx = lax.top_k(score, topk)
    valid = key_chunk[idx] <= q_chunk[None, :, None]
    kg = jax.vmap(lambda kb, ib: kb[ib])(k, idx)
    vg = jax.vmap(lambda vb, ib: vb[ib])(v, idx)
    logits = jnp.einsum('bqhgd,bqkhd->bqhgk', q.astype(f32), kg.astype(f32)) * (HEAD_DIM ** -0.5)
    logits = jnp.where(valid[:, :, None, None, :], logits, -jnp.inf)
    p = jax.nn.softmax(logits, axis=-1)
    o = jnp.einsum('bqhgk,bqkhd->bqhgd', p, vg.astype(f32))
    return o.astype(q.dtype)


def sparse_attention(q, qi, wi, q_chunk, k, v, ki, key_chunk, topk):
    B, L = q.shape[0], q.shape[1]
    if L <= Q_BLOCK:
        return dsa_attend(q, qi, wi, q_chunk, k, v, ki, key_chunk, topk)
    nb = -(-L // Q_BLOCK)
    pad = nb * Q_BLOCK - L

    def blocks(a):
        a = jnp.pad(a, [(0, 0), (0, pad)] + [(0, 0)] * (a.ndim - 2))
        a = a.reshape((B, nb, Q_BLOCK) + a.shape[2:])
        return jnp.moveaxis(a, 1, 0)

    qc = jnp.pad(q_chunk, (0, pad), mode='edge').reshape(nb, Q_BLOCK)
    out = lax.map(lambda xs: dsa_attend(xs[0], xs[1], xs[2], xs[3], k, v, ki, key_chunk, topk),
                  (blocks(q), blocks(qi), blocks(wi), qc))
    out = jnp.moveaxis(out, 0, 1).reshape((B, nb * Q_BLOCK) + out.shape[3:])
    return out[:, :L]


def token_mixers(h, pos, q_chunk, key_chunk, k_past, v_past, ki_past, conv_past,
                 w_in, conv_w, w_attn_out, w_conv_out, w_o, topk):
    B, L, _ = h.shape
    z = h @ w_in
    q, k, v, qi, ki, wi, cb, cc, cx, ga, gb = jnp.split(
        z, [END_Q, END_K, END_V, END_IQ, END_IK, END_IW, END_CB, END_CC, END_CX, END_GA], axis=-1)
    q = rope(q.reshape(B, L, N_HEADS, HEAD_DIM), pos).reshape(B, L, N_KV_HEADS, N_GROUPS, HEAD_DIM)
    k_new = rope(k.reshape(B, L, N_KV_HEADS, HEAD_DIM), pos)
    v_new = v.reshape(B, L, N_KV_HEADS, HEAD_DIM)
    qi = rope(qi.reshape(B, L, IDX_HEADS, IDX_DIM), pos)
    ki_new = rope(ki[:, :, None, :], pos)[:, :, 0, :]
    wi = wi * (IDX_HEADS ** -0.5)
    k_all = jnp.concatenate([k_past, k_new], axis=1)
    v_all = jnp.concatenate([v_past, v_new], axis=1)
    ki_all = jnp.concatenate([ki_past, ki_new], axis=1)
    o = sparse_attention(q, qi, wi, q_chunk, k_all, v_all, ki_all, key_chunk, topk)
    y_a = o.reshape(B, L, Q_COLS) @ w_attn_out
    u = cc * cx
    u_pad = jnp.concatenate([conv_past, u], axis=1)
    conv = sum(conv_w[j] * u_pad[:, j:j + L] for j in range(CONV_WIDTH))
    y_b = (cb * conv) @ w_conv_out
    conv_new = u_pad[:, -(CONV_WIDTH - 1):]
    m = jax.nn.sigmoid(ga) * y_a + jax.nn.sigmoid(gb) * y_b
    return m @ w_o, k_new, v_new, ki_new, conv_new


def trunk_layer(x, pos, q_chunk, key_chunk, k_past, v_past, ki_past, conv_past, gains,
                w_in, conv_w, w_attn_out, w_conv_out, w_o, w_up, w_down, topk):
    a, k_new, v_new, ki_new, conv_new = token_mixers(
        rms_norm(x, gains[0]), pos, q_chunk, key_chunk, k_past, v_past, ki_past, conv_past,
        w_in, conv_w, w_attn_out, w_conv_out, w_o, topk)
    x = x + rms_norm(a, gains[1])
    f = jnp.square(jax.nn.relu(rms_norm(x, gains[2]) @ w_up)) @ w_down
    x = x + rms_norm(f, gains[3])
    return x, k_new, v_new, ki_new, conv_new


def setup_inputs(seed: int = 0) -> dict:
    key = jax.random.key(seed)
    ks = jax.random.split(key, 16)
    f32 = jnp.float32

    def nrm(k, shape, scale):
        return jax.random.normal(k, shape, f32) * scale

    return {
        "x_prompt": nrm(ks[0], (BATCH, SEQ, D_MODEL), 1.0),
        "x_sample": nrm(ks[1], (DEC_BATCH, DEC_SEQ, D_MODEL), 1.0),
        "cache_k": nrm(ks[2], (DEPTH, DEC_BATCH, PAST_LEN, N_KV_HEADS, HEAD_DIM), 1.0),
        "cache_v": nrm(ks[3], (DEPTH, DEC_BATCH, PAST_LEN, N_KV_HEADS, HEAD_DIM), 1.0),
        "cache_kidx": nrm(ks[4], (DEPTH, DEC_BATCH, PAST_LEN, IDX_DIM), 1.0),
        "state_conv": nrm(ks[5], (DEPTH, DEC_BATCH, CONV_WIDTH - 1, CONV_DIM), 1.0),
        "meta_tokens": nrm(ks[6], (N_META, D_MODEL), 1.0),
        "norm_gains": 1.0 + nrm(ks[7], (DEPTH, 4, D_MODEL), 0.05),
        "w_in": nrm(ks[8], (DEPTH, D_MODEL, D_IN), D_MODEL ** -0.5),
        "conv_w": nrm(ks[9], (DEPTH, CONV_WIDTH, CONV_DIM), CONV_WIDTH ** -0.5),
        "w_attn_out": nrm(ks[10], (DEPTH, Q_COLS, D_MODEL), Q_COLS ** -0.5),
        "w_conv_out": nrm(ks[11], (DEPTH, CONV_DIM, D_MODEL), CONV_DIM ** -0.5),
        "w_o": nrm(ks[12], (DEPTH, D_MODEL, D_MODEL), D_MODEL ** -0.5),
        "w_up": nrm(ks[13], (DEPTH, D_MODEL, D_FF), D_MODEL ** -0.5),
        "w_down": nrm(ks[14], (DEPTH, D_FF, D_MODEL), D_FF ** -0.5),
    }


def reference(x_prompt, x_sample, cache_k, cache_v, cache_kidx, state_conv, meta_tokens,
              norm_gains, w_in, conv_w, w_attn_out, w_conv_out, w_o, w_up, w_down):
    B = x_prompt.shape[0]
    xp = jnp.concatenate(
        [jnp.broadcast_to(meta_tokens[None].astype(x_prompt.dtype), (B, N_META, D_MODEL)), x_prompt], axis=1)
    T = xp.shape[1]
    pos_p = jnp.arange(T, dtype=jnp.int32)
    chunk_p = jnp.where(pos_p < N_META, -1, (pos_p - N_META) // CHUNK)
    topk_p = min(TOPK_MAX, x_prompt.shape[1] // 4)
    k_empty = jnp.zeros((B, 0, N_KV_HEADS, HEAD_DIM), xp.dtype)
    ki_empty = jnp.zeros((B, 0, IDX_DIM), xp.dtype)
    conv_zero = jnp.zeros((B, CONV_WIDTH - 1, CONV_DIM), xp.dtype)

    xs = x_sample
    S = xs.shape[1]
    P = cache_k.shape[2]
    pos_s = P + jnp.arange(S, dtype=jnp.int32)
    qchunk_s = jnp.ones((S,), jnp.int32)
    kchunk_s = jnp.concatenate([jnp.zeros((P,), jnp.int32), jnp.ones((S,), jnp.int32)])
    topk_s = min(TOPK_MAX, (P + S) // 4)

    kp, vp, kip, cp = [], [], [], []
    ksl, vsl, kisl, csl = [], [], [], []
    for l in range(DEPTH):
        xp, k_n, v_n, ki_n, c_n = trunk_layer(
            xp, pos_p, chunk_p, chunk_p, k_empty, k_empty, ki_empty, conv_zero, norm_gains[l],
            w_in[l], conv_w[l], w_attn_out[l], w_conv_out[l], w_o[l], w_up[l], w_down[l], topk_p)
        kp.append(k_n); vp.append(v_n); kip.append(ki_n); cp.append(c_n)
        xs, k_n, v_n, ki_n, c_n = trunk_layer(
            xs, pos_s, qchunk_s, kchunk_s, cache_k[l], cache_v[l], cache_kidx[l], state_conv[l], norm_gains[l],
            w_in[l], conv_w[l], w_attn_out[l], w_conv_out[l], w_o[l], w_up[l], w_down[l], topk_s)
        ksl.append(k_n); vsl.append(v_n); kisl.append(ki_n); csl.append(c_n)

    y_prompt = xp[:, N_META:]
    y_sample = xs
    return (y_prompt, y_sample,
            jnp.stack(kp), jnp.stack(vp), jnp.stack(kip), jnp.stack(cp),
            jnp.stack(ksl), jnp.stack(vsl), jnp.stack(kisl), jnp.stack(csl))
```

```python
import functools
import math

import jax
import jax.numpy as jnp
from jax import lax
from jax.experimental import pallas as pl
from jax.experimental.pallas import tpu as pltpu

D_MODEL = 1024
CHUNK = 64
CHUNK_SHIFT = CHUNK.bit_length() - 1
N_META = 16
HEAD_DIM = 64
N_HEADS = 8
N_KV_HEADS = 2
N_GROUPS = N_HEADS // N_KV_HEADS
IDX_HEADS = 4
IDX_DIM = 64
ROT_DIM = HEAD_DIM // 4
ROT_HALF = ROT_DIM // 2
ROPE_THETA = 500000.0
CONV_DIM = D_MODEL // 2
CONV_WIDTH = 3
D_FF = 4 * D_MODEL
TOPK_MAX = 256
EPS = 1e-6

Q_COLS = N_HEADS * HEAD_DIM
KV_COLS = N_KV_HEADS * HEAD_DIM
IQ_COLS = IDX_HEADS * IDX_DIM
END_Q = Q_COLS
END_K = END_Q + KV_COLS
END_V = END_K + KV_COLS
END_IQ = END_V + IQ_COLS
END_IK = END_IQ + IDX_DIM
END_IW = END_IK + IDX_HEADS
END_CB = END_IW + CONV_DIM
END_CC = END_CB + CONV_DIM
END_CX = END_CC + CONV_DIM
END_GA = END_CX + D_MODEL
D_IN = END_GA + D_MODEL

SUBLANES = 8
LANES = 128
IW_ROWS = SUBLANES
ZT_Q = 0
ZT_K = ZT_Q + Q_COLS
ZT_V = ZT_K + KV_COLS
ZT_IQ = ZT_V + KV_COLS
ZT_IK = ZT_IQ + IQ_COLS
ZT_IW = ZT_IK + IDX_DIM
ZT_ROWS = ZT_IW + IW_ROWS

ROW_TILE = 256
Q_TILE = 256
NEG_BIAS = -1e30
INT_MIN = -(2 ** 31)
NEG_INF_KEY = INT_MIN + 0x7FFFFF
INDEX_BITS = 15
VMEM_LIMIT = 56 * 1024 * 1024


def _rms(x, g):
    return x * lax.rsqrt(jnp.mean(x * x, axis=-1, keepdims=True) + EPS) * g


def _sigmoid(x):
    return 1.0 / (1.0 + jnp.exp(-x))


def _dot(a, b):
    return jnp.dot(a, b, preferred_element_type=jnp.float32)


def _dot_nt(a, b):
    return lax.dot_general(a, b, (((1,), (1,)), ((), ())), preferred_element_type=jnp.float32)


def _rope_rows(zt, n_heads, cos, sin):
    pieces = []
    for h in range(n_heads):
        o = h * HEAD_DIM
        x1 = zt[o:o + ROT_HALF]
        x2 = zt[o + ROT_HALF:o + ROT_DIM]
        pieces.append(x1 * cos - x2 * sin)
        pieces.append(x2 * cos + x1 * sin)
        pieces.append(zt[o + ROT_DIM:o + HEAD_DIM])
    return jnp.concatenate(pieces, axis=0)


def _in_proj_kernel(x_ref, g_ref, wt_ref, wc_ref, cos_ref, sin_ref,
                    qt_ref, qit_ref, wit_ref, vt_ref, kb_ref, kib_ref,
                    k_ref, v_ref, ki_ref, u_ref, cb_ref):
    h = _rms(x_ref[...], g_ref[...]).astype(jnp.bfloat16)
    zt = _dot_nt(wt_ref[...], h)
    cos = cos_ref[...]
    sin = sin_ref[...]
    qt_ref[...] = _rope_rows(zt[ZT_Q:ZT_K], N_HEADS, cos, sin).astype(jnp.bfloat16)
    k = _rope_rows(zt[ZT_K:ZT_V], N_KV_HEADS, cos, sin).T
    k_ref[...] = k
    for g in range(N_KV_HEADS):
        kb_ref[g] = k[:, g * HEAD_DIM:(g + 1) * HEAD_DIM].astype(jnp.bfloat16)
    vt = zt[ZT_V:ZT_IQ]
    vt_ref[0] = vt.astype(jnp.bfloat16)
    v_ref[...] = vt.T
    qit_ref[...] = _rope_rows(zt[ZT_IQ:ZT_IK], IDX_HEADS, cos, sin).astype(jnp.bfloat16)
    ki = _rope_rows(zt[ZT_IK:ZT_IW], 1, cos, sin).T
    ki_ref[...] = ki
    kib_ref[...] = ki.astype(jnp.bfloat16)
    wit_ref[...] = zt[ZT_IW:ZT_ROWS]
    zc = _dot(h, wc_ref[...])
    cb_ref[...] = zc[:, :CONV_DIM]
    u_ref[...] = zc[:, CONV_DIM:2 * CONV_DIM] * zc[:, 2 * CONV_DIM:]


def _in_proj(x, g0, wt, wc, cos_t, sin_t):
    n = x.shape[0]
    tm = ROW_TILE
    nb = n // tm
    f32, bf16 = jnp.float32, jnp.bfloat16
    full = lambda shape: pl.BlockSpec(shape, lambda i: (0,) * len(shape))
    rows = lambda w: pl.BlockSpec((tm, w), lambda i: (i, 0))
    cols = lambda r: pl.BlockSpec((r, tm), lambda i: (0, i))
    out_shape = (
        jax.ShapeDtypeStruct((Q_COLS, n), bf16),
        jax.ShapeDtypeStruct((IQ_COLS, n), bf16),
        jax.ShapeDtypeStruct((IW_ROWS, n), f32),
        jax.ShapeDtypeStruct((nb, KV_COLS, tm), bf16),
        jax.ShapeDtypeStruct((N_KV_HEADS, n, HEAD_DIM), bf16),
        jax.ShapeDtypeStruct((n, IDX_DIM), bf16),
        jax.ShapeDtypeStruct((n, KV_COLS), f32),
        jax.ShapeDtypeStruct((n, KV_COLS), f32),
        jax.ShapeDtypeStruct((n, IDX_DIM), f32),
        jax.ShapeDtypeStruct((n, CONV_DIM), f32),
        jax.ShapeDtypeStruct((n, CONV_DIM), f32),
    )
    out_specs = (
        cols(Q_COLS), cols(IQ_COLS), cols(IW_ROWS),
        pl.BlockSpec((1, KV_COLS, tm), lambda i: (i, 0, 0)),
        pl.BlockSpec((N_KV_HEADS, tm, HEAD_DIM), lambda i: (0, i, 0)),
        rows(IDX_DIM), rows(KV_COLS), rows(KV_COLS), rows(IDX_DIM), rows(CONV_DIM), rows(CONV_DIM),
    )
    return pl.pallas_call(
        _in_proj_kernel,
        grid=(nb,),
        in_specs=[rows(D_MODEL), full((1, D_MODEL)), full((ZT_ROWS, D_MODEL)), full((D_MODEL, 3 * CONV_DIM)),
                  cols(ROT_HALF), cols(ROT_HALF)],
        out_specs=out_specs,
        out_shape=out_shape,
        compiler_params=pltpu.CompilerParams(dimension_semantics=("arbitrary",), vmem_limit_bytes=VMEM_LIMIT),
        name="in_proj",
    )(x, g0, wt, wc, cos_t, sin_t)


def _attn_kernel(qt_ref, qit_ref, wit_ref, kb_ref, kib_ref, vt_ref, o_ref,
                 skey_ref, bias_ref, ot_ref, *, tq, kb_rows, n_kblocks, causal, first_key, topk):
    j = pl.program_id(1)
    nkb = (j + 1) if causal else n_kblocks
    i32 = jnp.int32

    q_row = j * tq + lax.broadcasted_iota(i32, (1, tq), 1)
    if causal:
        lim = jnp.where(q_row >= first_key, ((q_row >> CHUNK_SHIFT) << CHUNK_SHIFT) + CHUNK, 0)
    else:
        lim = jnp.full((1, tq), n_kblocks * kb_rows, i32)

    def key_rows(kb):
        return kb * kb_rows + lax.broadcasted_iota(i32, (kb_rows, 1), 0)

    def kslice(kb):
        return pl.ds(pl.multiple_of(kb * kb_rows, SUBLANES), kb_rows)

    def score_body(kb, carry):
        ki = kib_ref[kslice(kb), :]
        acc = jnp.zeros((kb_rows, tq), jnp.float32)
        for h in range(IDX_HEADS):
            s = _dot(ki, qit_ref[h * IDX_DIM:(h + 1) * IDX_DIM, :])
            acc = acc + wit_ref[h:h + 1, :] * jnp.maximum(s, 0.0)
        rows = key_rows(kb)
        adm = jnp.logical_and(rows >= first_key, rows < lim)
        score = jnp.where(adm, acc, -jnp.inf)
        bits = lax.bitcast_convert_type(score, i32)
        skey_ref[kslice(kb), :] = bits ^ ((bits >> 31) & 0x7FFFFFFF)
        return carry

    lax.fori_loop(0, nkb, score_body, 0)

    def count(indicator):
        def body(kb, c):
            return c + jnp.sum(indicator(skey_ref[kslice(kb), :], kb), axis=0, keepdims=True)
        return lax.fori_loop(0, nkb, body, jnp.zeros((1, tq), i32))

    def select_body(i, thr):
        cand = thr + lax.shift_left(i32(1), i32(31) - i)
        cnt = count(lambda sk, kb: jnp.where(sk >= cand, 1, 0))
        return jnp.where(cnt >= topk, cand, thr)

    thr = lax.fori_loop(0, 32, select_body, jnp.full((1, tq), INT_MIN, i32))
    need = topk - count(lambda sk, kb: jnp.where(sk > thr, 1, 0))
    n_eq = count(lambda sk, kb: jnp.where(sk == thr, 1, 0))

    has_tie = jnp.logical_and(n_eq > need, thr > NEG_INF_KEY)

    def tie_search():
        def tie_body(i, idx_lim):
            cand = idx_lim + lax.shift_left(i32(1), i32(INDEX_BITS - 1) - i)
            cnt = count(lambda sk, kb: jnp.where(sk == thr, jnp.where(key_rows(kb) < cand, 1, 0), 0))
            return jnp.where(cnt < need, cand, idx_lim)
        return lax.fori_loop(0, INDEX_BITS, tie_body, jnp.zeros((1, tq), i32))

    idx_lim = lax.cond(jnp.max(has_tie.astype(i32)) > 0, tie_search,
                       lambda: jnp.full((1, tq), 2 ** INDEX_BITS, i32))

    def bias_body(kb, carry):
        sk = skey_ref[kslice(kb), :]
        tie_ok = jnp.where(key_rows(kb) <= idx_lim, 0.0, NEG_BIAS)
        b = jnp.where(sk > thr, 0.0, jnp.where(sk == thr, tie_ok, NEG_BIAS))
        bias_ref[kslice(kb), :] = jnp.where(sk > NEG_INF_KEY, b, NEG_BIAS)
        return carry

    lax.fori_loop(0, nkb, bias_body, 0)

    for h in range(N_HEADS):
        g = h // N_GROUPS
        qh = qt_ref[h * HEAD_DIM:(h + 1) * HEAD_DIM, :]

        def att_body(kb, carry, g=g, qh=qh):
            m, l, acc = carry
            logits = _dot(kb_ref[g, kslice(kb), :], qh) + bias_ref[kslice(kb), :]
            m_new = jnp.maximum(m, jnp.max(logits, axis=0, keepdims=True))
            p = jnp.exp(logits - m_new)
            alpha = jnp.exp(m - m_new)
            l = alpha * l + jnp.sum(p, axis=0, keepdims=True)
            vt = vt_ref[kb, g * HEAD_DIM:(g + 1) * HEAD_DIM, :]
            acc = alpha * acc + _dot(vt, p.astype(jnp.bfloat16))
            return m_new, l, acc

        m, l, acc = lax.fori_loop(
            0, nkb, att_body,
            (jnp.full((1, tq), NEG_BIAS, jnp.float32), jnp.zeros((1, tq), jnp.float32),
             jnp.zeros((HEAD_DIM, tq), jnp.float32)))
        ot_ref[h * HEAD_DIM:(h + 1) * HEAD_DIM, :] = jnp.where(m > 0.5 * NEG_BIAS, acc / l, 0.0)

    o_ref[...] = ot_ref[...].T.astype(o_ref.dtype)


def _attention(qt, qit, wit, kb, kib, vt, *, n_seq, lq, lk, tq, kb_rows, causal, first_key, topk):
    nq = lq // tq
    n_kblocks = lk // kb_rows
    kernel = functools.partial(_attn_kernel, tq=tq, kb_rows=kb_rows, n_kblocks=n_kblocks, causal=causal,
                               first_key=first_key, topk=topk)
    return pl.pallas_call(
        kernel,
        grid=(n_seq, nq),
        in_specs=[
            pl.BlockSpec((Q_COLS, tq), lambda b, j: (0, b * nq + j)),
            pl.BlockSpec((IQ_COLS, tq), lambda b, j: (0, b * nq + j)),
            pl.BlockSpec((IW_ROWS, tq), lambda b, j: (0, b * nq + j)),
            pl.BlockSpec((N_KV_HEADS, lk, HEAD_DIM), lambda b, j: (0, b, 0)),
            pl.BlockSpec((lk, IDX_DIM), lambda b, j: (b, 0)),
            pl.BlockSpec((n_kblocks, KV_COLS, kb_rows), lambda b, j: (b, 0, 0)),
        ],
        out_specs=pl.BlockSpec((tq, Q_COLS), lambda b, j: (b * nq + j, 0)),
        out_shape=jax.ShapeDtypeStruct((n_seq * lq, Q_COLS), jnp.bfloat16),
        scratch_shapes=[
            pltpu.VMEM((lk, tq), jnp.int32),
            pltpu.VMEM((lk, tq), jnp.float32),
            pltpu.VMEM((Q_COLS, tq), jnp.float32),
        ],
        compiler_params=pltpu.CompilerParams(dimension_semantics=("arbitrary", "arbitrary"),
                                             vmem_limit_bytes=VMEM_LIMIT),
        name="dsa_attention_causal" if causal else "dsa_attention_full",
    )(qt, qit, wit, kb, kib, vt)


def _mix_out_kernel(x_ref, o_ref, u_ref, halo_ref, cb_ref, g0_ref, g1_ref, cw_ref,
                    wg_ref, wao_ref, wco_ref, wo_ref, x1_ref, *, n_seg):
    x = x_ref[...]
    h = _rms(x, g0_ref[...]).astype(jnp.bfloat16)
    gates = _dot(h, wg_ref[...])
    y_a = _dot(o_ref[...], wao_ref[...])
    u = u_ref[...]
    halo = halo_ref[0]
    t = u.shape[0]
    seg = t // n_seg
    row = lax.broadcasted_iota(jnp.int32, (t, 1), 0)
    u1 = pltpu.roll(u, 1, 0)
    u2 = pltpu.roll(u, 2, 0)
    for s in range(n_seg):
        h6 = halo[s * SUBLANES + 6:s * SUBLANES + 7]
        h7 = halo[s * SUBLANES + 7:s * SUBLANES + 8]
        u1 = jnp.where(row == s * seg, h7, u1)
        u2 = jnp.where(row == s * seg, h6, jnp.where(row == s * seg + 1, h7, u2))
    conv = cw_ref[0:1] * u2 + cw_ref[1:2] * u1 + cw_ref[2:3] * u
    y_b = _dot((cb_ref[...] * conv).astype(jnp.bfloat16), wco_ref[...])
    m = _sigmoid(gates[:, :D_MODEL]) * y_a + _sigmoid(gates[:, D_MODEL:]) * y_b
    a = _dot(m.astype(jnp.bfloat16), wo_ref[...])
    x1_ref[...] = x + _rms(a, g1_ref[...])


def _mix_out(x, o, u, halo, cb, g0, g1, cw, wg, wao, wco, wo):
    n = x.shape[0]
    tm = ROW_TILE
    n_seg = halo.shape[1] // SUBLANES
    full = lambda shape: pl.BlockSpec(shape, lambda i: (0,) * len(shape))
    rows = lambda w: pl.BlockSpec((tm, w), lambda i: (i, 0))
    return pl.pallas_call(
        functools.partial(_mix_out_kernel, n_seg=n_seg),
        grid=(n // tm,),
        in_specs=[rows(D_MODEL), rows(Q_COLS), rows(CONV_DIM),
                  pl.BlockSpec((1, n_seg * SUBLANES, CONV_DIM), lambda i: (i, 0, 0)), rows(CONV_DIM),
                  full((1, D_MODEL)), full((1, D_MODEL)), full((CONV_WIDTH, CONV_DIM)),
                  full((D_MODEL, 2 * D_MODEL)), full((Q_COLS, D_MODEL)), full((CONV_DIM, D_MODEL)),
                  full((D_MODEL, D_MODEL))],
        out_specs=rows(D_MODEL),
        out_shape=jax.ShapeDtypeStruct((n, D_MODEL), jnp.float32),
        compiler_params=pltpu.CompilerParams(dimension_semantics=("arbitrary",), vmem_limit_bytes=VMEM_LIMIT),
        name="mix_out",
    )(x, o, u, halo, cb, g0, g1, cw, wg, wao, wco, wo)


def _mlp_kernel(x_ref, g2_ref, g3_ref, wup_ref, wdn_ref, x2_ref):
    x = x_ref[...]
    h = _rms(x, g2_ref[...]).astype(jnp.bfloat16)
    up = jnp.maximum(_dot(h, wup_ref[...]), 0.0)
    f = _dot((up * up).astype(jnp.bfloat16), wdn_ref[...])
    x2_ref[...] = x + _rms(f, g3_ref[...])


def _mlp(x, g2, g3, wup, wdn):
    n = x.shape[0]
    tm = ROW_TILE
    full = lambda shape: pl.BlockSpec(shape, lambda i: (0,) * len(shape))
    rows = lambda w: pl.BlockSpec((tm, w), lambda i: (i, 0))
    return pl.pallas_call(
        _mlp_kernel,
        grid=(n // tm,),
        in_specs=[rows(D_MODEL), full((1, D_MODEL)), full((1, D_MODEL)), full((D_MODEL, D_FF)),
                  full((D_FF, D_MODEL))],
        out_specs=rows(D_MODEL),
        out_shape=jax.ShapeDtypeStruct((n, D_MODEL), jnp.float32),
        compiler_params=pltpu.CompilerParams(dimension_semantics=("arbitrary",), vmem_limit_bytes=VMEM_LIMIT),
        name="mlp",
    )(x, g2, g3, wup, wdn)


def _layer_weights(w_in, conv_w, w_attn_out, w_conv_out, w_o, w_up, w_down):
    bf16 = jnp.bfloat16
    wt = jnp.concatenate([
        w_in[:, :END_Q] * (HEAD_DIM ** -0.5),
        w_in[:, END_Q:END_V],
        w_in[:, END_V:END_IQ] * (IDX_DIM ** -0.5),
        w_in[:, END_IQ:END_IK],
        w_in[:, END_IK:END_IW] * (IDX_HEADS ** -0.5),
        jnp.zeros((D_MODEL, IW_ROWS - IDX_HEADS), w_in.dtype),
    ], axis=1).T.astype(bf16)
    return dict(
        wt=wt,
        wc=w_in[:, END_IW:END_CX].astype(bf16),
        wg=w_in[:, END_CX:].astype(bf16),
        cw=conv_w,
        wao=w_attn_out.astype(bf16), wco=w_conv_out.astype(bf16), wo=w_o.astype(bf16),
        wup=w_up.astype(bf16), wdn=w_down.astype(bf16))


def _rope_tables(pos):
    inv = jnp.exp(-math.log(ROPE_THETA) * jnp.arange(ROT_HALF, dtype=jnp.float32) * (2.0 / ROT_DIM))
    ang = inv[:, None] * pos.astype(jnp.float32)[None, :]
    return jnp.cos(ang), jnp.sin(ang)


def _halo(u, n_seq, past):
    n = u.shape[0]
    per_seq = n // n_seq
    first = jnp.concatenate([jnp.zeros((n_seq, SUBLANES - (CONV_WIDTH - 1), CONV_DIM), u.dtype), past], axis=1)
    if per_seq >= ROW_TILE:
        tiles = per_seq // ROW_TILE
        tails = u.reshape(n_seq, tiles, ROW_TILE, CONV_DIM)[:, :tiles - 1, ROW_TILE - SUBLANES:]
        return jnp.concatenate([first[:, None], tails], axis=1).reshape(n_seq * tiles, SUBLANES, CONV_DIM)
    n_seg = ROW_TILE // per_seq
    return first.reshape(n_seq // n_seg, n_seg * SUBLANES, CONV_DIM)


def _pad_lanes(a, n_seq, per_seq, width):
    a = a.reshape(a.shape[0], n_seq, per_seq)
    a = jnp.pad(a, ((0, 0), (0, 0), (0, width - per_seq)))
    return a.reshape(a.shape[0], n_seq * width)


def kernel(x_prompt, x_sample, cache_k, cache_v, cache_kidx, state_conv, meta_tokens, norm_gains, w_in, conv_w,
           w_attn_out, w_conv_out, w_o, w_up, w_down):
    f32, bf16 = jnp.float32, jnp.bfloat16
    depth = w_in.shape[0]
    bp, seq, _ = x_prompt.shape
    bs, s_len, _ = x_sample.shape
    past_len = cache_k.shape[2]
    assert seq % Q_TILE == 0 and Q_TILE % CHUNK == 0 and ROW_TILE == Q_TILE and N_META <= Q_TILE
    assert (bs * s_len) % ROW_TILE == 0 and ROW_TILE % s_len == 0
    assert CONV_WIDTH - 1 <= s_len <= LANES and (past_len + s_len) % SUBLANES == 0

    pad = Q_TILE - N_META
    rp = Q_TILE + seq
    xp = jnp.concatenate([
        jnp.zeros((bp, pad, D_MODEL), f32),
        jnp.broadcast_to(meta_tokens[None].astype(f32), (bp, N_META, D_MODEL)),
        x_prompt.astype(f32)], axis=1).reshape(bp * rp, D_MODEL)
    cos_p, sin_p = _rope_tables(jnp.tile(jnp.arange(rp, dtype=jnp.int32) - pad, bp))
    topk_p = min(TOPK_MAX, seq // 4)

    xs = x_sample.astype(f32).reshape(bs * s_len, D_MODEL)
    ls = past_len + s_len
    cos_s, sin_s = _rope_tables(jnp.tile(past_len + jnp.arange(s_len, dtype=jnp.int32), bs))
    topk_s = min(TOPK_MAX, ls // 4)
    conv_zero = jnp.zeros((bp, CONV_WIDTH - 1, CONV_DIM), f32)

    outs = [[] for _ in range(8)]
    for l in range(depth):
        w = _layer_weights(w_in[l], conv_w[l], w_attn_out[l], w_conv_out[l], w_o[l], w_up[l], w_down[l])
        g = [norm_gains[l, i][None, :] for i in range(4)]

        qt, qit, wit, vt, kb, kib, k32, v32, ki32, u, cb = _in_proj(xp, g[0], w["wt"], w["wc"], cos_p, sin_p)
        o = _attention(qt, qit, wit, kb, kib, vt, n_seq=bp, lq=rp, lk=rp, tq=Q_TILE, kb_rows=ROW_TILE,
                       causal=True, first_key=pad, topk=topk_p)
        x1 = _mix_out(xp, o, u, _halo(u, bp, conv_zero), cb, g[0], g[1], w["cw"], w["wg"], w["wao"], w["wco"],
                      w["wo"])
        xp = _mlp(x1, g[2], g[3], w["wup"], w["wdn"])
        outs[0].append(k32.reshape(bp, rp, N_KV_HEADS, HEAD_DIM)[:, pad:])
        outs[1].append(v32.reshape(bp, rp, N_KV_HEADS, HEAD_DIM)[:, pad:])
        outs[2].append(ki32.reshape(bp, rp, IDX_DIM)[:, pad:])
        outs[3].append(u.reshape(bp, rp, CONV_DIM)[:, rp - (CONV_WIDTH - 1):])

        qt, qit, wit, vt, kb, kib, k32, v32, ki32, u, cb = _in_proj(xs, g[0], w["wt"], w["wc"], cos_s, sin_s)
        k_new = k32.reshape(bs, s_len, N_KV_HEADS, HEAD_DIM)
        v_new = v32.reshape(bs, s_len, N_KV_HEADS, HEAD_DIM)
        ki_new = ki32.reshape(bs, s_len, IDX_DIM)
        k_all = jnp.concatenate([cache_k[l], k_new], axis=1).astype(bf16)
        v_all = jnp.concatenate([cache_v[l], v_new], axis=1).astype(bf16)
        ki_all = jnp.concatenate([cache_kidx[l], ki_new], axis=1).astype(bf16)
        o = _attention(
            _pad_lanes(qt, bs, s_len, LANES), _pad_lanes(qit, bs, s_len, LANES), _pad_lanes(wit, bs, s_len, LANES),
            k_all.transpose(2, 0, 1, 3).reshape(N_KV_HEADS, bs * ls, HEAD_DIM),
            ki_all.reshape(bs * ls, IDX_DIM),
            v_all.reshape(bs, ls, KV_COLS).transpose(0, 2, 1),
            n_seq=bs, lq=LANES, lk=ls, tq=LANES, kb_rows=ls, causal=False, first_key=0, topk=topk_s)
        o = o.reshape(bs, LANES, Q_COLS)[:, :s_len].reshape(bs * s_len, Q_COLS)
        x1 = _mix_out(xs, o, u, _halo(u, bs, state_conv[l].astype(f32)), cb, g[0], g[1], w["cw"], w["wg"],
                      w["wao"], w["wco"], w["wo"])
        xs = _mlp(x1, g[2], g[3], w["wup"], w["wdn"])
        outs[4].append(k_new)
        outs[5].append(v_new)
        outs[6].append(ki_new)
        outs[7].append(u.reshape(bs, s_len, CONV_DIM)[:, s_len - (CONV_WIDTH - 1):])

    y_prompt = xp.reshape(bp, rp, D_MODEL)[:, Q_TILE:]
    y_sample = xs.reshape(bs, s_len, D_MODEL)
    return (y_prompt, y_sample) + tuple(jnp.stack(o) for o in outs)
```

```python
import functools
import math

import jax
import jax.numpy as jnp
from jax import lax
from jax.experimental import pallas as pl
from jax.experimental.pallas import tpu as pltpu

D_MODEL = 1024
CHUNK = 64
CHUNK_SHIFT = CHUNK.bit_length() - 1
N_META = 16
HEAD_DIM = 64
N_HEADS = 8
N_KV_HEADS = 2
N_GROUPS = N_HEADS // N_KV_HEADS
IDX_HEADS = 4
IDX_DIM = 64
ROT_DIM = HEAD_DIM // 4
ROT_HALF = ROT_DIM // 2
ROPE_THETA = 500000.0
CONV_DIM = D_MODEL // 2
CONV_WIDTH = 3
D_FF = 4 * D_MODEL
TOPK_MAX = 256
EPS = 1e-6

Q_COLS = N_HEADS * HEAD_DIM
KV_COLS = N_KV_HEADS * HEAD_DIM
IQ_COLS = IDX_HEADS * IDX_DIM
END_Q = Q_COLS
END_K = END_Q + KV_COLS
END_V = END_K + KV_COLS
END_IQ = END_V + IQ_COLS
END_IK = END_IQ + IDX_DIM
END_IW = END_IK + IDX_HEADS
END_CB = END_IW + CONV_DIM
END_CC = END_CB + CONV_DIM
END_CX = END_CC + CONV_DIM
END_GA = END_CX + D_MODEL
D_IN = END_GA + D_MODEL

SUBLANES = 8
LANES = 128
IW_ROWS = SUBLANES
ZT_Q = 0
ZT_K = ZT_Q + Q_COLS
ZT_V = ZT_K + KV_COLS
ZT_IQ = ZT_V + KV_COLS
ZT_IK = ZT_IQ + IQ_COLS
ZT_IW = ZT_IK + IDX_DIM
ZT_ROWS = ZT_IW + IW_ROWS

ROW_TILE = 256
Q_TILE = 256
NEG_BIAS = -1e30
INT_MIN = -(2 ** 31)
NEG_INF_KEY = INT_MIN + 0x7FFFFF
INT_MAX = 2 ** 31 - 1
ZERO_KEY = 0
COUNT_PACK_SHIFT = 16
COUNT_PACK = 1 << COUNT_PACK_SHIFT
TAKE_ALL = 2 ** 30
INTERP_STEPS = 20
STALL_STEPS = 2
SEARCH_CAP = INTERP_STEPS + 34
VMEM_LIMIT = 56 * 1024 * 1024


def _rms(x, g):
    return x * lax.rsqrt(jnp.mean(x * x, axis=-1, keepdims=True) + EPS) * g


def _sigmoid(x):
    return 1.0 / (1.0 + jnp.exp(-x))


def _dot(a, b):
    return jnp.dot(a, b, preferred_element_type=jnp.float32)


def _dot_nt(a, b):
    return lax.dot_general(a, b, (((1,), (1,)), ((), ())), preferred_element_type=jnp.float32)


def _rope_rows(zt, n_heads, cos, sin):
    pieces = []
    for h in range(n_heads):
        o = h * HEAD_DIM
        x1 = zt[o:o + ROT_HALF]
        x2 = zt[o + ROT_HALF:o + ROT_DIM]
        pieces.append(x1 * cos - x2 * sin)
        pieces.append(x2 * cos + x1 * sin)
        pieces.append(zt[o + ROT_DIM:o + HEAD_DIM])
    return jnp.concatenate(pieces, axis=0)


def _in_proj_kernel(x_ref, g_ref, wt_ref, wc_ref, cos_ref, sin_ref,
                    qt_ref, qit_ref, wit_ref, vt_ref, kb_ref, kib_ref,
                    k_ref, v_ref, ki_ref, u_ref, cb_ref):
    h = _rms(x_ref[...], g_ref[...]).astype(jnp.bfloat16)
    zt = _dot_nt(wt_ref[...], h)
    cos = cos_ref[...]
    sin = sin_ref[...]
    qt_ref[...] = _rope_rows(zt[ZT_Q:ZT_K], N_HEADS, cos, sin).astype(jnp.bfloat16)
    k = _rope_rows(zt[ZT_K:ZT_V], N_KV_HEADS, cos, sin).T
    k_ref[...] = k
    for g in range(N_KV_HEADS):
        kb_ref[g] = k[:, g * HEAD_DIM:(g + 1) * HEAD_DIM].astype(jnp.bfloat16)
    vt = zt[ZT_V:ZT_IQ]
    vt_ref[0] = vt.astype(jnp.bfloat16)
    v_ref[...] = vt.T
    qit_ref[...] = _rope_rows(zt[ZT_IQ:ZT_IK], IDX_HEADS, cos, sin).astype(jnp.bfloat16)
    ki = _rope_rows(zt[ZT_IK:ZT_IW], 1, cos, sin).T
    ki_ref[...] = ki
    kib_ref[...] = ki.astype(jnp.bfloat16)
    wit_ref[...] = zt[ZT_IW:ZT_ROWS]
    zc = _dot(h, wc_ref[...])
    cb_ref[...] = zc[:, :CONV_DIM]
    u_ref[...] = zc[:, CONV_DIM:2 * CONV_DIM] * zc[:, 2 * CONV_DIM:]


def _in_proj(x, g0, wt, wc, cos_t, sin_t):
    n = x.shape[0]
    tm = ROW_TILE
    nb = n // tm
    f32, bf16 = jnp.float32, jnp.bfloat16
    full = lambda shape: pl.BlockSpec(shape, lambda i: (0,) * len(shape))
    rows = lambda w: pl.BlockSpec((tm, w), lambda i: (i, 0))
    cols = lambda r: pl.BlockSpec((r, tm), lambda i: (0, i))
    out_shape = (
        jax.ShapeDtypeStruct((Q_COLS, n), bf16),
        jax.ShapeDtypeStruct((IQ_COLS, n), bf16),
        jax.ShapeDtypeStruct((IW_ROWS, n), f32),
        jax.ShapeDtypeStruct((nb, KV_COLS, tm), bf16),
        jax.ShapeDtypeStruct((N_KV_HEADS, n, HEAD_DIM), bf16),
        jax.ShapeDtypeStruct((n, IDX_DIM), bf16),
        jax.ShapeDtypeStruct((n, KV_COLS), f32),
        jax.ShapeDtypeStruct((n, KV_COLS), f32),
        jax.ShapeDtypeStruct((n, IDX_DIM), f32),
        jax.ShapeDtypeStruct((n, CONV_DIM), f32),
        jax.ShapeDtypeStruct((n, CONV_DIM), f32),
    )
    out_specs = (
        cols(Q_COLS), cols(IQ_COLS), cols(IW_ROWS),
        pl.BlockSpec((1, KV_COLS, tm), lambda i: (i, 0, 0)),
        pl.BlockSpec((N_KV_HEADS, tm, HEAD_DIM), lambda i: (0, i, 0)),
        rows(IDX_DIM), rows(KV_COLS), rows(KV_COLS), rows(IDX_DIM), rows(CONV_DIM), rows(CONV_DIM),
    )
    return pl.pallas_call(
        _in_proj_kernel,
        grid=(nb,),
        in_specs=[rows(D_MODEL), full((1, D_MODEL)), full((ZT_ROWS, D_MODEL)), full((D_MODEL, 3 * CONV_DIM)),
                  cols(ROT_HALF), cols(ROT_HALF)],
        out_specs=out_specs,
        out_shape=out_shape,
        compiler_params=pltpu.CompilerParams(dimension_semantics=("arbitrary",), vmem_limit_bytes=VMEM_LIMIT),
        name="in_proj",
    )(x, g0, wt, wc, cos_t, sin_t)


def _attn_kernel(qt_ref, qit_ref, wit_ref, kb_ref, kib_ref, vt_ref, tri_ref, o_ref,
                 skey_ref, bias_ref, ot_ref, qg_ref, acc_ref, *, tq, kb_rows, n_kblocks, causal, first_key, topk):
    j = pl.program_id(1)
    nkb = (j + 1) if causal else n_kblocks
    i32 = jnp.int32

    q_row = j * tq + lax.broadcasted_iota(i32, (1, tq), 1)
    if causal:
        lim = jnp.where(q_row >= first_key, ((q_row >> CHUNK_SHIFT) << CHUNK_SHIFT) + CHUNK, 0)
    else:
        lim = jnp.full((1, tq), n_kblocks * kb_rows, i32)

    def key_rows(kb):
        return kb * kb_rows + lax.broadcasted_iota(i32, (kb_rows, 1), 0)

    def kslice(kb):
        return pl.ds(pl.multiple_of(kb * kb_rows, SUBLANES), kb_rows)

    def to_key(f):
        bits = lax.bitcast_convert_type(f, i32)
        return bits ^ ((bits >> 31) & 0x7FFFFFFF)

    def to_f32(k):
        return lax.bitcast_convert_type(k ^ ((k >> 31) & 0x7FFFFFFF), jnp.float32)

    def score_body(kb, carry):
        kmax, kmin, s1, s2, czero = carry
        ki = kib_ref[kslice(kb), :]
        acc = jnp.zeros((kb_rows, tq), jnp.float32)
        for h in range(IDX_HEADS):
            s = _dot(ki, qit_ref[h * IDX_DIM:(h + 1) * IDX_DIM, :])
            acc = acc + wit_ref[h:h + 1, :] * jnp.maximum(s, 0.0)
        acc = jnp.where(acc == 0.0, 0.0, acc)
        rows = key_rows(kb)
        adm = jnp.logical_and(rows >= first_key, rows < lim)
        sk = to_key(jnp.where(adm, acc, -jnp.inf))
        skey_ref[kslice(kb), :] = sk
        masked = jnp.where(adm, acc, 0.0)
        kmax = jnp.maximum(kmax, jnp.max(sk, axis=0, keepdims=True))
        kmin = jnp.minimum(kmin, jnp.min(jnp.where(adm, sk, INT_MAX), axis=0, keepdims=True))
        s1 = s1 + jnp.sum(masked, axis=0, keepdims=True)
        s2 = s2 + jnp.sum(masked * masked, axis=0, keepdims=True)
        ind = jnp.where(sk > ZERO_KEY, COUNT_PACK + 1, jnp.where(sk == ZERO_KEY, 1, 0))
        czero = czero + jnp.sum(ind, axis=0, keepdims=True)
        return kmax, kmin, s1, s2, czero

    zero = jnp.zeros((1, tq), jnp.float32)
    izero = jnp.zeros((1, tq), i32)
    kmax, kmin, s1, s2, czero = lax.fori_loop(
        0, nkb, score_body,
        (jnp.full((1, tq), INT_MIN, i32), jnp.full((1, tq), INT_MAX, i32), zero, zero, izero))
    c_pos = czero >> COUNT_PACK_SHIFT
    c_nonneg = czero & (COUNT_PACK - 1)

    def count_ge(cand):
        def body(kb, c):
            return c + jnp.sum(jnp.where(skey_ref[kslice(kb), :] >= cand, 1, 0), axis=0, keepdims=True)
        return lax.fori_loop(0, nkb, body, izero)

    n_adm = jnp.maximum(lim - first_key, 0)
    n_f = jnp.maximum(n_adm, 1).astype(jnp.float32)
    mean = s1 / n_f
    sigma = jnp.sqrt(jnp.maximum(s2 / n_f - mean * mean, 0.0))
    tail = jnp.clip(topk / n_f, 1e-6, 1.0 - 1e-6)
    tq_ = jnp.sqrt(-2.0 * jnp.log(jnp.minimum(tail, 1.0 - tail)))
    zq = tq_ - (2.515517 + 0.802853 * tq_ + 0.010328 * tq_ * tq_) / (
        1.0 + 1.432788 * tq_ + 0.189269 * tq_ * tq_ + 0.001308 * tq_ * tq_ * tq_)
    zq = jnp.where(tail < 0.5, zq, -zq)
    step_scale = 1.5 * sigma / jnp.maximum(jnp.abs(zq), 0.5)
    log_k = math.log(topk + 0.5)

    def search_cond(st):
        return jnp.logical_and(st[0] < SEARCH_CAP, st[1] > 0)

    def search_body(st):
        it, _, lo, hi, c_lo, c_hi, lo_real, hi_real, w_lo, w_hi, last, stall, done = st
        f_lo, f_hi = to_f32(lo), to_f32(hi)
        g_lo = jnp.log(c_lo.astype(jnp.float32)) - log_k
        g_hi = log_k - jnp.log(jnp.maximum(c_hi.astype(jnp.float32), 0.5))
        t_in = f_lo + (f_hi - f_lo) * (w_lo * g_lo / (w_lo * g_lo + w_hi * g_hi))
        t_up = f_lo + step_scale * g_lo
        t_dn = f_hi - step_scale * g_hi
        t = jnp.where(lo_real > 0, jnp.where(hi_real > 0, t_in, t_up),
                      jnp.where(hi_real > 0, t_dn, mean + zq * sigma))
        cand = jnp.minimum(jnp.maximum(to_key(t), lo + 1), hi - 1)
        mid = (lo >> 1) + (hi >> 1) + (lo & hi & 1)
        cand = jnp.where(stall >= STALL_STEPS, mid, jnp.where(it >= INTERP_STEPS, mid, cand))
        cnt = count_ge(cand)
        up = jnp.where(done > 0, 0, jnp.where(cnt >= topk, 1, 0))
        dn = jnp.where(done > 0, 0, jnp.where(cnt >= topk, 0, 1))
        both = lo_real * hi_real
        stall = stall + both * jnp.where(cnt == c_lo, 1, jnp.where(cnt == c_hi, 1, 0))
        lo = jnp.where(up > 0, cand, lo)
        c_lo = jnp.where(up > 0, cnt, c_lo)
        lo_real = jnp.maximum(lo_real, up)
        hi = jnp.where(dn > 0, cand, hi)
        c_hi = jnp.where(dn > 0, cnt, c_hi)
        hi_real = jnp.maximum(hi_real, dn)
        w_hi = jnp.where(up > 0, jnp.where(last > 0, 0.5 * w_hi, 1.0), jnp.where(dn > 0, 1.0, w_hi))
        w_lo = jnp.where(dn > 0, jnp.where(last < 0, 0.5 * w_lo, 1.0), jnp.where(up > 0, 1.0, w_lo))
        last = up - dn + (1 - up - dn) * last
        done = jnp.where(c_lo == topk, 1, jnp.where(hi == lo + 1, 1, done))
        return (it + 1, jnp.sum(1 - done), lo, hi, c_lo, c_hi, lo_real, hi_real, w_lo, w_hi, last, stall, done)

    small = n_adm <= topk
    at_zero = jnp.logical_and(c_pos < topk, c_nonneg >= topk)
    above = c_pos >= topk
    lo0 = jnp.where(small, NEG_INF_KEY, jnp.where(at_zero, ZERO_KEY, jnp.where(above, ZERO_KEY + 1, kmin)))
    hi0 = jnp.where(small, NEG_INF_KEY + 1,
                    jnp.where(at_zero, ZERO_KEY + 1, jnp.where(above, kmax + 1, ZERO_KEY)))
    c_lo0 = jnp.where(small, topk, jnp.where(at_zero, c_nonneg, jnp.where(above, c_pos, n_adm)))
    c_hi0 = jnp.where(small, 0, jnp.where(at_zero, c_pos, jnp.where(above, 0, c_nonneg)))
    lo_real0 = jnp.where(small, 0, jnp.where(at_zero, 0, jnp.where(above, 1, 0)))
    hi_real0 = jnp.where(small, 0, jnp.where(at_zero, 0, jnp.where(above, 0, 1)))
    done0 = jnp.where(c_lo0 == topk, 1, jnp.where(hi0 == lo0 + 1, 1, 0))
    one = jnp.ones((1, tq), jnp.float32)
    st = lax.while_loop(search_cond, search_body,
                        (i32(0), jnp.sum(1 - done0), lo0, hi0, c_lo0, c_hi0, lo_real0, hi_real0, one, one,
                         izero, izero, done0))
    thr, hi, c_hi = st[2], st[3], st[5]
    need = jnp.where(small, 0, jnp.where(hi == thr + 1, topk - c_hi, TAKE_ALL)).astype(jnp.float32)

    def bias_body(kb, run):
        sk = skey_ref[kslice(kb), :]
        tie = jnp.where(sk == thr, 1.0, 0.0)
        rank = _dot(tri_ref[...], tie.astype(jnp.bfloat16)) + run
        tie_bias = jnp.where(rank <= need, 0.0, NEG_BIAS)
        bias_ref[kslice(kb), :] = jnp.where(sk > thr, 0.0, jnp.where(sk == thr, tie_bias, NEG_BIAS))
        return rank[kb_rows - 1:kb_rows, :]

    lax.fori_loop(0, nkb, bias_body, zero)

    gw = N_GROUPS * tq
    for g in range(N_KV_HEADS):
        for hh in range(N_GROUPS):
            h = g * N_GROUPS + hh
            qg_ref[g, :, hh * tq:(hh + 1) * tq] = qt_ref[h * HEAD_DIM:(h + 1) * HEAD_DIM, :]
    acc_ref[...] = jnp.zeros_like(acc_ref)

    def att_body(kb, carry):
        bias = bias_ref[kslice(kb), :]
        bias = jnp.concatenate([bias] * N_GROUPS, axis=1)
        new = []
        for g in range(N_KV_HEADS):
            m, l = carry[g]
            logits = _dot(kb_ref[g, kslice(kb), :], qg_ref[g]) + bias
            m_new = jnp.maximum(m, jnp.max(logits, axis=0, keepdims=True))
            p = jnp.exp(logits - m_new)
            alpha = jnp.exp(m - m_new)
            l = alpha * l + jnp.sum(p, axis=0, keepdims=True)
            vt = vt_ref[kb, g * HEAD_DIM:(g + 1) * HEAD_DIM, :]
            acc_ref[g] = alpha * acc_ref[g] + _dot(vt, p.astype(jnp.bfloat16))
            new.append((m_new, l))
        return tuple(new)

    init = (jnp.full((1, gw), NEG_BIAS, jnp.float32), jnp.zeros((1, gw), jnp.float32))
    stats = lax.fori_loop(0, nkb, att_body, (init,) * N_KV_HEADS)
    for g in range(N_KV_HEADS):
        m, l = stats[g]
        og = jnp.where(m > 0.5 * NEG_BIAS, acc_ref[g] / l, 0.0)
        for hh in range(N_GROUPS):
            h = g * N_GROUPS + hh
            ot_ref[h * HEAD_DIM:(h + 1) * HEAD_DIM, :] = og[:, hh * tq:(hh + 1) * tq]

    o_ref[...] = ot_ref[...].T.astype(o_ref.dtype)


def _attention(qt, qit, wit, kb, kib, vt, *, n_seq, lq, lk, tq, kb_rows, causal, first_key, topk):
    nq = lq // tq
    n_kblocks = lk // kb_rows
    assert lk < COUNT_PACK
    tri = jnp.tril(jnp.ones((kb_rows, kb_rows), jnp.bfloat16))
    kernel = functools.partial(_attn_kernel, tq=tq, kb_rows=kb_rows, n_kblocks=n_kblocks, causal=causal,
                               first_key=first_key, topk=topk)
    return pl.pallas_call(
        kernel,
        grid=(n_seq, nq),
        in_specs=[
            pl.BlockSpec((Q_COLS, tq), lambda b, j: (0, b * nq + j)),
            pl.BlockSpec((IQ_COLS, tq), lambda b, j: (0, b * nq + j)),
            pl.BlockSpec((IW_ROWS, tq), lambda b, j: (0, b * nq + j)),
            pl.BlockSpec((N_KV_HEADS, lk, HEAD_DIM), lambda b, j: (0, b, 0)),
            pl.BlockSpec((lk, IDX_DIM), lambda b, j: (b, 0)),
            pl.BlockSpec((n_kblocks, KV_COLS, kb_rows), lambda b, j: (b, 0, 0)),
            pl.BlockSpec((kb_rows, kb_rows), lambda b, j: (0, 0)),
        ],
        out_specs=pl.BlockSpec((tq, Q_COLS), lambda b, j: (b * nq + j, 0)),
        out_shape=jax.ShapeDtypeStruct((n_seq * lq, Q_COLS), jnp.bfloat16),
        scratch_shapes=[
            pltpu.VMEM((lk, tq), jnp.int32),
            pltpu.VMEM((lk, tq), jnp.float32),
            pltpu.VMEM((Q_COLS, tq), jnp.float32),
            pltpu.VMEM((N_KV_HEADS, HEAD_DIM, N_GROUPS * tq), jnp.bfloat16),
            pltpu.VMEM((N_KV_HEADS, HEAD_DIM, N_GROUPS * tq), jnp.float32),
        ],
        compiler_params=pltpu.CompilerParams(dimension_semantics=("arbitrary", "arbitrary"),
                                             vmem_limit_bytes=VMEM_LIMIT),
        name="dsa_attention_causal" if causal else "dsa_attention_full",
    )(qt, qit, wit, kb, kib, vt, tri)


def _mix_out_kernel(x_ref, o_ref, u_ref, halo_ref, cb_ref, g0_ref, g1_ref, cw_ref,
                    wg_ref, wao_ref, wco_ref, wo_ref, x1_ref, *, n_seg):
    x = x_ref[...]
    h = _rms(x, g0_ref[...]).astype(jnp.bfloat16)
    gates = _dot(h, wg_ref[...])
    y_a = _dot(o_ref[...], wao_ref[...])
    u = u_ref[...]
    halo = halo_ref[0]
    t = u.shape[0]
    seg = t // n_seg
    row = lax.broadcasted_iota(jnp.int32, (t, 1), 0)
    u1 = pltpu.roll(u, 1, 0)
    u2 = pltpu.roll(u, 2, 0)
    for s in range(n_seg):
        h6 = halo[s * SUBLANES + 6:s * SUBLANES + 7]
        h7 = halo[s * SUBLANES + 7:s * SUBLANES + 8]
        u1 = jnp.where(row == s * seg, h7, u1)
        u2 = jnp.where(row == s * seg, h6, jnp.where(row == s * seg + 1, h7, u2))
    conv = cw_ref[0:1] * u2 + cw_ref[1:2] * u1 + cw_ref[2:3] * u
    y_b = _dot((cb_ref[...] * conv).astype(jnp.bfloat16), wco_ref[...])
    m = _sigmoid(gates[:, :D_MODEL]) * y_a + _sigmoid(gates[:, D_MODEL:]) * y_b
    a = _dot(m.astype(jnp.bfloat16), wo_ref[...])
    x1_ref[...] = x + _rms(a, g1_ref[...])


def _mix_out(x, o, u, halo, cb, g0, g1, cw, wg, wao, wco, wo):
    n = x.shape[0]
    tm = ROW_TILE
    n_seg = halo.shape[1] // SUBLANES
    full = lambda shape: pl.BlockSpec(shape, lambda i: (0,) * len(shape))
    rows = lambda w: pl.BlockSpec((tm, w), lambda i: (i, 0))
    return pl.pallas_call(
        functools.partial(_mix_out_kernel, n_seg=n_seg),
        grid=(n // tm,),
        in_specs=[rows(D_MODEL), rows(Q_COLS), rows(CONV_DIM),
                  pl.BlockSpec((1, n_seg * SUBLANES, CONV_DIM), lambda i: (i, 0, 0)), rows(CONV_DIM),
                  full((1, D_MODEL)), full((1, D_MODEL)), full((CONV_WIDTH, CONV_DIM)),
                  full((D_MODEL, 2 * D_MODEL)), full((Q_COLS, D_MODEL)), full((CONV_DIM, D_MODEL)),
                  full((D_MODEL, D_MODEL))],
        out_specs=rows(D_MODEL),
        out_shape=jax.ShapeDtypeStruct((n, D_MODEL), jnp.float32),
        compiler_params=pltpu.CompilerParams(dimension_semantics=("arbitrary",), vmem_limit_bytes=VMEM_LIMIT),
        name="mix_out",
    )(x, o, u, halo, cb, g0, g1, cw, wg, wao, wco, wo)


def _mlp_kernel(x_ref, g2_ref, g3_ref, wup_ref, wdn_ref, x2_ref):
    x = x_ref[...]
    h = _rms(x, g2_ref[...]).astype(jnp.bfloat16)
    up = jnp.maximum(_dot(h, wup_ref[...]), 0.0)
    f = _dot((up * up).astype(jnp.bfloat16), wdn_ref[...])
    x2_ref[...] = x + _rms(f, g3_ref[...])


def _mlp(x, g2, g3, wup, wdn):
    n = x.shape[0]
    tm = ROW_TILE
    full = lambda shape: pl.BlockSpec(shape, lambda i: (0,) * len(shape))
    rows = lambda w: pl.BlockSpec((tm, w), lambda i: (i, 0))
    return pl.pallas_call(
        _mlp_kernel,
        grid=(n // tm,),
        in_specs=[rows(D_MODEL), full((1, D_MODEL)), full((1, D_MODEL)), full((D_MODEL, D_FF)),
                  full((D_FF, D_MODEL))],
        out_specs=rows(D_MODEL),
        out_shape=jax.ShapeDtypeStruct((n, D_MODEL), jnp.float32),
        compiler_params=pltpu.CompilerParams(dimension_semantics=("arbitrary",), vmem_limit_bytes=VMEM_LIMIT),
        name="mlp",
    )(x, g2, g3, wup, wdn)


def _layer_weights(w_in, conv_w, w_attn_out, w_conv_out, w_o, w_up, w_down):
    bf16 = jnp.bfloat16
    wt = jnp.concatenate([
        w_in[:, :END_Q] * (HEAD_DIM ** -0.5),
        w_in[:, END_Q:END_V],
        w_in[:, END_V:END_IQ] * (IDX_DIM ** -0.5),
        w_in[:, END_IQ:END_IK],
        w_in[:, END_IK:END_IW] * (IDX_HEADS ** -0.5),
        jnp.zeros((D_MODEL, IW_ROWS - IDX_HEADS), w_in.dtype),
    ], axis=1).T.astype(bf16)
    return dict(
        wt=wt,
        wc=w_in[:, END_IW:END_CX].astype(bf16),
        wg=w_in[:, END_CX:].astype(bf16),
        cw=conv_w,
        wao=w_attn_out.astype(bf16), wco=w_conv_out.astype(bf16), wo=w_o.astype(bf16),
        wup=w_up.astype(bf16), wdn=w_down.astype(bf16))


def _rope_tables(pos):
    inv = jnp.exp(-math.log(ROPE_THETA) * jnp.arange(ROT_HALF, dtype=jnp.float32) * (2.0 / ROT_DIM))
    ang = inv[:, None] * pos.astype(jnp.float32)[None, :]
    return jnp.cos(ang), jnp.sin(ang)


def _halo(u, n_seq, past):
    n = u.shape[0]
    per_seq = n // n_seq
    first = jnp.concatenate([jnp.zeros((n_seq, SUBLANES - (CONV_WIDTH - 1), CONV_DIM), u.dtype), past], axis=1)
    if per_seq >= ROW_TILE:
        tiles = per_seq // ROW_TILE
        tails = u.reshape(n_seq, tiles, ROW_TILE, CONV_DIM)[:, :tiles - 1, ROW_TILE - SUBLANES:]
        return jnp.concatenate([first[:, None], tails], axis=1).reshape(n_seq * tiles, SUBLANES, CONV_DIM)
    n_seg = ROW_TILE // per_seq
    return first.reshape(n_seq // n_seg, n_seg * SUBLANES, CONV_DIM)


def _pad_lanes(a, n_seq, per_seq, width):
    a = a.reshape(a.shape[0], n_seq, per_seq)
    a = jnp.pad(a, ((0, 0), (0, 0), (0, width - per_seq)))
    return a.reshape(a.shape[0], n_seq * width)


def kernel(x_prompt, x_sample, cache_k, cache_v, cache_kidx, state_conv, meta_tokens, norm_gains, w_in, conv_w,
           w_attn_out, w_conv_out, w_o, w_up, w_down):
    f32, bf16 = jnp.float32, jnp.bfloat16
    depth = w_in.shape[0]
    bp, seq, _ = x_prompt.shape
    bs, s_len, _ = x_sample.shape
    past_len = cache_k.shape[2]
    assert seq % Q_TILE == 0 and Q_TILE % CHUNK == 0 and ROW_TILE == Q_TILE and N_META <= Q_TILE
    assert (bs * s_len) % ROW_TILE == 0 and ROW_TILE % s_len == 0
    assert CONV_WIDTH - 1 <= s_len <= LANES and (past_len + s_len) % SUBLANES == 0

    pad = Q_TILE - N_META
    rp = Q_TILE + seq
    xp = jnp.concatenate([
        jnp.zeros((bp, pad, D_MODEL), f32),
        jnp.broadcast_to(meta_tokens[None].astype(f32), (bp, N_META, D_MODEL)),
        x_prompt.astype(f32)], axis=1).reshape(bp * rp, D_MODEL)
    cos_p, sin_p = _rope_tables(jnp.tile(jnp.arange(rp, dtype=jnp.int32) - pad, bp))
    topk_p = min(TOPK_MAX, seq // 4)

    xs = x_sample.astype(f32).reshape(bs * s_len, D_MODEL)
    ls = past_len + s_len
    cos_s, sin_s = _rope_tables(jnp.tile(past_len + jnp.arange(s_len, dtype=jnp.int32), bs))
    topk_s = min(TOPK_MAX, ls // 4)
    conv_zero = jnp.zeros((bp, CONV_WIDTH - 1, CONV_DIM), f32)

    outs = [[] for _ in range(8)]
    for l in range(depth):
        w = _layer_weights(w_in[l], conv_w[l], w_attn_out[l], w_conv_out[l], w_o[l], w_up[l], w_down[l])
        g = [norm_gains[l, i][None, :] for i in range(4)]

        qt, qit, wit, vt, kb, kib, k32, v32, ki32, u, cb = _in_proj(xp, g[0], w["wt"], w["wc"], cos_p, sin_p)
        o = _attention(qt, qit, wit, kb, kib, vt, n_seq=bp, lq=rp, lk=rp, tq=Q_TILE, kb_rows=ROW_TILE,
                       causal=True, first_key=pad, topk=topk_p)
        x1 = _mix_out(xp, o, u, _halo(u, bp, conv_zero), cb, g[0], g[1], w["cw"], w["wg"], w["wao"], w["wco"],
                      w["wo"])
        xp = _mlp(x1, g[2], g[3], w["wup"], w["wdn"])
        outs[0].append(k32.reshape(bp, rp, N_KV_HEADS, HEAD_DIM)[:, pad:])
        outs[1].append(v32.reshape(bp, rp, N_KV_HEADS, HEAD_DIM)[:, pad:])
        outs[2].append(ki32.reshape(bp, rp, IDX_DIM)[:, pad:])
        outs[3].append(u.reshape(bp, rp, CONV_DIM)[:, rp - (CONV_WIDTH - 1):])

        qt, qit, wit, vt, kb, kib, k32, v32, ki32, u, cb = _in_proj(xs, g[0], w["wt"], w["wc"], cos_s, sin_s)
        k_new = k32.reshape(bs, s_len, N_KV_HEADS, HEAD_DIM)
        v_new = v32.reshape(bs, s_len, N_KV_HEADS, HEAD_DIM)
        ki_new = ki32.reshape(bs, s_len, IDX_DIM)
        k_all = jnp.concatenate([cache_k[l], k_new], axis=1).astype(bf16)
        v_all = jnp.concatenate([cache_v[l], v_new], axis=1).astype(bf16)
        ki_all = jnp.concatenate([cache_kidx[l], ki_new], axis=1).astype(bf16)
        o = _attention(
            _pad_lanes(qt, bs, s_len, LANES), _pad_lanes(qit, bs, s_len, LANES), _pad_lanes(wit, bs, s_len, LANES),
            k_all.transpose(2, 0, 1, 3).reshape(N_KV_HEADS, bs * ls, HEAD_DIM),
            ki_all.reshape(bs * ls, IDX_DIM),
            v_all.reshape(bs, ls, KV_COLS).transpose(0, 2, 1),
            n_seq=bs, lq=LANES, lk=ls, tq=LANES, kb_rows=ls, causal=False, first_key=0, topk=topk_s)
        o = o.reshape(bs, LANES, Q_COLS)[:, :s_len].reshape(bs * s_len, Q_COLS)
        x1 = _mix_out(xs, o, u, _halo(u, bs, state_conv[l].astype(f32)), cb, g[0], g[1], w["cw"], w["wg"],
                      w["wao"], w["wco"], w["wo"])
        xs = _mlp(x1, g[2], g[3], w["wup"], w["wdn"])
        outs[4].append(k_new)
        outs[5].append(v_new)
        outs[6].append(ki_new)
        outs[7].append(u.reshape(bs, s_len, CONV_DIM)[:, s_len - (CONV_WIDTH - 1):])

    y_prompt = xp.reshape(bp, rp, D_MODEL)[:, Q_TILE:]
    y_sample = xs.reshape(bs, s_len, D_MODEL)
    return (y_prompt, y_sample) + tuple(jnp.stack(o) for o in outs)
```

```python
import functools
import math

import jax
import jax.numpy as jnp
from jax import lax
from jax.experimental import pallas as pl
from jax.experimental.pallas import tpu as pltpu

D_MODEL = 1024
CHUNK = 64
CHUNK_SHIFT = CHUNK.bit_length() - 1
N_META = 16
HEAD_DIM = 64
N_HEADS = 8
N_KV_HEADS = 2
N_GROUPS = N_HEADS // N_KV_HEADS
IDX_HEADS = 4
IDX_DIM = 64
ROT_DIM = HEAD_DIM // 4
ROT_HALF = ROT_DIM // 2
ROPE_THETA = 500000.0
CONV_DIM = D_MODEL // 2
CONV_WIDTH = 3
D_FF = 4 * D_MODEL
TOPK_MAX = 256
EPS = 1e-6

Q_COLS = N_HEADS * HEAD_DIM
KV_COLS = N_KV_HEADS * HEAD_DIM
IQ_COLS = IDX_HEADS * IDX_DIM
END_Q = Q_COLS
END_K = END_Q + KV_COLS
END_V = END_K + KV_COLS
END_IQ = END_V + IQ_COLS
END_IK = END_IQ + IDX_DIM
END_IW = END_IK + IDX_HEADS
END_CB = END_IW + CONV_DIM
END_CC = END_CB + CONV_DIM
END_CX = END_CC + CONV_DIM
END_GA = END_CX + D_MODEL
D_IN = END_GA + D_MODEL

SUBLANES = 8
LANES = 128
IW_ROWS = SUBLANES
ZT_Q = 0
ZT_K = ZT_Q + Q_COLS
ZT_V = ZT_K + KV_COLS
ZT_IQ = ZT_V + KV_COLS
ZT_IK = ZT_IQ + IQ_COLS
ZT_IW = ZT_IK + IDX_DIM
ZT_ROWS = ZT_IW + IW_ROWS

ROW_TILE = 256
Q_TILE = 256
NEG_BIAS = -1e30
LOG2_E = math.log2(math.e)
INT_MIN = -(2 ** 31)
NEG_INF_KEY = INT_MIN + 0x7FFFFF
MIN_FINITE_KEY = NEG_INF_KEY + 1
POS_INF_KEY = 0x7F800000
INT_MAX = 2 ** 31 - 1
ZERO_KEY = 0
COUNT_PACK_SHIFT = 16
COUNT_PACK = 1 << COUNT_PACK_SHIFT
TAKE_ALL = 2 ** 30
INTERP_STEPS = 20
STALL_STEPS = 2
SEARCH_CAP = INTERP_STEPS + 34
VMEM_LIMIT = 56 * 1024 * 1024


def _rms(x, g):
    return x * lax.rsqrt(jnp.mean(x * x, axis=-1, keepdims=True) + EPS) * g


def _sigmoid(x):
    return 1.0 / (1.0 + jnp.exp(-x))


def _dot(a, b):
    return jnp.dot(a, b, preferred_element_type=jnp.float32)


def _dot_nt(a, b):
    return lax.dot_general(a, b, (((1,), (1,)), ((), ())), preferred_element_type=jnp.float32)


def _rope_rows(zt, n_heads, cos, sin):
    pieces = []
    for h in range(n_heads):
        o = h * HEAD_DIM
        x1 = zt[o:o + ROT_HALF]
        x2 = zt[o + ROT_HALF:o + ROT_DIM]
        pieces.append(x1 * cos - x2 * sin)
        pieces.append(x2 * cos + x1 * sin)
        pieces.append(zt[o + ROT_DIM:o + HEAD_DIM])
    return jnp.concatenate(pieces, axis=0)


def _in_proj_kernel(x_ref, g_ref, wt_ref, wc_ref, cos_ref, sin_ref,
                    qt_ref, qit_ref, wit_ref, vt_ref, kb_ref, kib_ref,
                    k_ref, v_ref, ki_ref, u_ref, cb_ref):
    h = _rms(x_ref[...], g_ref[...]).astype(jnp.bfloat16)
    zt = _dot_nt(wt_ref[...], h)
    cos = cos_ref[...]
    sin = sin_ref[...]
    qt_ref[...] = (_rope_rows(zt[ZT_Q:ZT_K], N_HEADS, cos, sin) * LOG2_E).astype(jnp.bfloat16)
    k = _rope_rows(zt[ZT_K:ZT_V], N_KV_HEADS, cos, sin).T
    k_ref[...] = k
    for g in range(N_KV_HEADS):
        kb_ref[g] = k[:, g * HEAD_DIM:(g + 1) * HEAD_DIM].astype(jnp.bfloat16)
    vt = zt[ZT_V:ZT_IQ]
    vt_ref[0] = vt.astype(jnp.bfloat16)
    v_ref[...] = vt.T
    qit_ref[...] = _rope_rows(zt[ZT_IQ:ZT_IK], IDX_HEADS, cos, sin).astype(jnp.bfloat16)
    ki = _rope_rows(zt[ZT_IK:ZT_IW], 1, cos, sin).T
    ki_ref[...] = ki
    kib_ref[...] = ki.astype(jnp.bfloat16)
    wit_ref[...] = zt[ZT_IW:ZT_ROWS]
    zc = _dot(h, wc_ref[...])
    cb_ref[...] = zc[:, :CONV_DIM]
    u_ref[...] = zc[:, CONV_DIM:2 * CONV_DIM] * zc[:, 2 * CONV_DIM:]


def _in_proj(x, g0, wt, wc, cos_t, sin_t):
    n = x.shape[0]
    tm = ROW_TILE
    nb = n // tm
    f32, bf16 = jnp.float32, jnp.bfloat16
    full = lambda shape: pl.BlockSpec(shape, lambda i: (0,) * len(shape))
    rows = lambda w: pl.BlockSpec((tm, w), lambda i: (i, 0))
    cols = lambda r: pl.BlockSpec((r, tm), lambda i: (0, i))
    out_shape = (
        jax.ShapeDtypeStruct((Q_COLS, n), bf16),
        jax.ShapeDtypeStruct((IQ_COLS, n), bf16),
        jax.ShapeDtypeStruct((IW_ROWS, n), f32),
        jax.ShapeDtypeStruct((nb, KV_COLS, tm), bf16),
        jax.ShapeDtypeStruct((N_KV_HEADS, n, HEAD_DIM), bf16),
        jax.ShapeDtypeStruct((n, IDX_DIM), bf16),
        jax.ShapeDtypeStruct((n, KV_COLS), f32),
        jax.ShapeDtypeStruct((n, KV_COLS), f32),
        jax.ShapeDtypeStruct((n, IDX_DIM), f32),
        jax.ShapeDtypeStruct((n, CONV_DIM), f32),
        jax.ShapeDtypeStruct((n, CONV_DIM), f32),
    )
    out_specs = (
        cols(Q_COLS), cols(IQ_COLS), cols(IW_ROWS),
        pl.BlockSpec((1, KV_COLS, tm), lambda i: (i, 0, 0)),
        pl.BlockSpec((N_KV_HEADS, tm, HEAD_DIM), lambda i: (0, i, 0)),
        rows(IDX_DIM), rows(KV_COLS), rows(KV_COLS), rows(IDX_DIM), rows(CONV_DIM), rows(CONV_DIM),
    )
    return pl.pallas_call(
        _in_proj_kernel,
        grid=(nb,),
        in_specs=[rows(D_MODEL), full((1, D_MODEL)), full((ZT_ROWS, D_MODEL)), full((D_MODEL, 3 * CONV_DIM)),
                  cols(ROT_HALF), cols(ROT_HALF)],
        out_specs=out_specs,
        out_shape=out_shape,
        compiler_params=pltpu.CompilerParams(dimension_semantics=("arbitrary",), vmem_limit_bytes=VMEM_LIMIT),
        name="in_proj",
    )(x, g0, wt, wc, cos_t, sin_t)


def _attn_kernel(qt_ref, qit_ref, wit_ref, kb_ref, kib_ref, vt_ref, tri_ref, o_ref,
                 skey_ref, bias_ref, ot_ref, qg_ref, acc_ref, lg_ref, *, tq, kb_rows, n_kblocks, count_blocks, causal,
                 first_key, topk):
    j = pl.program_id(1)
    nkb = (j + 1) if causal else n_kblocks
    i32 = jnp.int32

    q_row = j * tq + lax.broadcasted_iota(i32, (1, tq), 1)
    if causal:
        lim = jnp.where(q_row >= first_key, ((q_row >> CHUNK_SHIFT) << CHUNK_SHIFT) + CHUNK, 0)
    else:
        lim = jnp.full((1, tq), n_kblocks * kb_rows, i32)

    def key_rows(kb):
        return kb * kb_rows + lax.broadcasted_iota(i32, (kb_rows, 1), 0)

    def kslice(kb):
        return pl.ds(pl.multiple_of(kb * kb_rows, SUBLANES), kb_rows)

    def to_key(f):
        bits = lax.bitcast_convert_type(f, i32)
        return bits ^ ((bits >> 31) & 0x7FFFFFFF)

    def to_f32(k):
        return lax.bitcast_convert_type(k ^ ((k >> 31) & 0x7FFFFFFF), jnp.float32)

    def score_body(kb, carry):
        s2, czero = carry
        ki = kib_ref[kslice(kb), :]
        acc = jnp.zeros((kb_rows, tq), jnp.float32)
        for h in range(IDX_HEADS):
            s = _dot(ki, qit_ref[h * IDX_DIM:(h + 1) * IDX_DIM, :])
            acc = acc + wit_ref[h:h + 1, :] * jnp.maximum(s, 0.0)
        acc = jnp.where(acc == 0.0, 0.0, acc)
        rows = key_rows(kb)
        adm = jnp.logical_and(rows >= first_key, rows < lim)
        sk = to_key(jnp.where(adm, acc, -jnp.inf))
        skey_ref[kslice(kb), :] = sk
        masked = jnp.where(adm, acc, 0.0)
        s2 = s2 + jnp.sum(masked * masked, axis=0, keepdims=True)
        ind = jnp.where(sk > ZERO_KEY, COUNT_PACK + 1, jnp.where(sk == ZERO_KEY, 1, 0))
        czero = czero + jnp.sum(ind, axis=0, keepdims=True)
        return s2, czero

    zero = jnp.zeros((1, tq), jnp.float32)
    izero = jnp.zeros((1, tq), i32)
    s2, czero = lax.fori_loop(0, nkb, score_body, (zero, izero))
    c_pos = czero >> COUNT_PACK_SHIFT
    c_nonneg = czero & (COUNT_PACK - 1)

    cslice_rows = count_blocks * kb_rows
    if count_blocks > 1:
        skey_ref[kslice(nkb), :] = jnp.full((kb_rows, tq), INT_MIN, i32)

    def count_ge(cand):
        def body(c, acc):
            sk = skey_ref[pl.ds(pl.multiple_of(c * cslice_rows, SUBLANES), cslice_rows), :]
            return acc + jnp.sum(jnp.where(sk >= cand, 1, 0), axis=0, keepdims=True)
        return lax.fori_loop(0, (nkb + count_blocks - 1) // count_blocks, body, izero)

    n_adm = jnp.maximum(lim - first_key, 0)
    n_f = jnp.maximum(n_adm, 1).astype(jnp.float32)
    sigma = jnp.sqrt(s2 / n_f)
    tail = jnp.clip(topk / n_f, 1e-6, 1.0 - 1e-6)
    tq_ = jnp.sqrt(-2.0 * jnp.log(jnp.minimum(tail, 1.0 - tail)))
    zq = tq_ - (2.515517 + 0.802853 * tq_ + 0.010328 * tq_ * tq_) / (
        1.0 + 1.432788 * tq_ + 0.189269 * tq_ * tq_ + 0.001308 * tq_ * tq_ * tq_)
    zq = jnp.where(tail < 0.5, zq, -zq)
    step_scale = 1.5 * sigma / jnp.maximum(jnp.abs(zq), 0.5)
    log_k = math.log(topk + 0.5)

    def search_cond(st):
        return jnp.logical_and(st[0] < SEARCH_CAP, st[1] > 0)

    def search_body(st):
        it, _, lo, hi, c_lo, c_hi, lo_real, hi_real, w_lo, w_hi, last, stall, done = st
        f_lo, f_hi = to_f32(lo), to_f32(hi)
        g_lo = jnp.log(c_lo.astype(jnp.float32)) - log_k
        g_hi = log_k - jnp.log(jnp.maximum(c_hi.astype(jnp.float32), 0.5))
        t_in = f_lo + (f_hi - f_lo) * (w_lo * g_lo / (w_lo * g_lo + w_hi * g_hi))
        t_up = f_lo + step_scale * g_lo
        t_dn = f_hi - step_scale * g_hi
        t = jnp.where(lo_real > 0, jnp.where(hi_real > 0, t_in, t_up), t_dn)
        cand = jnp.minimum(jnp.maximum(to_key(t), lo + 1), hi - 1)
        mid = (lo >> 1) + (hi >> 1) + (lo & hi & 1)
        cand = jnp.where(stall >= STALL_STEPS, mid, jnp.where(it >= INTERP_STEPS, mid, cand))
        cnt = count_ge(cand)
        up = jnp.where(done > 0, 0, jnp.where(cnt >= topk, 1, 0))
        dn = jnp.where(done > 0, 0, jnp.where(cnt >= topk, 0, 1))
        both = lo_real * hi_real
        stall = stall + both * jnp.where(cnt == c_lo, 1, jnp.where(cnt == c_hi, 1, 0))
        lo = jnp.where(up > 0, cand, lo)
        c_lo = jnp.where(up > 0, cnt, c_lo)
        lo_real = jnp.maximum(lo_real, up)
        hi = jnp.where(dn > 0, cand, hi)
        c_hi = jnp.where(dn > 0, cnt, c_hi)
        hi_real = jnp.maximum(hi_real, dn)
        w_hi = jnp.where(up > 0, jnp.where(last > 0, 0.5 * w_hi, 1.0), jnp.where(dn > 0, 1.0, w_hi))
        w_lo = jnp.where(dn > 0, jnp.where(last < 0, 0.5 * w_lo, 1.0), jnp.where(up > 0, 1.0, w_lo))
        last = up - dn + (1 - up - dn) * last
        done = jnp.where(c_lo == topk, 1, jnp.where(hi == lo + 1, 1, done))
        return (it + 1, jnp.sum(1 - done), lo, hi, c_lo, c_hi, lo_real, hi_real, w_lo, w_hi, last, stall, done)

    small = n_adm <= topk
    at_zero = jnp.logical_and(c_pos < topk, c_nonneg >= topk)
    above = c_pos >= topk
    lo0 = jnp.where(small, NEG_INF_KEY,
                    jnp.where(at_zero, ZERO_KEY, jnp.where(above, ZERO_KEY + 1, MIN_FINITE_KEY)))
    hi0 = jnp.where(small, NEG_INF_KEY + 1,
                    jnp.where(at_zero, ZERO_KEY + 1, jnp.where(above, POS_INF_KEY, ZERO_KEY)))
    c_lo0 = jnp.where(small, topk, jnp.where(at_zero, c_nonneg, jnp.where(above, c_pos, n_adm)))
    c_hi0 = jnp.where(small, 0, jnp.where(at_zero, c_pos, jnp.where(above, 0, c_nonneg)))
    lo_real0 = jnp.where(small, 0, jnp.where(at_zero, 0, jnp.where(above, 1, 0)))
    hi_real0 = jnp.where(small, 0, jnp.where(at_zero, 0, jnp.where(above, 0, 1)))
    done0 = jnp.where(c_lo0 == topk, 1, jnp.where(hi0 == lo0 + 1, 1, 0))
    one = jnp.ones((1, tq), jnp.float32)
    st = lax.while_loop(search_cond, search_body,
                        (i32(0), jnp.sum(1 - done0), lo0, hi0, c_lo0, c_hi0, lo_real0, hi_real0, one, one,
                         izero, izero, done0))
    thr, hi, c_hi = st[2], st[3], st[5]
    need = jnp.where(small, 0, jnp.where(hi == thr + 1, topk - c_hi, TAKE_ALL)).astype(jnp.float32)

    def bias_body(kb, run):
        sk = skey_ref[kslice(kb), :]
        tie = jnp.where(sk == thr, 1.0, 0.0)
        rank = _dot(tri_ref[...], tie.astype(jnp.bfloat16)) + run
        tie_bias = jnp.where(rank <= need, 0.0, NEG_BIAS)
        bias_ref[kslice(kb), :] = jnp.where(sk > thr, 0.0, jnp.where(sk == thr, tie_bias, NEG_BIAS))
        return rank[kb_rows - 1:kb_rows, :]

    lax.fori_loop(0, nkb, bias_body, zero)

    gw = N_GROUPS * tq
    for g in range(N_KV_HEADS):
        for hh in range(N_GROUPS):
            h = g * N_GROUPS + hh
            qg_ref[g, :, hh * tq:(hh + 1) * tq] = qt_ref[h * HEAD_DIM:(h + 1) * HEAD_DIM, :]
    acc_ref[...] = jnp.zeros_like(acc_ref)

    def logits_stage(kb, g, slot):
        bias = bias_ref[kslice(kb), :]
        bias = jnp.concatenate([bias] * N_GROUPS, axis=1)
        logits = _dot(kb_ref[g, kslice(kb), :], qg_ref[g]) + bias
        lg_ref[slot, g] = logits
        return jnp.max(logits, axis=0, keepdims=True)

    def value_stage(kb, g, slot, m, m_blk):
        m_new = jnp.maximum(m, m_blk)
        p = jnp.exp2(lg_ref[slot, g] - m_new).astype(jnp.bfloat16)
        alpha = jnp.exp2(m - m_new)
        ones = jnp.ones((SUBLANES, kb_rows), jnp.bfloat16)
        vt = jnp.concatenate([vt_ref[kb, g * HEAD_DIM:(g + 1) * HEAD_DIM, :], ones], axis=0)
        acc_ref[g] = alpha * acc_ref[g] + _dot(vt, p)
        return m_new

    groups = range(N_KV_HEADS)

    def step(kb, slot, ms, m_blks, prefetch):
        nxt = tuple(logits_stage(kb + 1, g, 1 - slot) for g in groups) if prefetch else None
        return tuple(value_stage(kb, g, slot, ms[g], m_blks[g]) for g in groups), nxt

    def pair_body(i, carry):
        ms, m_blks = carry
        ms, m_blks = step(2 * i, 0, ms, m_blks, True)
        return step(2 * i + 1, 1, ms, m_blks, True)

    m0 = jnp.full((1, gw), NEG_BIAS, jnp.float32)
    carry = ((m0,) * N_KV_HEADS, tuple(logits_stage(0, g, 0) for g in groups))
    n_pairs = (nkb - 1) // 2
    ms, m_blks = lax.fori_loop(0, n_pairs, pair_body, carry)
    last = 2 * n_pairs

    def two_left():
        ms1, m_blks1 = step(last, 0, ms, m_blks, True)
        return step(last + 1, 1, ms1, m_blks1, False)[0]

    ms = lax.cond(nkb - last == 2, two_left, lambda: step(last, 0, ms, m_blks, False)[0])
    for g in groups:
        m = ms[g]
        acc = acc_ref[g]
        og = jnp.where(m > 0.5 * NEG_BIAS, acc[:HEAD_DIM] / acc[HEAD_DIM:HEAD_DIM + 1], 0.0)
        for hh in range(N_GROUPS):
            h = g * N_GROUPS + hh
            ot_ref[h * HEAD_DIM:(h + 1) * HEAD_DIM, :] = og[:, hh * tq:(hh + 1) * tq]

    o_ref[...] = ot_ref[...].T.astype(o_ref.dtype)


def _attention(qt, qit, wit, kb, kib, vt, *, n_seq, lq, lk, tq, kb_rows, causal, first_key, topk):
    nq = lq // tq
    n_kblocks = lk // kb_rows
    assert lk < COUNT_PACK
    tri = jnp.tril(jnp.ones((kb_rows, kb_rows), jnp.bfloat16))
    count_blocks = 2 if causal else 1
    kernel = functools.partial(_attn_kernel, tq=tq, kb_rows=kb_rows, n_kblocks=n_kblocks,
                               count_blocks=count_blocks, causal=causal, first_key=first_key, topk=topk)
    return pl.pallas_call(
        kernel,
        grid=(n_seq, nq),
        in_specs=[
            pl.BlockSpec((Q_COLS, tq), lambda b, j: (0, b * nq + j)),
            pl.BlockSpec((IQ_COLS, tq), lambda b, j: (0, b * nq + j)),
            pl.BlockSpec((IW_ROWS, tq), lambda b, j: (0, b * nq + j)),
            pl.BlockSpec((N_KV_HEADS, lk, HEAD_DIM), lambda b, j: (0, b, 0)),
            pl.BlockSpec((lk, IDX_DIM), lambda b, j: (b, 0)),
            pl.BlockSpec((n_kblocks, KV_COLS, kb_rows), lambda b, j: (b, 0, 0)),
            pl.BlockSpec((kb_rows, kb_rows), lambda b, j: (0, 0)),
        ],
        out_specs=pl.BlockSpec((tq, Q_COLS), lambda b, j: (b * nq + j, 0)),
        out_shape=jax.ShapeDtypeStruct((n_seq * lq, Q_COLS), jnp.bfloat16),
        scratch_shapes=[
            pltpu.VMEM((lk + (count_blocks - 1) * kb_rows, tq), jnp.int32),
            pltpu.VMEM((lk, tq), jnp.float32),
            pltpu.VMEM((Q_COLS, tq), jnp.float32),
            pltpu.VMEM((N_KV_HEADS, HEAD_DIM, N_GROUPS * tq), jnp.bfloat16),
            pltpu.VMEM((N_KV_HEADS, HEAD_DIM + SUBLANES, N_GROUPS * tq), jnp.float32),
            pltpu.VMEM((2, N_KV_HEADS, kb_rows, N_GROUPS * tq), jnp.float32),
        ],
        compiler_params=pltpu.CompilerParams(dimension_semantics=("arbitrary", "arbitrary"),
                                             vmem_limit_bytes=VMEM_LIMIT),
        name="dsa_attention_causal" if causal else "dsa_attention_full",
    )(qt, qit, wit, kb, kib, vt, tri)


def _mix_out_kernel(x_ref, o_ref, u_ref, halo_ref, cb_ref, g0_ref, g1_ref, cw_ref,
                    wg_ref, wao_ref, wco_ref, wo_ref, x1_ref, *, n_seg):
    x = x_ref[...]
    h = _rms(x, g0_ref[...]).astype(jnp.bfloat16)
    gates = _dot(h, wg_ref[...])
    y_a = _dot(o_ref[...], wao_ref[...])
    u = u_ref[...]
    halo = halo_ref[0]
    t = u.shape[0]
    seg = t // n_seg
    row = lax.broadcasted_iota(jnp.int32, (t, 1), 0)
    u1 = pltpu.roll(u, 1, 0)
    u2 = pltpu.roll(u, 2, 0)
    for s in range(n_seg):
        h6 = halo[s * SUBLANES + 6:s * SUBLANES + 7]
        h7 = halo[s * SUBLANES + 7:s * SUBLANES + 8]
        u1 = jnp.where(row == s * seg, h7, u1)
        u2 = jnp.where(row == s * seg, h6, jnp.where(row == s * seg + 1, h7, u2))
    conv = cw_ref[0:1] * u2 + cw_ref[1:2] * u1 + cw_ref[2:3] * u
    y_b = _dot((cb_ref[...] * conv).astype(jnp.bfloat16), wco_ref[...])
    m = _sigmoid(gates[:, :D_MODEL]) * y_a + _sigmoid(gates[:, D_MODEL:]) * y_b
    a = _dot(m.astype(jnp.bfloat16), wo_ref[...])
    x1_ref[...] = x + _rms(a, g1_ref[...])


def _mix_out(x, o, u, halo, cb, g0, g1, cw, wg, wao, wco, wo):
    n = x.shape[0]
    tm = ROW_TILE
    n_seg = halo.shape[1] // SUBLANES
    full = lambda shape: pl.BlockSpec(shape, lambda i: (0,) * len(shape))
    rows = lambda w: pl.BlockSpec((tm, w), lambda i: (i, 0))
    return pl.pallas_call(
        functools.partial(_mix_out_kernel, n_seg=n_seg),
        grid=(n // tm,),
        in_specs=[rows(D_MODEL), rows(Q_COLS), rows(CONV_DIM),
                  pl.BlockSpec((1, n_seg * SUBLANES, CONV_DIM), lambda i: (i, 0, 0)), rows(CONV_DIM),
                  full((1, D_MODEL)), full((1, D_MODEL)), full((CONV_WIDTH, CONV_DIM)),
                  full((D_MODEL, 2 * D_MODEL)), full((Q_COLS, D_MODEL)), full((CONV_DIM, D_MODEL)),
                  full((D_MODEL, D_MODEL))],
        out_specs=rows(D_MODEL),
        out_shape=jax.ShapeDtypeStruct((n, D_MODEL), jnp.float32),
        compiler_params=pltpu.CompilerParams(dimension_semantics=("arbitrary",), vmem_limit_bytes=VMEM_LIMIT),
        name="mix_out",
    )(x, o, u, halo, cb, g0, g1, cw, wg, wao, wco, wo)


def _mlp_kernel(x_ref, g2_ref, g3_ref, wup_ref, wdn_ref, x2_ref):
    x = x_ref[...]
    h = _rms(x, g2_ref[...]).astype(jnp.bfloat16)
    up = jnp.maximum(_dot(h, wup_ref[...]), 0.0)
    f = _dot((up * up).astype(jnp.bfloat16), wdn_ref[...])
    x2_ref[...] = x + _rms(f, g3_ref[...])


def _mlp(x, g2, g3, wup, wdn):
    n = x.shape[0]
    tm = ROW_TILE
    full = lambda shape: pl.BlockSpec(shape, lambda i: (0,) * len(shape))
    rows = lambda w: pl.BlockSpec((tm, w), lambda i: (i, 0))
    return pl.pallas_call(
        _mlp_kernel,
        grid=(n // tm,),
        in_specs=[rows(D_MODEL), full((1, D_MODEL)), full((1, D_MODEL)), full((D_MODEL, D_FF)),
                  full((D_FF, D_MODEL))],
        out_specs=rows(D_MODEL),
        out_shape=jax.ShapeDtypeStruct((n, D_MODEL), jnp.float32),
        compiler_params=pltpu.CompilerParams(dimension_semantics=("arbitrary",), vmem_limit_bytes=VMEM_LIMIT),
        name="mlp",
    )(x, g2, g3, wup, wdn)


def _layer_weights(w_in, conv_w, w_attn_out, w_conv_out, w_o, w_up, w_down):
    bf16 = jnp.bfloat16
    wt = jnp.concatenate([
        w_in[:, :END_Q] * (HEAD_DIM ** -0.5),
        w_in[:, END_Q:END_V],
        w_in[:, END_V:END_IQ] * (IDX_DIM ** -0.5),
        w_in[:, END_IQ:END_IK],
        w_in[:, END_IK:END_IW] * (IDX_HEADS ** -0.5),
        jnp.zeros((D_MODEL, IW_ROWS - IDX_HEADS), w_in.dtype),
    ], axis=1).T.astype(bf16)
    return dict(
        wt=wt,
        wc=w_in[:, END_IW:END_CX].astype(bf16),
        wg=w_in[:, END_CX:].astype(bf16),
        cw=conv_w,
        wao=w_attn_out.astype(bf16), wco=w_conv_out.astype(bf16), wo=w_o.astype(bf16),
        wup=w_up.astype(bf16), wdn=w_down.astype(bf16))


def _rope_tables(pos):
    inv = jnp.exp(-math.log(ROPE_THETA) * jnp.arange(ROT_HALF, dtype=jnp.float32) * (2.0 / ROT_DIM))
    ang = inv[:, None] * pos.astype(jnp.float32)[None, :]
    return jnp.cos(ang), jnp.sin(ang)


def _halo(u, n_seq, past):
    n = u.shape[0]
    per_seq = n // n_seq
    first = jnp.concatenate([jnp.zeros((n_seq, SUBLANES - (CONV_WIDTH - 1), CONV_DIM), u.dtype), past], axis=1)
    if per_seq >= ROW_TILE:
        tiles = per_seq // ROW_TILE
        tails = u.reshape(n_seq, tiles, ROW_TILE, CONV_DIM)[:, :tiles - 1, ROW_TILE - SUBLANES:]
        return jnp.concatenate([first[:, None], tails], axis=1).reshape(n_seq * tiles, SUBLANES, CONV_DIM)
    n_seg = ROW_TILE // per_seq
    return first.reshape(n_seq // n_seg, n_seg * SUBLANES, CONV_DIM)


def _pad_lanes(a, n_seq, per_seq, width):
    a = a.reshape(a.shape[0], n_seq, per_seq)
    a = jnp.pad(a, ((0, 0), (0, 0), (0, width - per_seq)))
    return a.reshape(a.shape[0], n_seq * width)


def kernel(x_prompt, x_sample, cache_k, cache_v, cache_kidx, state_conv, meta_tokens, norm_gains, w_in, conv_w,
           w_attn_out, w_conv_out, w_o, w_up, w_down):
    f32, bf16 = jnp.float32, jnp.bfloat16
    depth = w_in.shape[0]
    bp, seq, _ = x_prompt.shape
    bs, s_len, _ = x_sample.shape
    past_len = cache_k.shape[2]
    assert seq % Q_TILE == 0 and Q_TILE % CHUNK == 0 and ROW_TILE == Q_TILE and N_META <= Q_TILE
    assert (bs * s_len) % ROW_TILE == 0 and ROW_TILE % s_len == 0
    assert CONV_WIDTH - 1 <= s_len <= LANES and (past_len + s_len) % SUBLANES == 0

    pad = Q_TILE - N_META
    rp = Q_TILE + seq
    xp = jnp.concatenate([
        jnp.zeros((bp, pad, D_MODEL), f32),
        jnp.broadcast_to(meta_tokens[None].astype(f32), (bp, N_META, D_MODEL)),
        x_prompt.astype(f32)], axis=1).reshape(bp * rp, D_MODEL)
    cos_p, sin_p = _rope_tables(jnp.tile(jnp.arange(rp, dtype=jnp.int32) - pad, bp))
    topk_p = min(TOPK_MAX, seq // 4)

    xs = x_sample.astype(f32).reshape(bs * s_len, D_MODEL)
    ls = past_len + s_len
    cos_s, sin_s = _rope_tables(jnp.tile(past_len + jnp.arange(s_len, dtype=jnp.int32), bs))
    topk_s = min(TOPK_MAX, ls // 4)
    conv_zero = jnp.zeros((bp, CONV_WIDTH - 1, CONV_DIM), f32)

    outs = [[] for _ in range(8)]
    for l in range(depth):
        w = _layer_weights(w_in[l], conv_w[l], w_attn_out[l], w_conv_out[l], w_o[l], w_up[l], w_down[l])
        g = [norm_gains[l, i][None, :] for i in range(4)]

        qt, qit, wit, vt, kb, kib, k32, v32, ki32, u, cb = _in_proj(xp, g[0], w["wt"], w["wc"], cos_p, sin_p)
        o = _attention(qt, qit, wit, kb, kib, vt, n_seq=bp, lq=rp, lk=rp, tq=Q_TILE, kb_rows=ROW_TILE,
                       causal=True, first_key=pad, topk=topk_p)
        x1 = _mix_out(xp, o, u, _halo(u, bp, conv_zero), cb, g[0], g[1], w["cw"], w["wg"], w["wao"], w["wco"],
                      w["wo"])
        xp = _mlp(x1, g[2], g[3], w["wup"], w["wdn"])
        outs[0].append(k32.reshape(bp, rp, N_KV_HEADS, HEAD_DIM)[:, pad:])
        outs[1].append(v32.reshape(bp, rp, N_KV_HEADS, HEAD_DIM)[:, pad:])
        outs[2].append(ki32.reshape(bp, rp, IDX_DIM)[:, pad:])
        outs[3].append(u.reshape(bp, rp, CONV_DIM)[:, rp - (CONV_WIDTH - 1):])

        qt, qit, wit, vt, kb, kib, k32, v32, ki32, u, cb = _in_proj(xs, g[0], w["wt"], w["wc"], cos_s, sin_s)
        k_new = k32.reshape(bs, s_len, N_KV_HEADS, HEAD_DIM)
        v_new = v32.reshape(bs, s_len, N_KV_HEADS, HEAD_DIM)
        ki_new = ki32.reshape(bs, s_len, IDX_DIM)
        k_all = jnp.concatenate([cache_k[l], k_new], axis=1).astype(bf16)
        v_all = jnp.concatenate([cache_v[l], v_new], axis=1).astype(bf16)
        ki_all = jnp.concatenate([cache_kidx[l], ki_new], axis=1).astype(bf16)
        o = _attention(
            _pad_lanes(qt, bs, s_len, LANES), _pad_lanes(qit, bs, s_len, LANES), _pad_lanes(wit, bs, s_len, LANES),
            k_all.transpose(2, 0, 1, 3).reshape(N_KV_HEADS, bs * ls, HEAD_DIM),
            ki_all.reshape(bs * ls, IDX_DIM),
            v_all.reshape(bs, ls, KV_COLS).transpose(0, 2, 1),
            n_seq=bs, lq=LANES, lk=ls, tq=LANES, kb_rows=ls, causal=False, first_key=0, topk=topk_s)
        o = o.reshape(bs, LANES, Q_COLS)[:, :s_len].reshape(bs * s_len, Q_COLS)
        x1 = _mix_out(xs, o, u, _halo(u, bs, state_conv[l].astype(f32)), cb, g[0], g[1], w["cw"], w["wg"],
                      w["wao"], w["wco"], w["wo"])
        xs = _mlp(x1, g[2], g[3], w["wup"], w["wdn"])
        outs[4].append(k_new)
        outs[5].append(v_new)
        outs[6].append(ki_new)
        outs[7].append(u.reshape(bs, s_len, CONV_DIM)[:, s_len - (CONV_WIDTH - 1):])

    y_prompt = xp.reshape(bp, rp, D_MODEL)[:, Q_TILE:]
    y_sample = xs.reshape(bs, s_len, D_MODEL)
    return (y_prompt, y_sample) + tuple(jnp.stack(o) for o in outs)
```

```python
import functools
import math

import jax
import jax.numpy as jnp
from jax import lax
from jax.experimental import pallas as pl
from jax.experimental.pallas import tpu as pltpu

D_MODEL = 1024
CHUNK = 64
CHUNK_SHIFT = CHUNK.bit_length() - 1
N_META = 16
HEAD_DIM = 64
N_HEADS = 8
N_KV_HEADS = 2
N_GROUPS = N_HEADS // N_KV_HEADS
IDX_HEADS = 4
IDX_DIM = 64
ROT_DIM = HEAD_DIM // 4
ROT_HALF = ROT_DIM // 2
ROPE_THETA = 500000.0
CONV_DIM = D_MODEL // 2
CONV_WIDTH = 3
D_FF = 4 * D_MODEL
TOPK_MAX = 256
EPS = 1e-6

Q_COLS = N_HEADS * HEAD_DIM
KV_COLS = N_KV_HEADS * HEAD_DIM
IQ_COLS = IDX_HEADS * IDX_DIM
END_Q = Q_COLS
END_K = END_Q + KV_COLS
END_V = END_K + KV_COLS
END_IQ = END_V + IQ_COLS
END_IK = END_IQ + IDX_DIM
END_IW = END_IK + IDX_HEADS
END_CB = END_IW + CONV_DIM
END_CC = END_CB + CONV_DIM
END_CX = END_CC + CONV_DIM
END_GA = END_CX + D_MODEL
D_IN = END_GA + D_MODEL

SUBLANES = 8
LANES = 128
IW_ROWS = SUBLANES
ZT_Q = 0
ZT_K = ZT_Q + Q_COLS
ZT_V = ZT_K + KV_COLS
ZT_IQ = ZT_V + KV_COLS
ZT_IK = ZT_IQ + IQ_COLS
ZT_IW = ZT_IK + IDX_DIM
ZT_ROWS = ZT_IW + IW_ROWS

ROW_TILE = 256
Q_TILE = 256
NEG_BIAS = -1e30
LOG2_E = math.log2(math.e)
INT_MIN = -(2 ** 31)
NEG_INF_KEY = INT_MIN + 0x7FFFFF
MIN_FINITE_KEY = NEG_INF_KEY + 1
POS_INF_KEY = 0x7F800000
INT_MAX = 2 ** 31 - 1
ZERO_KEY = 0
TAKE_ALL = 2 ** 30
INTERP_STEPS = 20
STALL_STEPS = 2
SEARCH_CAP = INTERP_STEPS + 34
VMEM_LIMIT = 56 * 1024 * 1024


def _rms(x, g):
    return x * lax.rsqrt(jnp.mean(x * x, axis=-1, keepdims=True) + EPS) * g


def _sigmoid(x):
    return 1.0 / (1.0 + jnp.exp(-x))


def _dot(a, b):
    return jnp.dot(a, b, preferred_element_type=jnp.float32)


def _dot_nt(a, b):
    return lax.dot_general(a, b, (((1,), (1,)), ((), ())), preferred_element_type=jnp.float32)


def _rope_rows(zt, n_heads, cos, sin):
    pieces = []
    for h in range(n_heads):
        o = h * HEAD_DIM
        x1 = zt[o:o + ROT_HALF]
        x2 = zt[o + ROT_HALF:o + ROT_DIM]
        pieces.append(x1 * cos - x2 * sin)
        pieces.append(x2 * cos + x1 * sin)
        pieces.append(zt[o + ROT_DIM:o + HEAD_DIM])
    return jnp.concatenate(pieces, axis=0)


def _in_proj_kernel(x_ref, g_ref, wt_ref, wc_ref, cos_ref, sin_ref,
                    qt_ref, qit_ref, wit_ref, vt_ref, kb_ref, kib_ref,
                    k_ref, v_ref, ki_ref, u_ref, cb_ref):
    h = _rms(x_ref[...], g_ref[...]).astype(jnp.bfloat16)
    zt = _dot_nt(wt_ref[...], h)
    cos = cos_ref[...]
    sin = sin_ref[...]
    qt_ref[...] = (_rope_rows(zt[ZT_Q:ZT_K], N_HEADS, cos, sin) * LOG2_E).astype(jnp.bfloat16)
    k = _rope_rows(zt[ZT_K:ZT_V], N_KV_HEADS, cos, sin).T
    k_ref[...] = k
    for g in range(N_KV_HEADS):
        kb_ref[g] = k[:, g * HEAD_DIM:(g + 1) * HEAD_DIM].astype(jnp.bfloat16)
    vt = zt[ZT_V:ZT_IQ]
    vt_ref[0] = vt.astype(jnp.bfloat16)
    v_ref[...] = vt.T
    qit_ref[...] = _rope_rows(zt[ZT_IQ:ZT_IK], IDX_HEADS, cos, sin).astype(jnp.bfloat16)
    ki = _rope_rows(zt[ZT_IK:ZT_IW], 1, cos, sin).T
    ki_ref[...] = ki
    kib_ref[...] = ki.astype(jnp.bfloat16)
    wit_ref[...] = zt[ZT_IW:ZT_ROWS]
    zc = _dot(h, wc_ref[...])
    cb_ref[...] = zc[:, :CONV_DIM]
    u_ref[...] = zc[:, CONV_DIM:2 * CONV_DIM] * zc[:, 2 * CONV_DIM:]


def _in_proj(x, g0, wt, wc, cos_t, sin_t):
    n = x.shape[0]
    tm = ROW_TILE
    nb = n // tm
    f32, bf16 = jnp.float32, jnp.bfloat16
    full = lambda shape: pl.BlockSpec(shape, lambda i: (0,) * len(shape))
    rows = lambda w: pl.BlockSpec((tm, w), lambda i: (i, 0))
    cols = lambda r: pl.BlockSpec((r, tm), lambda i: (0, i))
    out_shape = (
        jax.ShapeDtypeStruct((Q_COLS, n), bf16),
        jax.ShapeDtypeStruct((IQ_COLS, n), bf16),
        jax.ShapeDtypeStruct((IW_ROWS, n), f32),
        jax.ShapeDtypeStruct((nb, KV_COLS, tm), bf16),
        jax.ShapeDtypeStruct((N_KV_HEADS, n, HEAD_DIM), bf16),
        jax.ShapeDtypeStruct((n, IDX_DIM), bf16),
        jax.ShapeDtypeStruct((n, KV_COLS), f32),
        jax.ShapeDtypeStruct((n, KV_COLS), f32),
        jax.ShapeDtypeStruct((n, IDX_DIM), f32),
        jax.ShapeDtypeStruct((n, CONV_DIM), f32),
        jax.ShapeDtypeStruct((n, CONV_DIM), f32),
    )
    out_specs = (
        cols(Q_COLS), cols(IQ_COLS), cols(IW_ROWS),
        pl.BlockSpec((1, KV_COLS, tm), lambda i: (i, 0, 0)),
        pl.BlockSpec((N_KV_HEADS, tm, HEAD_DIM), lambda i: (0, i, 0)),
        rows(IDX_DIM), rows(KV_COLS), rows(KV_COLS), rows(IDX_DIM), rows(CONV_DIM), rows(CONV_DIM),
    )
    return pl.pallas_call(
        _in_proj_kernel,
        grid=(nb,),
        in_specs=[rows(D_MODEL), full((1, D_MODEL)), full((ZT_ROWS, D_MODEL)), full((D_MODEL, 3 * CONV_DIM)),
                  cols(ROT_HALF), cols(ROT_HALF)],
        out_specs=out_specs,
        out_shape=out_shape,
        compiler_params=pltpu.CompilerParams(dimension_semantics=("arbitrary",), vmem_limit_bytes=VMEM_LIMIT),
        name="in_proj",
    )(x, g0, wt, wc, cos_t, sin_t)


def _attn_kernel(qt_ref, qit_ref, wit_ref, kb_ref, kib_ref, vt_ref, tri_ref, o_ref,
                 skey_ref, bias_ref, ot_ref, qg_ref, acc_ref, lg_ref, *, tq, kb_rows, n_kblocks, count_blocks, causal,
                 first_key, topk):
    j = pl.program_id(1)
    nkb = (j + 1) if causal else n_kblocks
    i32 = jnp.int32

    q_row = j * tq + lax.broadcasted_iota(i32, (1, tq), 1)
    if causal:
        lim = jnp.where(q_row >= first_key, ((q_row >> CHUNK_SHIFT) << CHUNK_SHIFT) + CHUNK, 0)
    else:
        lim = jnp.full((1, tq), n_kblocks * kb_rows, i32)

    def key_rows(kb):
        return kb * kb_rows + lax.broadcasted_iota(i32, (kb_rows, 1), 0)

    def kslice(kb):
        return pl.ds(pl.multiple_of(kb * kb_rows, SUBLANES), kb_rows)

    def to_key(f):
        bits = lax.bitcast_convert_type(f, i32)
        return bits ^ ((bits >> 31) & 0x7FFFFFFF)

    def to_f32(k):
        return lax.bitcast_convert_type(k ^ ((k >> 31) & 0x7FFFFFFF), jnp.float32)

    def score_block(kb, edge):
        ki = kib_ref[kslice(kb), :]
        acc = jnp.zeros((kb_rows, tq), jnp.float32)
        for h in range(IDX_HEADS):
            s = _dot(ki, qit_ref[h * IDX_DIM:(h + 1) * IDX_DIM, :])
            acc = acc + wit_ref[h:h + 1, :] * jnp.maximum(s, 0.0)
        acc = jnp.where(acc == 0.0, 0.0, acc)
        if edge:
            rows = key_rows(kb)
            adm = jnp.logical_and(rows >= first_key, rows < lim)
            sk = to_key(jnp.where(adm, acc, -jnp.inf))
            masked = jnp.where(adm, acc, 0.0)
            scale = (jnp.sum(masked * masked, axis=0, keepdims=True),
                     jnp.sum(jnp.where(adm, 1.0, 0.0), axis=0, keepdims=True))
        else:
            sk = to_key(acc)
            scale = None
        skey_ref[kslice(kb), :] = sk
        return jnp.sum(jnp.where(sk > ZERO_KEY, 1, 0), axis=0, keepdims=True), scale

    zero = jnp.zeros((1, tq), jnp.float32)
    izero = jnp.zeros((1, tq), i32)
    c_pos, (s2, s0) = score_block(0, True)
    c_pos = lax.fori_loop(1, nkb - 1, lambda kb, c: c + score_block(kb, False)[0], c_pos)

    def last_block():
        c, (s2_l, s0_l) = score_block(nkb - 1, True)
        return c_pos + c, s2 + s2_l, s0 + s0_l

    c_pos, s2, s0 = lax.cond(nkb > 1, last_block, lambda: (c_pos, s2, s0))

    cslice_rows = count_blocks * kb_rows
    n_steps = (nkb + count_blocks - 1) // count_blocks
    if count_blocks > 1:
        skey_ref[kslice(nkb), :] = jnp.full((kb_rows, tq), INT_MIN, i32)

    def count_ge(cand):
        def body(c, acc):
            sk = skey_ref[pl.ds(pl.multiple_of(c * cslice_rows, SUBLANES), cslice_rows), :]
            return acc + jnp.sum(jnp.where(sk >= cand, 1, 0), axis=0, keepdims=True)
        return lax.fori_loop(0, n_steps, body, izero)

    n_adm = jnp.maximum(lim - first_key, 0)
    n_f = jnp.maximum(n_adm, 1).astype(jnp.float32)
    sigma = jnp.sqrt(s2 / jnp.maximum(s0, 1.0))
    tail = jnp.clip(topk / n_f, 1e-6, 1.0 - 1e-6)
    tq_ = jnp.sqrt(-2.0 * jnp.log(jnp.minimum(tail, 1.0 - tail)))
    zq = tq_ - (2.515517 + 0.802853 * tq_ + 0.010328 * tq_ * tq_) / (
        1.0 + 1.432788 * tq_ + 0.189269 * tq_ * tq_ + 0.001308 * tq_ * tq_ * tq_)
    zq = jnp.where(tail < 0.5, zq, -zq)
    step_scale = 1.5 * sigma / jnp.maximum(jnp.abs(zq), 0.5)
    log_k = math.log(topk + 0.5)

    def search_cond(st):
        return jnp.logical_and(st[0] < SEARCH_CAP, st[1] > 0)

    def search_body(st):
        it, _, lo, hi, c_lo, c_hi, lo_real, hi_real, w_lo, w_hi, last, stall, done = st
        f_lo, f_hi = to_f32(lo), to_f32(hi)
        g_lo = jnp.log(c_lo.astype(jnp.float32)) - log_k
        g_hi = log_k - jnp.log(jnp.maximum(c_hi.astype(jnp.float32), 0.5))
        t_in = f_lo + (f_hi - f_lo) * (w_lo * g_lo / (w_lo * g_lo + w_hi * g_hi))
        t_up = f_lo + step_scale * g_lo
        t_dn = f_hi - step_scale * g_hi
        t = jnp.where(lo_real > 0, jnp.where(hi_real > 0, t_in, t_up), t_dn)
        cand = jnp.minimum(jnp.maximum(to_key(t), lo + 1), hi - 1)
        mid = (lo >> 1) + (hi >> 1) + (lo & hi & 1)
        cand = jnp.where(stall >= STALL_STEPS, mid, jnp.where(it >= INTERP_STEPS, mid, cand))
        cand = jnp.where(it == 0, jnp.where(below, ZERO_KEY, cand), cand)
        cnt = count_ge(cand)
        up = jnp.where(done > 0, 0, jnp.where(cnt >= topk, 1, 0))
        dn = jnp.where(done > 0, 0, jnp.where(cnt >= topk, 0, 1))
        both = lo_real * hi_real
        stall = stall + both * jnp.where(cnt == c_lo, 1, jnp.where(cnt == c_hi, 1, 0))
        lo = jnp.where(up > 0, cand, lo)
        c_lo = jnp.where(up > 0, cnt, c_lo)
        lo_real = jnp.maximum(lo_real, up)
        hi = jnp.where(dn > 0, cand, hi)
        c_hi = jnp.where(dn > 0, cnt, c_hi)
        hi_real = jnp.maximum(hi_real, dn)
        w_hi = jnp.where(up > 0, jnp.where(last > 0, 0.5 * w_hi, 1.0), jnp.where(dn > 0, 1.0, w_hi))
        w_lo = jnp.where(dn > 0, jnp.where(last < 0, 0.5 * w_lo, 1.0), jnp.where(up > 0, 1.0, w_lo))
        last = up - dn + (1 - up - dn) * last
        done = jnp.where(c_lo == topk, 1, jnp.where(hi == lo + 1, 1, done))
        return (it + 1, jnp.sum(1 - done), lo, hi, c_lo, c_hi, lo_real, hi_real, w_lo, w_hi, last, stall, done)

    small = n_adm <= topk
    above = c_pos >= topk
    below = jnp.where(small, 0, jnp.where(above, 0, 1)) > 0
    lo0 = jnp.where(small, NEG_INF_KEY, jnp.where(above, ZERO_KEY + 1, MIN_FINITE_KEY))
    hi0 = jnp.where(small, NEG_INF_KEY + 1, jnp.where(above, POS_INF_KEY, ZERO_KEY + 1))
    c_lo0 = jnp.where(small, topk, jnp.where(above, c_pos, n_adm))
    c_hi0 = jnp.where(small, 0, jnp.where(above, 0, c_pos))
    lo_real0 = jnp.where(small, 0, jnp.where(above, 1, 0))
    hi_real0 = jnp.where(small, 0, jnp.where(above, 0, 1))
    done0 = jnp.where(c_lo0 == topk, 1, jnp.where(hi0 == lo0 + 1, 1, 0))
    one = jnp.ones((1, tq), jnp.float32)
    st = lax.while_loop(search_cond, search_body,
                        (i32(0), jnp.sum(1 - done0), lo0, hi0, c_lo0, c_hi0, lo_real0, hi_real0, one, one,
                         izero, izero, done0))
    thr, hi, c_hi = st[2], st[3], st[5]
    need = jnp.where(small, 0, jnp.where(hi == thr + 1, topk - c_hi, TAKE_ALL)).astype(jnp.float32)

    def bias_block(kb, run):
        sk = skey_ref[kslice(kb), :]
        tie = jnp.where(sk == thr, 1.0, 0.0)
        rank = _dot(tri_ref[...], tie.astype(jnp.bfloat16)) + run
        tie_bias = jnp.where(rank <= need, 0.0, NEG_BIAS)
        bias_ref[kslice(kb), :] = jnp.where(sk > thr, 0.0, jnp.where(sk == thr, tie_bias, NEG_BIAS))
        return rank[kb_rows - 1:kb_rows, :]

    def bias_body(c, run):
        for i in range(count_blocks):
            run = bias_block(c * count_blocks + i, run)
        return run

    lax.fori_loop(0, n_steps, bias_body, zero)

    gw = N_GROUPS * tq
    for g in range(N_KV_HEADS):
        for hh in range(N_GROUPS):
            h = g * N_GROUPS + hh
            qg_ref[g, :, hh * tq:(hh + 1) * tq] = qt_ref[h * HEAD_DIM:(h + 1) * HEAD_DIM, :]
    acc_ref[...] = jnp.zeros_like(acc_ref)

    def logits_stage(kb, g, slot):
        bias = bias_ref[kslice(kb), :]
        bias = jnp.concatenate([bias] * N_GROUPS, axis=1)
        logits = _dot(kb_ref[g, kslice(kb), :], qg_ref[g]) + bias
        lg_ref[slot, g] = logits
        return jnp.max(logits, axis=0, keepdims=True)

    def value_stage(kb, g, slot, m, m_blk):
        m_new = jnp.maximum(m, m_blk)
        p = jnp.exp2(lg_ref[slot, g] - m_new).astype(jnp.bfloat16)
        alpha = jnp.exp2(m - m_new)
        ones = jnp.ones((SUBLANES, kb_rows), jnp.bfloat16)
        vt = jnp.concatenate([vt_ref[kb, g * HEAD_DIM:(g + 1) * HEAD_DIM, :], ones], axis=0)
        acc_ref[g] = alpha * acc_ref[g] + _dot(vt, p)
        return m_new

    groups = range(N_KV_HEADS)

    def step(kb, slot, ms, m_blks, prefetch):
        nxt = tuple(logits_stage(kb + 1, g, 1 - slot) for g in groups) if prefetch else None
        return tuple(value_stage(kb, g, slot, ms[g], m_blks[g]) for g in groups), nxt

    def pair_body(i, carry):
        ms, m_blks = carry
        ms, m_blks = step(2 * i, 0, ms, m_blks, True)
        return step(2 * i + 1, 1, ms, m_blks, True)

    m0 = jnp.full((1, gw), NEG_BIAS, jnp.float32)
    carry = ((m0,) * N_KV_HEADS, tuple(logits_stage(0, g, 0) for g in groups))
    n_pairs = (nkb - 1) // 2
    ms, m_blks = lax.fori_loop(0, n_pairs, pair_body, carry)
    last = 2 * n_pairs

    def two_left():
        ms1, m_blks1 = step(last, 0, ms, m_blks, True)
        return step(last + 1, 1, ms1, m_blks1, False)[0]

    ms = lax.cond(nkb - last == 2, two_left, lambda: step(last, 0, ms, m_blks, False)[0])
    for g in groups:
        m = ms[g]
        acc = acc_ref[g]
        og = jnp.where(m > 0.5 * NEG_BIAS, acc[:HEAD_DIM] / acc[HEAD_DIM:HEAD_DIM + 1], 0.0)
        for hh in range(N_GROUPS):
            h = g * N_GROUPS + hh
            ot_ref[h * HEAD_DIM:(h + 1) * HEAD_DIM, :] = og[:, hh * tq:(hh + 1) * tq]

    o_ref[...] = ot_ref[...].T.astype(o_ref.dtype)


def _attention(qt, qit, wit, kb, kib, vt, *, n_seq, lq, lk, tq, kb_rows, causal, first_key, topk):
    nq = lq // tq
    n_kblocks = lk // kb_rows
    tri = jnp.tril(jnp.ones((kb_rows, kb_rows), jnp.bfloat16))
    count_blocks = 2 if causal else 1
    kernel = functools.partial(_attn_kernel, tq=tq, kb_rows=kb_rows, n_kblocks=n_kblocks,
                               count_blocks=count_blocks, causal=causal, first_key=first_key, topk=topk)
    return pl.pallas_call(
        kernel,
        grid=(n_seq, nq),
        in_specs=[
            pl.BlockSpec((Q_COLS, tq), lambda b, j: (0, b * nq + j)),
            pl.BlockSpec((IQ_COLS, tq), lambda b, j: (0, b * nq + j)),
            pl.BlockSpec((IW_ROWS, tq), lambda b, j: (0, b * nq + j)),
            pl.BlockSpec((N_KV_HEADS, lk, HEAD_DIM), lambda b, j: (0, b, 0)),
            pl.BlockSpec((lk, IDX_DIM), lambda b, j: (b, 0)),
            pl.BlockSpec((n_kblocks, KV_COLS, kb_rows), lambda b, j: (b, 0, 0)),
            pl.BlockSpec((kb_rows, kb_rows), lambda b, j: (0, 0)),
        ],
        out_specs=pl.BlockSpec((tq, Q_COLS), lambda b, j: (b * nq + j, 0)),
        out_shape=jax.ShapeDtypeStruct((n_seq * lq, Q_COLS), jnp.bfloat16),
        scratch_shapes=[
            pltpu.VMEM((lk + (count_blocks - 1) * kb_rows, tq), jnp.int32),
            pltpu.VMEM((lk + (count_blocks - 1) * kb_rows, tq), jnp.float32),
            pltpu.VMEM((Q_COLS, tq), jnp.float32),
            pltpu.VMEM((N_KV_HEADS, HEAD_DIM, N_GROUPS * tq), jnp.bfloat16),
            pltpu.VMEM((N_KV_HEADS, HEAD_DIM + SUBLANES, N_GROUPS * tq), jnp.float32),
            pltpu.VMEM((2, N_KV_HEADS, kb_rows, N_GROUPS * tq), jnp.float32),
        ],
        compiler_params=pltpu.CompilerParams(dimension_semantics=("arbitrary", "arbitrary"),
                                             vmem_limit_bytes=VMEM_LIMIT),
        name="dsa_attention_causal" if causal else "dsa_attention_full",
    )(qt, qit, wit, kb, kib, vt, tri)


def _mix_out_kernel(x_ref, o_ref, u_ref, halo_ref, cb_ref, g0_ref, g1_ref, cw_ref,
                    wg_ref, wao_ref, wco_ref, wo_ref, x1_ref, *, n_seg):
    x = x_ref[...]
    h = _rms(x, g0_ref[...]).astype(jnp.bfloat16)
    gates = _dot(h, wg_ref[...])
    y_a = _dot(o_ref[...], wao_ref[...])
    u = u_ref[...]
    halo = halo_ref[0]
    t = u.shape[0]
    seg = t // n_seg
    row = lax.broadcasted_iota(jnp.int32, (t, 1), 0)
    u1 = pltpu.roll(u, 1, 0)
    u2 = pltpu.roll(u, 2, 0)
    for s in range(n_seg):
        h6 = halo[s * SUBLANES + 6:s * SUBLANES + 7]
        h7 = halo[s * SUBLANES + 7:s * SUBLANES + 8]
        u1 = jnp.where(row == s * seg, h7, u1)
        u2 = jnp.where(row == s * seg, h6, jnp.where(row == s * seg + 1, h7, u2))
    conv = cw_ref[0:1] * u2 + cw_ref[1:2] * u1 + cw_ref[2:3] * u
    y_b = _dot((cb_ref[...] * conv).astype(jnp.bfloat16), wco_ref[...])
    m = _sigmoid(gates[:, :D_MODEL]) * y_a + _sigmoid(gates[:, D_MODEL:]) * y_b
    a = _dot(m.astype(jnp.bfloat16), wo_ref[...])
    x1_ref[...] = x + _rms(a, g1_ref[...])


def _mix_out(x, o, u, halo, cb, g0, g1, cw, wg, wao, wco, wo):
    n = x.shape[0]
    tm = ROW_TILE
    n_seg = halo.shape[1] // SUBLANES
    full = lambda shape: pl.BlockSpec(shape, lambda i: (0,) * len(shape))
    rows = lambda w: pl.BlockSpec((tm, w), lambda i: (i, 0))
    return pl.pallas_call(
        functools.partial(_mix_out_kernel, n_seg=n_seg),
        grid=(n // tm,),
        in_specs=[rows(D_MODEL), rows(Q_COLS), rows(CONV_DIM),
                  pl.BlockSpec((1, n_seg * SUBLANES, CONV_DIM), lambda i: (i, 0, 0)), rows(CONV_DIM),
                  full((1, D_MODEL)), full((1, D_MODEL)), full((CONV_WIDTH, CONV_DIM)),
                  full((D_MODEL, 2 * D_MODEL)), full((Q_COLS, D_MODEL)), full((CONV_DIM, D_MODEL)),
                  full((D_MODEL, D_MODEL))],
        out_specs=rows(D_MODEL),
        out_shape=jax.ShapeDtypeStruct((n, D_MODEL), jnp.float32),
        compiler_params=pltpu.CompilerParams(dimension_semantics=("arbitrary",), vmem_limit_bytes=VMEM_LIMIT),
        name="mix_out",
    )(x, o, u, halo, cb, g0, g1, cw, wg, wao, wco, wo)


def _mlp_kernel(x_ref, g2_ref, g3_ref, wup_ref, wdn_ref, x2_ref):
    x = x_ref[...]
    h = _rms(x, g2_ref[...]).astype(jnp.bfloat16)
    up = jnp.maximum(_dot(h, wup_ref[...]), 0.0)
    f = _dot((up * up).astype(jnp.bfloat16), wdn_ref[...])
    x2_ref[...] = x + _rms(f, g3_ref[...])


def _mlp(x, g2, g3, wup, wdn):
    n = x.shape[0]
    tm = ROW_TILE
    full = lambda shape: pl.BlockSpec(shape, lambda i: (0,) * len(shape))
    rows = lambda w: pl.BlockSpec((tm, w), lambda i: (i, 0))
    return pl.pallas_call(
        _mlp_kernel,
        grid=(n // tm,),
        in_specs=[rows(D_MODEL), full((1, D_MODEL)), full((1, D_MODEL)), full((D_MODEL, D_FF)),
                  full((D_FF, D_MODEL))],
        out_specs=rows(D_MODEL),
        out_shape=jax.ShapeDtypeStruct((n, D_MODEL), jnp.float32),
        compiler_params=pltpu.CompilerParams(dimension_semantics=("arbitrary",), vmem_limit_bytes=VMEM_LIMIT),
        name="mlp",
    )(x, g2, g3, wup, wdn)


def _layer_weights(w_in, conv_w, w_attn_out, w_conv_out, w_o, w_up, w_down):
    bf16 = jnp.bfloat16
    wt = jnp.concatenate([
        w_in[:, :END_Q] * (HEAD_DIM ** -0.5),
        w_in[:, END_Q:END_V],
        w_in[:, END_V:END_IQ] * (IDX_DIM ** -0.5),
        w_in[:, END_IQ:END_IK],
        w_in[:, END_IK:END_IW] * (IDX_HEADS ** -0.5),
        jnp.zeros((D_MODEL, IW_ROWS - IDX_HEADS), w_in.dtype),
    ], axis=1).T.astype(bf16)
    return dict(
        wt=wt,
        wc=w_in[:, END_IW:END_CX].astype(bf16),
        wg=w_in[:, END_CX:].astype(bf16),
        cw=conv_w,
        wao=w_attn_out.astype(bf16), wco=w_conv_out.astype(bf16), wo=w_o.astype(bf16),
        wup=w_up.astype(bf16), wdn=w_down.astype(bf16))


def _rope_tables(pos):
    inv = jnp.exp(-math.log(ROPE_THETA) * jnp.arange(ROT_HALF, dtype=jnp.float32) * (2.0 / ROT_DIM))
    ang = inv[:, None] * pos.astype(jnp.float32)[None, :]
    return jnp.cos(ang), jnp.sin(ang)


def _halo(u, n_seq, past):
    n = u.shape[0]
    per_seq = n // n_seq
    first = jnp.concatenate([jnp.zeros((n_seq, SUBLANES - (CONV_WIDTH - 1), CONV_DIM), u.dtype), past], axis=1)
    if per_seq >= ROW_TILE:
        tiles = per_seq // ROW_TILE
        tails = u.reshape(n_seq, tiles, ROW_TILE, CONV_DIM)[:, :tiles - 1, ROW_TILE - SUBLANES:]
        return jnp.concatenate([first[:, None], tails], axis=1).reshape(n_seq * tiles, SUBLANES, CONV_DIM)
    n_seg = ROW_TILE // per_seq
    return first.reshape(n_seq // n_seg, n_seg * SUBLANES, CONV_DIM)


def _pad_lanes(a, n_seq, per_seq, width):
    a = a.reshape(a.shape[0], n_seq, per_seq)
    a = jnp.pad(a, ((0, 0), (0, 0), (0, width - per_seq)))
    return a.reshape(a.shape[0], n_seq * width)


def kernel(x_prompt, x_sample, cache_k, cache_v, cache_kidx, state_conv, meta_tokens, norm_gains, w_in, conv_w,
           w_attn_out, w_conv_out, w_o, w_up, w_down):
    f32, bf16 = jnp.float32, jnp.bfloat16
    depth = w_in.shape[0]
    bp, seq, _ = x_prompt.shape
    bs, s_len, _ = x_sample.shape
    past_len = cache_k.shape[2]
    assert seq % Q_TILE == 0 and Q_TILE % CHUNK == 0 and ROW_TILE == Q_TILE and N_META <= Q_TILE
    assert (bs * s_len) % ROW_TILE == 0 and ROW_TILE % s_len == 0
    assert CONV_WIDTH - 1 <= s_len <= LANES and (past_len + s_len) % SUBLANES == 0

    pad = Q_TILE - N_META
    rp = Q_TILE + seq
    xp = jnp.concatenate([
        jnp.zeros((bp, pad, D_MODEL), f32),
        jnp.broadcast_to(meta_tokens[None].astype(f32), (bp, N_META, D_MODEL)),
        x_prompt.astype(f32)], axis=1).reshape(bp * rp, D_MODEL)
    cos_p, sin_p = _rope_tables(jnp.tile(jnp.arange(rp, dtype=jnp.int32) - pad, bp))
    topk_p = min(TOPK_MAX, seq // 4)

    xs = x_sample.astype(f32).reshape(bs * s_len, D_MODEL)
    ls = past_len + s_len
    cos_s, sin_s = _rope_tables(jnp.tile(past_len + jnp.arange(s_len, dtype=jnp.int32), bs))
    topk_s = min(TOPK_MAX, ls // 4)
    conv_zero = jnp.zeros((bp, CONV_WIDTH - 1, CONV_DIM), f32)

    outs = [[] for _ in range(8)]
    for l in range(depth):
        w = _layer_weights(w_in[l], conv_w[l], w_attn_out[l], w_conv_out[l], w_o[l], w_up[l], w_down[l])
        g = [norm_gains[l, i][None, :] for i in range(4)]

        qt, qit, wit, vt, kb, kib, k32, v32, ki32, u, cb = _in_proj(xp, g[0], w["wt"], w["wc"], cos_p, sin_p)
        o = _attention(qt, qit, wit, kb, kib, vt, n_seq=bp, lq=rp, lk=rp, tq=Q_TILE, kb_rows=ROW_TILE,
                       causal=True, first_key=pad, topk=topk_p)
        x1 = _mix_out(xp, o, u, _halo(u, bp, conv_zero), cb, g[0], g[1], w["cw"], w["wg"], w["wao"], w["wco"],
                      w["wo"])
        xp = _mlp(x1, g[2], g[3], w["wup"], w["wdn"])
        outs[0].append(k32.reshape(bp, rp, N_KV_HEADS, HEAD_DIM)[:, pad:])
        outs[1].append(v32.reshape(bp, rp, N_KV_HEADS, HEAD_DIM)[:, pad:])
        outs[2].append(ki32.reshape(bp, rp, IDX_DIM)[:, pad:])
        outs[3].append(u.reshape(bp, rp, CONV_DIM)[:, rp - (CONV_WIDTH - 1):])

        qt, qit, wit, vt, kb, kib, k32, v32, ki32, u, cb = _in_proj(xs, g[0], w["wt"], w["wc"], cos_s, sin_s)
        k_new = k32.reshape(bs, s_len, N_KV_HEADS, HEAD_DIM)
        v_new = v32.reshape(bs, s_len, N_KV_HEADS, HEAD_DIM)
        ki_new = ki32.reshape(bs, s_len, IDX_DIM)
        k_all = jnp.concatenate([cache_k[l], k_new], axis=1).astype(bf16)
        v_all = jnp.concatenate([cache_v[l], v_new], axis=1).astype(bf16)
        ki_all = jnp.concatenate([cache_kidx[l], ki_new], axis=1).astype(bf16)
        o = _attention(
            _pad_lanes(qt, bs, s_len, LANES), _pad_lanes(qit, bs, s_len, LANES), _pad_lanes(wit, bs, s_len, LANES),
            k_all.transpose(2, 0, 1, 3).reshape(N_KV_HEADS, bs * ls, HEAD_DIM),
            ki_all.reshape(bs * ls, IDX_DIM),
            v_all.reshape(bs, ls, KV_COLS).transpose(0, 2, 1),
            n_seq=bs, lq=LANES, lk=ls, tq=LANES, kb_rows=ls, causal=False, first_key=0, topk=topk_s)
        o = o.reshape(bs, LANES, Q_COLS)[:, :s_len].reshape(bs * s_len, Q_COLS)
        x1 = _mix_out(xs, o, u, _halo(u, bs, state_conv[l].astype(f32)), cb, g[0], g[1], w["cw"], w["wg"],
                      w["wao"], w["wco"], w["wo"])
        xs = _mlp(x1, g[2], g[3], w["wup"], w["wdn"])
        outs[4].append(k_new)
        outs[5].append(v_new)
        outs[6].append(ki_new)
        outs[7].append(u.reshape(bs, s_len, CONV_DIM)[:, s_len - (CONV_WIDTH - 1):])

    y_prompt = xp.reshape(bp, rp, D_MODEL)[:, Q_TILE:]
    y_sample = xs.reshape(bs, s_len, D_MODEL)
    return (y_prompt, y_sample) + tuple(jnp.stack(o) for o in outs)
```

```python
import functools
import math

import jax
import jax.numpy as jnp
from jax import lax
from jax.experimental import pallas as pl
from jax.experimental.pallas import tpu as pltpu

D_MODEL = 1024
CHUNK = 64
CHUNK_SHIFT = CHUNK.bit_length() - 1
N_META = 16
HEAD_DIM = 64
N_HEADS = 8
N_KV_HEADS = 2
N_GROUPS = N_HEADS // N_KV_HEADS
IDX_HEADS = 4
IDX_DIM = 64
ROT_DIM = HEAD_DIM // 4
ROT_HALF = ROT_DIM // 2
ROPE_THETA = 500000.0
CONV_DIM = D_MODEL // 2
CONV_WIDTH = 3
D_FF = 4 * D_MODEL
TOPK_MAX = 256
EPS = 1e-6

Q_COLS = N_HEADS * HEAD_DIM
KV_COLS = N_KV_HEADS * HEAD_DIM
IQ_COLS = IDX_HEADS * IDX_DIM
END_Q = Q_COLS
END_K = END_Q + KV_COLS
END_V = END_K + KV_COLS
END_IQ = END_V + IQ_COLS
END_IK = END_IQ + IDX_DIM
END_IW = END_IK + IDX_HEADS
END_CB = END_IW + CONV_DIM
END_CC = END_CB + CONV_DIM
END_CX = END_CC + CONV_DIM
END_GA = END_CX + D_MODEL
D_IN = END_GA + D_MODEL

SUBLANES = 8
LANES = 128
IW_ROWS = SUBLANES
ZT_Q = 0
ZT_K = ZT_Q + Q_COLS
ZT_V = ZT_K + KV_COLS
ZT_IQ = ZT_V + KV_COLS
ZT_IK = ZT_IQ + IQ_COLS
ZT_IW = ZT_IK + IDX_DIM
ZT_ROWS = ZT_IW + IW_ROWS

ROW_TILE = 256
Q_TILE = 256
NEG_BIAS = -1e30
LOG2_E = math.log2(math.e)
INT_MIN = -(2 ** 31)
NEG_INF_KEY = INT_MIN + 0x7FFFFF
MIN_FINITE_KEY = NEG_INF_KEY + 1
POS_INF_KEY = 0x7F800000
INT_MAX = 2 ** 31 - 1
ZERO_KEY = 0
TAKE_ALL = 2 ** 30
INTERP_STEPS = 20
STALL_STEPS = 2
SEARCH_CAP = INTERP_STEPS + 34
VMEM_LIMIT = 56 * 1024 * 1024


def _rms(x, g):
    return x * lax.rsqrt(jnp.mean(x * x, axis=-1, keepdims=True) + EPS) * g


def _sigmoid(x):
    return 1.0 / (1.0 + jnp.exp(-x))


def _dot(a, b):
    return jnp.dot(a, b, preferred_element_type=jnp.float32)


def _dot_nt(a, b):
    return lax.dot_general(a, b, (((1,), (1,)), ((), ())), preferred_element_type=jnp.float32)


def _rope_rows(zt, n_heads, cos, sin):
    pieces = []
    for h in range(n_heads):
        o = h * HEAD_DIM
        x1 = zt[o:o + ROT_HALF]
        x2 = zt[o + ROT_HALF:o + ROT_DIM]
        pieces.append(x1 * cos - x2 * sin)
        pieces.append(x2 * cos + x1 * sin)
        pieces.append(zt[o + ROT_DIM:o + HEAD_DIM])
    return jnp.concatenate(pieces, axis=0)


def _in_proj_kernel(x_ref, g_ref, wt_ref, wc_ref, cos_ref, sin_ref,
                    qt_ref, qit_ref, wit_ref, vt_ref, kb_ref, kib_ref,
                    k_ref, v_ref, ki_ref, u_ref, cb_ref):
    h = _rms(x_ref[...], g_ref[...]).astype(jnp.bfloat16)
    zt = _dot_nt(wt_ref[...], h)
    cos = cos_ref[...]
    sin = sin_ref[...]
    qt_ref[...] = (_rope_rows(zt[ZT_Q:ZT_K], N_HEADS, cos, sin) * LOG2_E).astype(jnp.bfloat16)
    k = _rope_rows(zt[ZT_K:ZT_V], N_KV_HEADS, cos, sin).T
    k_ref[...] = k
    for g in range(N_KV_HEADS):
        kb_ref[g] = k[:, g * HEAD_DIM:(g + 1) * HEAD_DIM].astype(jnp.bfloat16)
    vt = zt[ZT_V:ZT_IQ]
    vt_ref[0] = vt.astype(jnp.bfloat16)
    v_ref[...] = vt.T
    qit_ref[...] = _rope_rows(zt[ZT_IQ:ZT_IK], IDX_HEADS, cos, sin).astype(jnp.bfloat16)
    ki = _rope_rows(zt[ZT_IK:ZT_IW], 1, cos, sin).T
    ki_ref[...] = ki
    kib_ref[...] = ki.astype(jnp.bfloat16)
    wit_ref[...] = zt[ZT_IW:ZT_ROWS]
    zc = _dot(h, wc_ref[...])
    cb_ref[...] = zc[:, :CONV_DIM]
    u_ref[...] = zc[:, CONV_DIM:2 * CONV_DIM] * zc[:, 2 * CONV_DIM:]


def _in_proj(x, g0, wt, wc, cos_t, sin_t):
    n = x.shape[0]
    tm = ROW_TILE
    nb = n // tm
    f32, bf16 = jnp.float32, jnp.bfloat16
    full = lambda shape: pl.BlockSpec(shape, lambda i: (0,) * len(shape))
    rows = lambda w: pl.BlockSpec((tm, w), lambda i: (i, 0))
    cols = lambda r: pl.BlockSpec((r, tm), lambda i: (0, i))
    out_shape = (
        jax.ShapeDtypeStruct((Q_COLS, n), bf16),
        jax.ShapeDtypeStruct((IQ_COLS, n), bf16),
        jax.ShapeDtypeStruct((IW_ROWS, n), f32),
        jax.ShapeDtypeStruct((nb, KV_COLS, tm), bf16),
        jax.ShapeDtypeStruct((N_KV_HEADS, n, HEAD_DIM), bf16),
        jax.ShapeDtypeStruct((n, IDX_DIM), bf16),
        jax.ShapeDtypeStruct((n, KV_COLS), f32),
        jax.ShapeDtypeStruct((n, KV_COLS), f32),
        jax.ShapeDtypeStruct((n, IDX_DIM), f32),
        jax.ShapeDtypeStruct((n, CONV_DIM), f32),
        jax.ShapeDtypeStruct((n, CONV_DIM), f32),
    )
    out_specs = (
        cols(Q_COLS), cols(IQ_COLS), cols(IW_ROWS),
        pl.BlockSpec((1, KV_COLS, tm), lambda i: (i, 0, 0)),
        pl.BlockSpec((N_KV_HEADS, tm, HEAD_DIM), lambda i: (0, i, 0)),
        rows(IDX_DIM), rows(KV_COLS), rows(KV_COLS), rows(IDX_DIM), rows(CONV_DIM), rows(CONV_DIM),
    )
    return pl.pallas_call(
        _in_proj_kernel,
        grid=(nb,),
        in_specs=[rows(D_MODEL), full((1, D_MODEL)), full((ZT_ROWS, D_MODEL)), full((D_MODEL, 3 * CONV_DIM)),
                  cols(ROT_HALF), cols(ROT_HALF)],
        out_specs=out_specs,
        out_shape=out_shape,
        compiler_params=pltpu.CompilerParams(dimension_semantics=("arbitrary",), vmem_limit_bytes=VMEM_LIMIT),
        name="in_proj",
    )(x, g0, wt, wc, cos_t, sin_t)


def _attn_kernel(qt_ref, qit_ref, wit_ref, kb_ref, kib_ref, vt_ref, tri_ref, o_ref,
                 skey_ref, ot_ref, qg_ref, acc_ref, lg_ref, *, tq, kb_rows, n_kblocks, count_blocks, causal,
                 first_key, topk):
    j = pl.program_id(1)
    nkb = (j + 1) if causal else n_kblocks
    i32 = jnp.int32

    q_row = j * tq + lax.broadcasted_iota(i32, (1, tq), 1)
    if causal:
        lim = jnp.where(q_row >= first_key, ((q_row >> CHUNK_SHIFT) << CHUNK_SHIFT) + CHUNK, 0)
    else:
        lim = jnp.full((1, tq), n_kblocks * kb_rows, i32)

    def key_rows(kb):
        return kb * kb_rows + lax.broadcasted_iota(i32, (kb_rows, 1), 0)

    def kslice(kb):
        return pl.ds(pl.multiple_of(kb * kb_rows, SUBLANES), kb_rows)

    def to_key(f):
        bits = lax.bitcast_convert_type(f, i32)
        return bits ^ ((bits >> 31) & 0x7FFFFFFF)

    def to_f32(k):
        return lax.bitcast_convert_type(k ^ ((k >> 31) & 0x7FFFFFFF), jnp.float32)

    def score_block(kb, edge):
        ki = kib_ref[kslice(kb), :]
        acc = jnp.zeros((kb_rows, tq), jnp.float32)
        for h in range(IDX_HEADS):
            s = _dot(ki, qit_ref[h * IDX_DIM:(h + 1) * IDX_DIM, :])
            acc = acc + wit_ref[h:h + 1, :] * jnp.maximum(s, 0.0)
        acc = jnp.where(acc == 0.0, 0.0, acc)
        if edge:
            rows = key_rows(kb)
            adm = jnp.logical_and(rows >= first_key, rows < lim)
            sk = to_key(jnp.where(adm, acc, -jnp.inf))
            masked = jnp.where(adm, acc, 0.0)
            scale = (jnp.sum(masked * masked, axis=0, keepdims=True),
                     jnp.sum(jnp.where(adm, 1.0, 0.0), axis=0, keepdims=True))
        else:
            sk = to_key(acc)
            scale = None
        skey_ref[kslice(kb), :] = sk
        return jnp.sum(jnp.where(sk > ZERO_KEY, 1, 0), axis=0, keepdims=True), scale

    zero = jnp.zeros((1, tq), jnp.float32)
    izero = jnp.zeros((1, tq), i32)
    c_pos, (s2, s0) = score_block(0, True)
    n_inner_pairs = (nkb - 2) // 2
    c_pos = lax.fori_loop(
        0, n_inner_pairs,
        lambda p, c: c + score_block(2 * p + 1, False)[0] + score_block(2 * p + 2, False)[0], c_pos)
    n_left = nkb - 1 - 2 * jnp.maximum(n_inner_pairs, 0)

    def last_block(c):
        c_l, (s2_l, s0_l) = score_block(nkb - 1, True)
        return c + c_l, s2 + s2_l, s0 + s0_l

    c_pos, s2, s0 = lax.cond(
        n_left == 2, lambda: last_block(c_pos + score_block(nkb - 2, False)[0]),
        lambda: lax.cond(n_left == 1, lambda: last_block(c_pos), lambda: (c_pos, s2, s0)))

    cslice_rows = count_blocks * kb_rows
    n_steps = (nkb + count_blocks - 1) // count_blocks
    if count_blocks > 1:
        skey_ref[kslice(nkb), :] = jnp.full((kb_rows, tq), INT_MIN, i32)

    def count_ge(cand):
        def body(c, acc):
            sk = skey_ref[pl.ds(pl.multiple_of(c * cslice_rows, SUBLANES), cslice_rows), :]
            return acc + jnp.sum(jnp.where(sk >= cand, 1, 0), axis=0, keepdims=True)
        return lax.fori_loop(0, n_steps, body, izero)

    n_adm = jnp.maximum(lim - first_key, 0)
    n_f = jnp.maximum(n_adm, 1).astype(jnp.float32)
    sigma = jnp.sqrt(s2 / jnp.maximum(s0, 1.0))
    tail = jnp.clip(topk / n_f, 1e-6, 1.0 - 1e-6)
    tq_ = jnp.sqrt(-2.0 * jnp.log(jnp.minimum(tail, 1.0 - tail)))
    zq = tq_ - (2.515517 + 0.802853 * tq_ + 0.010328 * tq_ * tq_) / (
        1.0 + 1.432788 * tq_ + 0.189269 * tq_ * tq_ + 0.001308 * tq_ * tq_ * tq_)
    zq = jnp.where(tail < 0.5, zq, -zq)
    step_scale = 1.5 * sigma / jnp.maximum(jnp.abs(zq), 0.5)
    log_k = math.log(topk + 0.5)

    def search_cond(st):
        return jnp.logical_and(st[0] < SEARCH_CAP, st[1] > 0)

    def search_body(st):
        it, _, lo, hi, c_lo, c_hi, lo_real, hi_real, w_lo, w_hi, last, stall, done = st
        f_lo, f_hi = to_f32(lo), to_f32(hi)
        g_lo = jnp.log(c_lo.astype(jnp.float32)) - log_k
        g_hi = log_k - jnp.log(jnp.maximum(c_hi.astype(jnp.float32), 0.5))
        t_in = f_lo + (f_hi - f_lo) * (w_lo * g_lo / (w_lo * g_lo + w_hi * g_hi))
        t_up = f_lo + step_scale * g_lo
        t_dn = f_hi - step_scale * g_hi
        t = jnp.where(lo_real > 0, jnp.where(hi_real > 0, t_in, t_up), t_dn)
        cand = jnp.minimum(jnp.maximum(to_key(t), lo + 1), hi - 1)
        mid = (lo >> 1) + (hi >> 1) + (lo & hi & 1)
        cand = jnp.where(stall >= STALL_STEPS, mid, jnp.where(it >= INTERP_STEPS, mid, cand))
        cand = jnp.where(it == 0, jnp.where(below, ZERO_KEY, cand), cand)
        cnt = count_ge(cand)
        up = jnp.where(done > 0, 0, jnp.where(cnt >= topk, 1, 0))
        dn = jnp.where(done > 0, 0, jnp.where(cnt >= topk, 0, 1))
        both = lo_real * hi_real
        stall = stall + both * jnp.where(cnt == c_lo, 1, jnp.where(cnt == c_hi, 1, 0))
        lo = jnp.where(up > 0, cand, lo)
        c_lo = jnp.where(up > 0, cnt, c_lo)
        lo_real = jnp.maximum(lo_real, up)
        hi = jnp.where(dn > 0, cand, hi)
        c_hi = jnp.where(dn > 0, cnt, c_hi)
        hi_real = jnp.maximum(hi_real, dn)
        w_hi = jnp.where(up > 0, jnp.where(last > 0, 0.5 * w_hi, 1.0), jnp.where(dn > 0, 1.0, w_hi))
        w_lo = jnp.where(dn > 0, jnp.where(last < 0, 0.5 * w_lo, 1.0), jnp.where(up > 0, 1.0, w_lo))
        last = up - dn + (1 - up - dn) * last
        done = jnp.where(c_lo == topk, 1, jnp.where(hi == lo + 1, 1, done))
        return (it + 1, jnp.sum(1 - done), lo, hi, c_lo, c_hi, lo_real, hi_real, w_lo, w_hi, last, stall, done)

    small = n_adm <= topk
    above = c_pos >= topk
    below = jnp.where(small, 0, jnp.where(above, 0, 1)) > 0
    lo0 = jnp.where(small, NEG_INF_KEY, jnp.where(above, ZERO_KEY + 1, MIN_FINITE_KEY))
    hi0 = jnp.where(small, NEG_INF_KEY + 1, jnp.where(above, POS_INF_KEY, ZERO_KEY + 1))
    c_lo0 = jnp.where(small, topk, jnp.where(above, c_pos, n_adm))
    c_hi0 = jnp.where(small, 0, jnp.where(above, 0, c_pos))
    lo_real0 = jnp.where(small, 0, jnp.where(above, 1, 0))
    hi_real0 = jnp.where(small, 0, jnp.where(above, 0, 1))
    done0 = jnp.where(c_lo0 == topk, 1, jnp.where(hi0 == lo0 + 1, 1, 0))
    one = jnp.ones((1, tq), jnp.float32)
    st = lax.while_loop(search_cond, search_body,
                        (i32(0), jnp.sum(1 - done0), lo0, hi0, c_lo0, c_hi0, lo_real0, hi_real0, one, one,
                         izero, izero, done0))
    thr, hi, c_hi = st[2], st[3], st[5]
    need = jnp.where(small, 0, jnp.where(hi == thr + 1, topk - c_hi, TAKE_ALL)).astype(jnp.float32)

    def block_bias(kb, run):
        sk = skey_ref[kslice(kb), :]
        tie = jnp.where(sk == thr, 1.0, 0.0)
        rank = _dot(tri_ref[...], tie.astype(jnp.bfloat16)) + run
        tie_bias = jnp.where(rank <= need, 0.0, NEG_BIAS)
        bias = jnp.where(sk > thr, 0.0, jnp.where(sk == thr, tie_bias, NEG_BIAS))
        return bias, rank[kb_rows - 1:kb_rows, :]

    gw = N_GROUPS * tq
    for g in range(N_KV_HEADS):
        for hh in range(N_GROUPS):
            h = g * N_GROUPS + hh
            qg_ref[g, :, hh * tq:(hh + 1) * tq] = qt_ref[h * HEAD_DIM:(h + 1) * HEAD_DIM, :]
    acc_ref[...] = jnp.zeros_like(acc_ref)

    groups = range(N_KV_HEADS)

    def logits_stage(kb, slot, run):
        bias, run = block_bias(kb, run)
        bias = jnp.concatenate([bias] * N_GROUPS, axis=1)
        m_blks = []
        for g in groups:
            logits = _dot(kb_ref[g, kslice(kb), :], qg_ref[g]) + bias
            lg_ref[slot, g] = logits
            m_blks.append(jnp.max(logits, axis=0, keepdims=True))
        return tuple(m_blks), run

    def value_stage(kb, g, slot, m, m_blk):
        m_new = jnp.maximum(m, m_blk)
        p = jnp.exp2(lg_ref[slot, g] - m_new).astype(jnp.bfloat16)
        alpha = jnp.exp2(m - m_new)
        ones = jnp.ones((SUBLANES, kb_rows), jnp.bfloat16)
        vt = jnp.concatenate([vt_ref[kb, g * HEAD_DIM:(g + 1) * HEAD_DIM, :], ones], axis=0)
        acc_ref[g] = alpha * acc_ref[g] + _dot(vt, p)
        return m_new

    def step(kb, slot, ms, staged, prefetch):
        m_blks, run = staged
        nxt = logits_stage(kb + 1, 1 - slot, run) if prefetch else None
        return tuple(value_stage(kb, g, slot, ms[g], m_blks[g]) for g in groups), nxt

    def pair_body(i, carry):
        ms, staged = carry
        ms, staged = step(2 * i, 0, ms, staged, True)
        return step(2 * i + 1, 1, ms, staged, True)

    m0 = jnp.full((1, gw), NEG_BIAS, jnp.float32)
    carry = ((m0,) * N_KV_HEADS, logits_stage(0, 0, zero))
    n_pairs = (nkb - 1) // 2
    ms, staged = lax.fori_loop(0, n_pairs, pair_body, carry)
    last = 2 * n_pairs

    def two_left():
        ms1, staged1 = step(last, 0, ms, staged, True)
        return step(last + 1, 1, ms1, staged1, False)[0]

    ms = lax.cond(nkb - last == 2, two_left, lambda: step(last, 0, ms, staged, False)[0])
    for g in groups:
        m = ms[g]
        acc = acc_ref[g]
        og = jnp.where(m > 0.5 * NEG_BIAS, acc[:HEAD_DIM] / acc[HEAD_DIM:HEAD_DIM + 1], 0.0)
        for hh in range(N_GROUPS):
            h = g * N_GROUPS + hh
            ot_ref[h * HEAD_DIM:(h + 1) * HEAD_DIM, :] = og[:, hh * tq:(hh + 1) * tq]

    o_ref[...] = ot_ref[...].T.astype(o_ref.dtype)


def _attention(qt, qit, wit, kb, kib, vt, *, n_seq, lq, lk, tq, kb_rows, causal, first_key, topk):
    nq = lq // tq
    n_kblocks = lk // kb_rows
    tri = jnp.tril(jnp.ones((kb_rows, kb_rows), jnp.bfloat16))
    count_blocks = 2 if causal else 1
    kernel = functools.partial(_attn_kernel, tq=tq, kb_rows=kb_rows, n_kblocks=n_kblocks,
                               count_blocks=count_blocks, causal=causal, first_key=first_key, topk=topk)
    return pl.pallas_call(
        kernel,
        grid=(n_seq, nq),
        in_specs=[
            pl.BlockSpec((Q_COLS, tq), lambda b, j: (0, b * nq + j)),
            pl.BlockSpec((IQ_COLS, tq), lambda b, j: (0, b * nq + j)),
            pl.BlockSpec((IW_ROWS, tq), lambda b, j: (0, b * nq + j)),
            pl.BlockSpec((N_KV_HEADS, lk, HEAD_DIM), lambda b, j: (0, b, 0)),
            pl.BlockSpec((lk, IDX_DIM), lambda b, j: (b, 0)),
            pl.BlockSpec((n_kblocks, KV_COLS, kb_rows), lambda b, j: (b, 0, 0)),
            pl.BlockSpec((kb_rows, kb_rows), lambda b, j: (0, 0)),
        ],
        out_specs=pl.BlockSpec((tq, Q_COLS), lambda b, j: (b * nq + j, 0)),
        out_shape=jax.ShapeDtypeStruct((n_seq * lq, Q_COLS), jnp.bfloat16),
        scratch_shapes=[
            pltpu.VMEM((lk + (count_blocks - 1) * kb_rows, tq), jnp.int32),
            pltpu.VMEM((Q_COLS, tq), jnp.float32),
            pltpu.VMEM((N_KV_HEADS, HEAD_DIM, N_GROUPS * tq), jnp.bfloat16),
            pltpu.VMEM((N_KV_HEADS, HEAD_DIM + SUBLANES, N_GROUPS * tq), jnp.float32),
            pltpu.VMEM((2, N_KV_HEADS, kb_rows, N_GROUPS * tq), jnp.float32),
        ],
        compiler_params=pltpu.CompilerParams(dimension_semantics=("arbitrary", "arbitrary"),
                                             vmem_limit_bytes=VMEM_LIMIT),
        name="dsa_attention_causal" if causal else "dsa_attention_full",
    )(qt, qit, wit, kb, kib, vt, tri)


def _mix_out_kernel(x_ref, o_ref, u_ref, halo_ref, cb_ref, g0_ref, g1_ref, cw_ref,
                    wg_ref, wao_ref, wco_ref, wo_ref, x1_ref, *, n_seg):
    x = x_ref[...]
    h = _rms(x, g0_ref[...]).astype(jnp.bfloat16)
    gates = _dot(h, wg_ref[...])
    y_a = _dot(o_ref[...], wao_ref[...])
    u = u_ref[...]
    halo = halo_ref[0]
    t = u.shape[0]
    seg = t // n_seg
    row = lax.broadcasted_iota(jnp.int32, (t, 1), 0)
    u1 = pltpu.roll(u, 1, 0)
    u2 = pltpu.roll(u, 2, 0)
    for s in range(n_seg):
        h6 = halo[s * SUBLANES + 6:s * SUBLANES + 7]
        h7 = halo[s * SUBLANES + 7:s * SUBLANES + 8]
        u1 = jnp.where(row == s * seg, h7, u1)
        u2 = jnp.where(row == s * seg, h6, jnp.where(row == s * seg + 1, h7, u2))
    conv = cw_ref[0:1] * u2 + cw_ref[1:2] * u1 + cw_ref[2:3] * u
    y_b = _dot((cb_ref[...] * conv).astype(jnp.bfloat16), wco_ref[...])
    m = _sigmoid(gates[:, :D_MODEL]) * y_a + _sigmoid(gates[:, D_MODEL:]) * y_b
    a = _dot(m.astype(jnp.bfloat16), wo_ref[...])
    x1_ref[...] = x + _rms(a, g1_ref[...])


def _mix_out(x, o, u, halo, cb, g0, g1, cw, wg, wao, wco, wo):
    n = x.shape[0]
    tm = ROW_TILE
    n_seg = halo.shape[1] // SUBLANES
    full = lambda shape: pl.BlockSpec(shape, lambda i: (0,) * len(shape))
    rows = lambda w: pl.BlockSpec((tm, w), lambda i: (i, 0))
    return pl.pallas_call(
        functools.partial(_mix_out_kernel, n_seg=n_seg),
        grid=(n // tm,),
        in_specs=[rows(D_MODEL), rows(Q_COLS), rows(CONV_DIM),
                  pl.BlockSpec((1, n_seg * SUBLANES, CONV_DIM), lambda i: (i, 0, 0)), rows(CONV_DIM),
                  full((1, D_MODEL)), full((1, D_MODEL)), full((CONV_WIDTH, CONV_DIM)),
                  full((D_MODEL, 2 * D_MODEL)), full((Q_COLS, D_MODEL)), full((CONV_DIM, D_MODEL)),
                  full((D_MODEL, D_MODEL))],
        out_specs=rows(D_MODEL),
        out_shape=jax.ShapeDtypeStruct((n, D_MODEL), jnp.float32),
        compiler_params=pltpu.CompilerParams(dimension_semantics=("arbitrary",), vmem_limit_bytes=VMEM_LIMIT),
        name="mix_out",
    )(x, o, u, halo, cb, g0, g1, cw, wg, wao, wco, wo)


def _mlp_kernel(x_ref, g2_ref, g3_ref, wup_ref, wdn_ref, x2_ref):
    x = x_ref[...]
    h = _rms(x, g2_ref[...]).astype(jnp.bfloat16)
    up = jnp.maximum(_dot(h, wup_ref[...]), 0.0)
    f = _dot((up * up).astype(jnp.bfloat16), wdn_ref[...])
    x2_ref[...] = x + _rms(f, g3_ref[...])


def _mlp(x, g2, g3, wup, wdn):
    n = x.shape[0]
    tm = ROW_TILE
    full = lambda shape: pl.BlockSpec(shape, lambda i: (0,) * len(shape))
    rows = lambda w: pl.BlockSpec((tm, w), lambda i: (i, 0))
    return pl.pallas_call(
        _mlp_kernel,
        grid=(n // tm,),
        in_specs=[rows(D_MODEL), full((1, D_MODEL)), full((1, D_MODEL)), full((D_MODEL, D_FF)),
                  full((D_FF, D_MODEL))],
        out_specs=rows(D_MODEL),
        out_shape=jax.ShapeDtypeStruct((n, D_MODEL), jnp.float32),
        compiler_params=pltpu.CompilerParams(dimension_semantics=("arbitrary",), vmem_limit_bytes=VMEM_LIMIT),
        name="mlp",
    )(x, g2, g3, wup, wdn)


def _layer_weights(w_in, conv_w, w_attn_out, w_conv_out, w_o, w_up, w_down):
    bf16 = jnp.bfloat16
    wt = jnp.concatenate([
        w_in[:, :END_Q] * (HEAD_DIM ** -0.5),
        w_in[:, END_Q:END_V],
        w_in[:, END_V:END_IQ] * (IDX_DIM ** -0.5),
        w_in[:, END_IQ:END_IK],
        w_in[:, END_IK:END_IW] * (IDX_HEADS ** -0.5),
        jnp.zeros((D_MODEL, IW_ROWS - IDX_HEADS), w_in.dtype),
    ], axis=1).T.astype(bf16)
    return dict(
        wt=wt,
        wc=w_in[:, END_IW:END_CX].astype(bf16),
        wg=w_in[:, END_CX:].astype(bf16),
        cw=conv_w,
        wao=w_attn_out.astype(bf16), wco=w_conv_out.astype(bf16), wo=w_o.astype(bf16),
        wup=w_up.astype(bf16), wdn=w_down.astype(bf16))


def _rope_tables(pos):
    inv = jnp.exp(-math.log(ROPE_THETA) * jnp.arange(ROT_HALF, dtype=jnp.float32) * (2.0 / ROT_DIM))
    ang = inv[:, None] * pos.astype(jnp.float32)[None, :]
    return jnp.cos(ang), jnp.sin(ang)


def _halo(u, n_seq, past):
    n = u.shape[0]
    per_seq = n // n_seq
    first = jnp.concatenate([jnp.zeros((n_seq, SUBLANES - (CONV_WIDTH - 1), CONV_DIM), u.dtype), past], axis=1)
    if per_seq >= ROW_TILE:
        tiles = per_seq // ROW_TILE
        tails = u.reshape(n_seq, tiles, ROW_TILE, CONV_DIM)[:, :tiles - 1, ROW_TILE - SUBLANES:]
        return jnp.concatenate([first[:, None], tails], axis=1).reshape(n_seq * tiles, SUBLANES, CONV_DIM)
    n_seg = ROW_TILE // per_seq
    return first.reshape(n_seq // n_seg, n_seg * SUBLANES, CONV_DIM)


def _pad_lanes(a, n_seq, per_seq, width):
    a = a.reshape(a.shape[0], n_seq, per_seq)
    a = jnp.pad(a, ((0, 0), (0, 0), (0, width - per_seq)))
    return a.reshape(a.shape[0], n_seq * width)


def kernel(x_prompt, x_sample, cache_k, cache_v, cache_kidx, state_conv, meta_tokens, norm_gains, w_in, conv_w,
           w_attn_out, w_conv_out, w_o, w_up, w_down):
    f32, bf16 = jnp.float32, jnp.bfloat16
    depth = w_in.shape[0]
    bp, seq, _ = x_prompt.shape
    bs, s_len, _ = x_sample.shape
    past_len = cache_k.shape[2]
    assert seq % Q_TILE == 0 and Q_TILE % CHUNK == 0 and ROW_TILE == Q_TILE and N_META <= Q_TILE
    assert (bs * s_len) % ROW_TILE == 0 and ROW_TILE % s_len == 0
    assert CONV_WIDTH - 1 <= s_len <= LANES and (past_len + s_len) % SUBLANES == 0

    pad = Q_TILE - N_META
    rp = Q_TILE + seq
    xp = jnp.concatenate([
        jnp.zeros((bp, pad, D_MODEL), f32),
        jnp.broadcast_to(meta_tokens[None].astype(f32), (bp, N_META, D_MODEL)),
        x_prompt.astype(f32)], axis=1).reshape(bp * rp, D_MODEL)
    cos_p, sin_p = _rope_tables(jnp.tile(jnp.arange(rp, dtype=jnp.int32) - pad, bp))
    topk_p = min(TOPK_MAX, seq // 4)

    xs = x_sample.astype(f32).reshape(bs * s_len, D_MODEL)
    ls = past_len + s_len
    cos_s, sin_s = _rope_tables(jnp.tile(past_len + jnp.arange(s_len, dtype=jnp.int32), bs))
    topk_s = min(TOPK_MAX, ls // 4)
    conv_zero = jnp.zeros((bp, CONV_WIDTH - 1, CONV_DIM), f32)

    outs = [[] for _ in range(8)]
    for l in range(depth):
        w = _layer_weights(w_in[l], conv_w[l], w_attn_out[l], w_conv_out[l], w_o[l], w_up[l], w_down[l])
        g = [norm_gains[l, i][None, :] for i in range(4)]

        qt, qit, wit, vt, kb, kib, k32, v32, ki32, u, cb = _in_proj(xp, g[0], w["wt"], w["wc"], cos_p, sin_p)
        o = _attention(qt, qit, wit, kb, kib, vt, n_seq=bp, lq=rp, lk=rp, tq=Q_TILE, kb_rows=ROW_TILE,
                       causal=True, first_key=pad, topk=topk_p)
        x1 = _mix_out(xp, o, u, _halo(u, bp, conv_zero), cb, g[0], g[1], w["cw"], w["wg"], w["wao"], w["wco"],
                      w["wo"])
        xp = _mlp(x1, g[2], g[3], w["wup"], w["wdn"])
        outs[0].append(k32.reshape(bp, rp, N_KV_HEADS, HEAD_DIM)[:, pad:])
        outs[1].append(v32.reshape(bp, rp, N_KV_HEADS, HEAD_DIM)[:, pad:])
        outs[2].append(ki32.reshape(bp, rp, IDX_DIM)[:, pad:])
        outs[3].append(u.reshape(bp, rp, CONV_DIM)[:, rp - (CONV_WIDTH - 1):])

        qt, qit, wit, vt, kb, kib, k32, v32, ki32, u, cb = _in_proj(xs, g[0], w["wt"], w["wc"], cos_s, sin_s)
        k_new = k32.reshape(bs, s_len, N_KV_HEADS, HEAD_DIM)
        v_new = v32.reshape(bs, s_len, N_KV_HEADS, HEAD_DIM)
        ki_new = ki32.reshape(bs, s_len, IDX_DIM)
        k_all = jnp.concatenate([cache_k[l], k_new], axis=1).astype(bf16)
        v_all = jnp.concatenate([cache_v[l], v_new], axis=1).astype(bf16)
        ki_all = jnp.concatenate([cache_kidx[l], ki_new], axis=1).astype(bf16)
        o = _attention(
            _pad_lanes(qt, bs, s_len, LANES), _pad_lanes(qit, bs, s_len, LANES), _pad_lanes(wit, bs, s_len, LANES),
            k_all.transpose(2, 0, 1, 3).reshape(N_KV_HEADS, bs * ls, HEAD_DIM),
            ki_all.reshape(bs * ls, IDX_DIM),
            v_all.reshape(bs, ls, KV_COLS).transpose(0, 2, 1),
            n_seq=bs, lq=LANES, lk=ls, tq=LANES, kb_rows=ls, causal=False, first_key=0, topk=topk_s)
        o = o.reshape(bs, LANES, Q_COLS)[:, :s_len].reshape(bs * s_len, Q_COLS)
        x1 = _mix_out(xs, o, u, _halo(u, bs, state_conv[l].astype(f32)), cb, g[0], g[1], w["cw"], w["wg"],
                      w["wao"], w["wco"], w["wo"])
        xs = _mlp(x1, g[2], g[3], w["wup"], w["wdn"])
        outs[4].append(k_new)
        outs[5].append(v_new)
        outs[6].append(ki_new)
        outs[7].append(u.reshape(bs, s_len, CONV_DIM)[:, s_len - (CONV_WIDTH - 1):])

    y_prompt = xp.reshape(bp, rp, D_MODEL)[:, Q_TILE:]
    y_sample = xs.reshape(bs, s_len, D_MODEL)
    return (y_prompt, y_sample) + tuple(jnp.stack(o) for o in outs)
```

```python
import functools
import math

import jax
import jax.numpy as jnp
from jax import lax
from jax.experimental import pallas as pl
from jax.experimental.pallas import tpu as pltpu

D_MODEL = 1024
CHUNK = 64
CHUNK_SHIFT = CHUNK.bit_length() - 1
N_META = 16
HEAD_DIM = 64
N_HEADS = 8
N_KV_HEADS = 2
N_GROUPS = N_HEADS // N_KV_HEADS
IDX_HEADS = 4
IDX_DIM = 64
ROT_DIM = HEAD_DIM // 4
ROT_HALF = ROT_DIM // 2
ROPE_THETA = 500000.0
CONV_DIM = D_MODEL // 2
CONV_WIDTH = 3
D_FF = 4 * D_MODEL
TOPK_MAX = 256
EPS = 1e-6

Q_COLS = N_HEADS * HEAD_DIM
KV_COLS = N_KV_HEADS * HEAD_DIM
IQ_COLS = IDX_HEADS * IDX_DIM
END_Q = Q_COLS
END_K = END_Q + KV_COLS
END_V = END_K + KV_COLS
END_IQ = END_V + IQ_COLS
END_IK = END_IQ + IDX_DIM
END_IW = END_IK + IDX_HEADS
END_CB = END_IW + CONV_DIM
END_CC = END_CB + CONV_DIM
END_CX = END_CC + CONV_DIM
END_GA = END_CX + D_MODEL
D_IN = END_GA + D_MODEL

SUBLANES = 8
LANES = 128
IW_ROWS = SUBLANES
ZT_Q = 0
ZT_K = ZT_Q + Q_COLS
ZT_V = ZT_K + KV_COLS
ZT_IQ = ZT_V + KV_COLS
ZT_IK = ZT_IQ + IQ_COLS
ZT_IW = ZT_IK + IDX_DIM
ZT_ROWS = ZT_IW + IW_ROWS

ROW_TILE = 256
Q_TILE = 256
NEG_BIAS = -1e30
LOG2_E = math.log2(math.e)
KEY_LANES = 128
BOUND_MARGIN = 1.01
MIN_DENOMINATOR = 2.0 ** -60
INT_MIN = -(2 ** 31)
NEG_INF_KEY = INT_MIN + 0x7FFFFF
MIN_FINITE_KEY = NEG_INF_KEY + 1
POS_INF_KEY = 0x7F800000
INT_MAX = 2 ** 31 - 1
ZERO_KEY = 0
TAKE_ALL = 2 ** 30
INTERP_STEPS = 20
STALL_STEPS = 2
SUM_CHAINS = 4
SEARCH_CAP = INTERP_STEPS + 34
VMEM_LIMIT = 56 * 1024 * 1024


def _rms(x, g):
    return x * lax.rsqrt(jnp.mean(x * x, axis=-1, keepdims=True) + EPS) * g


def _sigmoid(x):
    return 1.0 / (1.0 + jnp.exp(-x))


def _dot(a, b):
    return jnp.dot(a, b, preferred_element_type=jnp.float32)


def _dot_nt(a, b):
    return lax.dot_general(a, b, (((1,), (1,)), ((), ())), preferred_element_type=jnp.float32)


def _rope_rows(zt, n_heads, cos, sin):
    pieces = []
    for h in range(n_heads):
        o = h * HEAD_DIM
        x1 = zt[o:o + ROT_HALF]
        x2 = zt[o + ROT_HALF:o + ROT_DIM]
        pieces.append(x1 * cos - x2 * sin)
        pieces.append(x2 * cos + x1 * sin)
        pieces.append(zt[o + ROT_DIM:o + HEAD_DIM])
    return jnp.concatenate(pieces, axis=0)


def _in_proj_kernel(x_ref, g_ref, wt_ref, wc_ref, cos_ref, sin_ref,
                    qt_ref, qit_ref, wit_ref, vt_ref, kb_ref, kib_ref,
                    k_ref, v_ref, ki_ref, u_ref, cb_ref):
    h = _rms(x_ref[...], g_ref[...]).astype(jnp.bfloat16)
    zt = _dot_nt(wt_ref[...], h)
    cos = cos_ref[...]
    sin = sin_ref[...]
    qt_ref[...] = (_rope_rows(zt[ZT_Q:ZT_K], N_HEADS, cos, sin) * LOG2_E).astype(jnp.bfloat16)
    k = _rope_rows(zt[ZT_K:ZT_V], N_KV_HEADS, cos, sin).T
    k_ref[...] = k
    tail = jnp.where(lax.broadcasted_iota(jnp.int32, (k.shape[0], KEY_LANES - HEAD_DIM), 1) == 0, 1.0, 0.0)
    for g in range(N_KV_HEADS):
        kb_ref[g] = jnp.concatenate([k[:, g * HEAD_DIM:(g + 1) * HEAD_DIM], tail], axis=1).astype(jnp.bfloat16)
    vt = zt[ZT_V:ZT_IQ]
    vt_ref[0] = vt.astype(jnp.bfloat16)
    v_ref[...] = vt.T
    qit_ref[...] = _rope_rows(zt[ZT_IQ:ZT_IK], IDX_HEADS, cos, sin).astype(jnp.bfloat16)
    ki = _rope_rows(zt[ZT_IK:ZT_IW], 1, cos, sin).T
    ki_ref[...] = ki
    kib_ref[...] = ki.astype(jnp.bfloat16)
    wit_ref[...] = zt[ZT_IW:ZT_ROWS]
    zc = _dot(h, wc_ref[...])
    cb_ref[...] = zc[:, :CONV_DIM]
    u_ref[...] = zc[:, CONV_DIM:2 * CONV_DIM] * zc[:, 2 * CONV_DIM:]


def _in_proj(x, g0, wt, wc, cos_t, sin_t):
    n = x.shape[0]
    tm = ROW_TILE
    nb = n // tm
    f32, bf16 = jnp.float32, jnp.bfloat16
    full = lambda shape: pl.BlockSpec(shape, lambda i: (0,) * len(shape))
    rows = lambda w: pl.BlockSpec((tm, w), lambda i: (i, 0))
    cols = lambda r: pl.BlockSpec((r, tm), lambda i: (0, i))
    out_shape = (
        jax.ShapeDtypeStruct((Q_COLS, n), bf16),
        jax.ShapeDtypeStruct((IQ_COLS, n), bf16),
        jax.ShapeDtypeStruct((IW_ROWS, n), f32),
        jax.ShapeDtypeStruct((nb, KV_COLS, tm), bf16),
        jax.ShapeDtypeStruct((N_KV_HEADS, n, KEY_LANES), bf16),
        jax.ShapeDtypeStruct((n, IDX_DIM), bf16),
        jax.ShapeDtypeStruct((n, KV_COLS), f32),
        jax.ShapeDtypeStruct((n, KV_COLS), f32),
        jax.ShapeDtypeStruct((n, IDX_DIM), f32),
        jax.ShapeDtypeStruct((n, CONV_DIM), f32),
        jax.ShapeDtypeStruct((n, CONV_DIM), f32),
    )
    out_specs = (
        cols(Q_COLS), cols(IQ_COLS), cols(IW_ROWS),
        pl.BlockSpec((1, KV_COLS, tm), lambda i: (i, 0, 0)),
        pl.BlockSpec((N_KV_HEADS, tm, KEY_LANES), lambda i: (0, i, 0)),
        rows(IDX_DIM), rows(KV_COLS), rows(KV_COLS), rows(IDX_DIM), rows(CONV_DIM), rows(CONV_DIM),
    )
    return pl.pallas_call(
        _in_proj_kernel,
        grid=(nb,),
        in_specs=[rows(D_MODEL), full((1, D_MODEL)), full((ZT_ROWS, D_MODEL)), full((D_MODEL, 3 * CONV_DIM)),
                  cols(ROT_HALF), cols(ROT_HALF)],
        out_specs=out_specs,
        out_shape=out_shape,
        compiler_params=pltpu.CompilerParams(dimension_semantics=("arbitrary",), vmem_limit_bytes=VMEM_LIMIT),
        name="in_proj",
    )(x, g0, wt, wc, cos_t, sin_t)


def _attn_kernel(qt_ref, qit_ref, wit_ref, kb_ref, kib_ref, vt_ref, tri_ref, o_ref,
                 skey_ref, ot_ref, qg_ref, acc_ref, lg_ref, p_ref, kmax_ref, *, tq, kb_rows, n_kblocks, count_blocks,
                 causal, first_key, topk):
    j = pl.program_id(1)
    nkb = (j + 1) if causal else n_kblocks
    i32 = jnp.int32

    q_row = j * tq + lax.broadcasted_iota(i32, (1, tq), 1)
    if causal:
        lim = jnp.where(q_row >= first_key, ((q_row >> CHUNK_SHIFT) << CHUNK_SHIFT) + CHUNK, 0)
    else:
        lim = jnp.full((1, tq), n_kblocks * kb_rows, i32)

    def key_rows(kb):
        return kb * kb_rows + lax.broadcasted_iota(i32, (kb_rows, 1), 0)

    def kslice(kb):
        return pl.ds(pl.multiple_of(kb * kb_rows, SUBLANES), kb_rows)

    def to_key(f):
        bits = lax.bitcast_convert_type(f, i32)
        return bits ^ ((bits >> 31) & 0x7FFFFFFF)

    def to_f32(k):
        return lax.bitcast_convert_type(k ^ ((k >> 31) & 0x7FFFFFFF), jnp.float32)

    def score_block(kb, edge):
        ki = kib_ref[kslice(kb), :]
        acc = jnp.zeros((kb_rows, tq), jnp.float32)
        for h in range(IDX_HEADS):
            s = _dot(ki, qit_ref[h * IDX_DIM:(h + 1) * IDX_DIM, :])
            acc = acc + wit_ref[h:h + 1, :] * jnp.maximum(s, 0.0)
        acc = jnp.where(acc == 0.0, 0.0, acc)
        if edge:
            rows = key_rows(kb)
            adm = jnp.logical_and(rows >= first_key, rows < lim)
            sk = to_key(jnp.where(adm, acc, -jnp.inf))
            masked = jnp.where(adm, acc, 0.0)
            scale = (jnp.sum(masked * masked, axis=0, keepdims=True),
                     jnp.sum(jnp.where(adm, 1.0, 0.0), axis=0, keepdims=True))
        else:
            sk = to_key(acc)
            scale = None
        skey_ref[kslice(kb), :] = sk
        return jnp.sum(jnp.where(sk > ZERO_KEY, 1, 0), axis=0, keepdims=True), scale

    zero = jnp.zeros((1, tq), jnp.float32)
    izero = jnp.zeros((1, tq), i32)
    c_pos, (s2, s0) = score_block(0, True)
    n_inner_pairs = (nkb - 2) // 2
    c_pos = lax.fori_loop(
        0, n_inner_pairs,
        lambda p, c: c + score_block(2 * p + 1, False)[0] + score_block(2 * p + 2, False)[0], c_pos)
    n_left = nkb - 1 - 2 * jnp.maximum(n_inner_pairs, 0)

    def last_block(c):
        c_l, (s2_l, s0_l) = score_block(nkb - 1, True)
        return c + c_l, s2 + s2_l, s0 + s0_l

    c_pos, s2, s0 = lax.cond(
        n_left == 2, lambda: last_block(c_pos + score_block(nkb - 2, False)[0]),
        lambda: lax.cond(n_left == 1, lambda: last_block(c_pos), lambda: (c_pos, s2, s0)))

    cslice_rows = count_blocks * kb_rows
    n_steps = (nkb + count_blocks - 1) // count_blocks
    if count_blocks > 1:
        skey_ref[kslice(nkb), :] = jnp.full((kb_rows, tq), INT_MIN, i32)

    assert cslice_rows % (SUM_CHAINS * SUBLANES) == 0

    def count_ge(cand):
        def body(c, acc):
            sk = skey_ref[pl.ds(pl.multiple_of(c * cslice_rows, SUBLANES), cslice_rows), :]
            ind = jnp.where(sk >= cand, 1, 0)
            return acc + jnp.sum(ind.reshape(SUM_CHAINS, -1, SUBLANES, tq), axis=1)
        acc = lax.fori_loop(0, n_steps, body, jnp.zeros((SUM_CHAINS, SUBLANES, tq), i32))
        return jnp.sum(acc.reshape(SUM_CHAINS * SUBLANES, tq), axis=0, keepdims=True)

    n_adm = jnp.maximum(lim - first_key, 0)
    n_f = jnp.maximum(n_adm, 1).astype(jnp.float32)
    sigma = jnp.sqrt(s2 / jnp.maximum(s0, 1.0))
    tail = jnp.clip(topk / n_f, 1e-6, 1.0 - 1e-6)
    tq_ = jnp.sqrt(-2.0 * jnp.log(jnp.minimum(tail, 1.0 - tail)))
    zq = tq_ - (2.515517 + 0.802853 * tq_ + 0.010328 * tq_ * tq_) / (
        1.0 + 1.432788 * tq_ + 0.189269 * tq_ * tq_ + 0.001308 * tq_ * tq_ * tq_)
    zq = jnp.where(tail < 0.5, zq, -zq)
    step_scale = 1.5 * sigma / jnp.maximum(jnp.abs(zq), 0.5)
    log_k = math.log(topk + 0.5)

    def search_cond(st):
        return jnp.logical_and(st[0] < SEARCH_CAP, st[1] > 0)

    def search_body(st):
        it, _, lo, hi, c_lo, c_hi, lo_real, hi_real, w_lo, w_hi, last, stall, done = st
        f_lo, f_hi = to_f32(lo), to_f32(hi)
        g_lo = jnp.log(c_lo.astype(jnp.float32)) - log_k
        g_hi = log_k - jnp.log(jnp.maximum(c_hi.astype(jnp.float32), 0.5))
        t_in = f_lo + (f_hi - f_lo) * (w_lo * g_lo / (w_lo * g_lo + w_hi * g_hi))
        t_up = f_lo + step_scale * g_lo
        t_dn = f_hi - step_scale * g_hi
        t = jnp.where(lo_real > 0, jnp.where(hi_real > 0, t_in, t_up), t_dn)
        cand = jnp.minimum(jnp.maximum(to_key(t), lo + 1), hi - 1)
        mid = (lo >> 1) + (hi >> 1) + (lo & hi & 1)
        cand = jnp.where(stall >= STALL_STEPS, mid, jnp.where(it >= INTERP_STEPS, mid, cand))
        cand = jnp.where(it == 0, jnp.where(below, ZERO_KEY, cand), cand)
        cnt = count_ge(cand)
        up = jnp.where(done > 0, 0, jnp.where(cnt >= topk, 1, 0))
        dn = jnp.where(done > 0, 0, jnp.where(cnt >= topk, 0, 1))
        both = lo_real * hi_real
        stall = stall + both * jnp.where(cnt == c_lo, 1, jnp.where(cnt == c_hi, 1, 0))
        lo = jnp.where(up > 0, cand, lo)
        c_lo = jnp.where(up > 0, cnt, c_lo)
        lo_real = jnp.maximum(lo_real, up)
        hi = jnp.where(dn > 0, cand, hi)
        c_hi = jnp.where(dn > 0, cnt, c_hi)
        hi_real = jnp.maximum(hi_real, dn)
        w_hi = jnp.where(up > 0, jnp.where(last > 0, 0.5 * w_hi, 1.0), jnp.where(dn > 0, 1.0, w_hi))
        w_lo = jnp.where(dn > 0, jnp.where(last < 0, 0.5 * w_lo, 1.0), jnp.where(up > 0, 1.0, w_lo))
        last = up - dn + (1 - up - dn) * last
        done = jnp.where(c_lo == topk, 1, jnp.where(hi == lo + 1, 1, done))
        return (it + 1, jnp.sum(1 - done), lo, hi, c_lo, c_hi, lo_real, hi_real, w_lo, w_hi, last, stall, done)

    small = n_adm <= topk
    above = c_pos >= topk
    below = jnp.where(small, 0, jnp.where(above, 0, 1)) > 0
    lo0 = jnp.where(small, NEG_INF_KEY, jnp.where(above, ZERO_KEY + 1, MIN_FINITE_KEY))
    hi0 = jnp.where(small, NEG_INF_KEY + 1, jnp.where(above, POS_INF_KEY, ZERO_KEY + 1))
    c_lo0 = jnp.where(small, topk, jnp.where(above, c_pos, n_adm))
    c_hi0 = jnp.where(small, 0, jnp.where(above, 0, c_pos))
    lo_real0 = jnp.where(small, 0, jnp.where(above, 1, 0))
    hi_real0 = jnp.where(small, 0, jnp.where(above, 0, 1))
    done0 = jnp.where(c_lo0 == topk, 1, jnp.where(hi0 == lo0 + 1, 1, 0))
    one = jnp.ones((1, tq), jnp.float32)
    st = lax.while_loop(search_cond, search_body,
                        (i32(0), jnp.sum(1 - done0), lo0, hi0, c_lo0, c_hi0, lo_real0, hi_real0, one, one,
                         izero, izero, done0))
    thr, hi, c_hi = st[2], st[3], st[5]
    need = jnp.where(small, 0, jnp.where(hi == thr + 1, topk - c_hi, TAKE_ALL)).astype(jnp.float32)

    def block_bias(kb, run):
        sk = skey_ref[kslice(kb), :]
        tie = jnp.where(sk == thr, 1.0, 0.0)
        rank = _dot(tri_ref[...], tie.astype(jnp.bfloat16)) + run
        tie_bias = jnp.where(rank <= need, 0.0, NEG_BIAS)
        bias = jnp.where(sk > thr, 0.0, jnp.where(sk == thr, tie_bias, NEG_BIAS))
        return bias, rank[kb_rows - 1:kb_rows, :]

    gw = N_GROUPS * tq
    groups = range(N_KV_HEADS)
    lane = lax.broadcasted_iota(i32, (1, LANES), 1)

    @pl.when(j == 0)
    def _():
        def norm_body(kb, best):
            out = []
            for g in groups:
                k = jnp.where(lane < HEAD_DIM, kb_ref[g, kslice(kb), :].astype(jnp.float32), 0.0)
                n2 = jnp.max(jnp.sum(k * k, axis=1, keepdims=True), axis=0, keepdims=True)
                out.append(jnp.maximum(best[g], n2))
            return tuple(out)
        best = lax.fori_loop(0, n_kblocks, norm_body, (jnp.zeros((1, 1), jnp.float32),) * N_KV_HEADS)
        for g in groups:
            kmax_ref[g] = jnp.broadcast_to(best[g], (SUBLANES, LANES))

    pad_row = lax.broadcasted_iota(i32, (KEY_LANES - HEAD_DIM, 1), 0)
    for g in groups:
        q = jnp.concatenate([qt_ref[h * HEAD_DIM:(h + 1) * HEAD_DIM, :]
                             for h in range(g * N_GROUPS, (g + 1) * N_GROUPS)], axis=1)
        qf = q.astype(jnp.float32)
        bound = jnp.sqrt(jnp.sum(qf * qf, axis=0, keepdims=True) * kmax_ref[g][0:1, 0:1]) * BOUND_MARGIN
        pad = jnp.where(pad_row == 0, -bound, 0.0).astype(jnp.bfloat16)
        qg_ref[g] = jnp.concatenate([q, pad], axis=0)

    def value_rows(kb, g):
        ones = jnp.ones((SUBLANES, kb_rows), jnp.bfloat16)
        return jnp.concatenate([vt_ref[kb, g * HEAD_DIM:(g + 1) * HEAD_DIM, :], ones], axis=0)

    def pipelined(logits_stage, value_stage, ms):
        def step(kb, slot, ms, staged, prefetch):
            extra, run = staged
            nxt = logits_stage(kb + 1, 1 - slot, run) if prefetch else None
            return tuple(value_stage(kb, g, slot, ms[g], extra[g]) for g in groups), nxt

        def pair_body(i, carry):
            ms, staged = carry
            ms, staged = step(2 * i, 0, ms, staged, True)
            return step(2 * i + 1, 1, ms, staged, True)

        n_pairs = (nkb - 1) // 2
        ms, staged = lax.fori_loop(0, n_pairs, pair_body, (ms, logits_stage(0, 0, zero)))
        last = 2 * n_pairs

        def two_left():
            ms1, staged1 = step(last, 0, ms, staged, True)
            return step(last + 1, 1, ms1, staged1, False)[0]

        return lax.cond(nkb - last == 2, two_left, lambda: step(last, 0, ms, staged, False)[0])

    def logits(kb, g, bias):
        return _dot(kb_ref[g, kslice(kb), :], qg_ref[g]) + bias

    def bounded_logits_stage(kb, slot, run):
        bias, run = block_bias(kb, run)
        bias = jnp.concatenate([bias] * N_GROUPS, axis=1)
        for g in groups:
            p_ref[slot, g] = jnp.exp2(logits(kb, g, bias)).astype(jnp.bfloat16)
        return (izero,) * N_KV_HEADS, run

    def bounded_value_stage(kb, g, slot, m, unused):
        acc_ref[g] = acc_ref[g] + _dot(value_rows(kb, g), p_ref[slot, g])
        return m

    def online_logits_stage(kb, slot, run):
        bias, run = block_bias(kb, run)
        bias = jnp.concatenate([bias] * N_GROUPS, axis=1)
        m_blks = []
        for g in groups:
            lg = logits(kb, g, bias)
            lg_ref[slot, g] = lg
            m_blks.append(jnp.max(lg, axis=0, keepdims=True))
        return tuple(m_blks), run

    def online_value_stage(kb, g, slot, m, m_blk):
        m_new = jnp.maximum(m, m_blk)
        p = jnp.exp2(lg_ref[slot, g] - m_new).astype(jnp.bfloat16)
        acc_ref[g] = jnp.exp2(m - m_new) * acc_ref[g] + _dot(value_rows(kb, g), p)
        return m_new

    has_key = jnp.concatenate([n_adm] * N_GROUPS, axis=1) > 0
    acc_ref[...] = jnp.zeros_like(acc_ref)
    pipelined(bounded_logits_stage, bounded_value_stage, (izero,) * N_KV_HEADS)
    underflow = izero[:, 0:1]
    for g in groups:
        denom = acc_ref[g][HEAD_DIM:HEAD_DIM + 1, :]
        underflow = underflow + jnp.sum(jnp.where(has_key, jnp.where(denom < MIN_DENOMINATOR, 1, 0), 0),
                                        axis=1, keepdims=True)

    @pl.when(underflow[0, 0] > 0)
    def _():
        acc_ref[...] = jnp.zeros_like(acc_ref)
        for g in groups:
            qg_ref[g, HEAD_DIM:, :] = jnp.zeros((KEY_LANES - HEAD_DIM, gw), jnp.bfloat16)
        pipelined(online_logits_stage, online_value_stage,
                  (jnp.full((1, gw), NEG_BIAS, jnp.float32),) * N_KV_HEADS)

    for g in groups:
        acc = acc_ref[g]
        og = jnp.where(has_key, acc[:HEAD_DIM] / acc[HEAD_DIM:HEAD_DIM + 1], 0.0)
        for hh in range(N_GROUPS):
            h = g * N_GROUPS + hh
            ot_ref[h * HEAD_DIM:(h + 1) * HEAD_DIM, :] = og[:, hh * tq:(hh + 1) * tq]

    o_ref[...] = ot_ref[...].T.astype(o_ref.dtype)


def _attention(qt, qit, wit, kb, kib, vt, *, n_seq, lq, lk, tq, kb_rows, causal, first_key, topk):
    nq = lq // tq
    n_kblocks = lk // kb_rows
    tri = jnp.tril(jnp.ones((kb_rows, kb_rows), jnp.bfloat16))
    count_blocks = 2 if causal else 1
    kernel = functools.partial(_attn_kernel, tq=tq, kb_rows=kb_rows, n_kblocks=n_kblocks,
                               count_blocks=count_blocks, causal=causal, first_key=first_key, topk=topk)
    return pl.pallas_call(
        kernel,
        grid=(n_seq, nq),
        in_specs=[
            pl.BlockSpec((Q_COLS, tq), lambda b, j: (0, b * nq + j)),
            pl.BlockSpec((IQ_COLS, tq), lambda b, j: (0, b * nq + j)),
            pl.BlockSpec((IW_ROWS, tq), lambda b, j: (0, b * nq + j)),
            pl.BlockSpec((N_KV_HEADS, lk, KEY_LANES), lambda b, j: (0, b, 0)),
            pl.BlockSpec((lk, IDX_DIM), lambda b, j: (b, 0)),
            pl.BlockSpec((n_kblocks, KV_COLS, kb_rows), lambda b, j: (b, 0, 0)),
            pl.BlockSpec((kb_rows, kb_rows), lambda b, j: (0, 0)),
        ],
        out_specs=pl.BlockSpec((tq, Q_COLS), lambda b, j: (b * nq + j, 0)),
        out_shape=jax.ShapeDtypeStruct((n_seq * lq, Q_COLS), jnp.bfloat16),
        scratch_shapes=[
            pltpu.VMEM((lk + (count_blocks - 1) * kb_rows, tq), jnp.int32),
            pltpu.VMEM((Q_COLS, tq), jnp.float32),
            pltpu.VMEM((N_KV_HEADS, KEY_LANES, N_GROUPS * tq), jnp.bfloat16),
            pltpu.VMEM((N_KV_HEADS, HEAD_DIM + SUBLANES, N_GROUPS * tq), jnp.float32),
            pltpu.VMEM((2, N_KV_HEADS, kb_rows, N_GROUPS * tq), jnp.float32),
            pltpu.VMEM((2, N_KV_HEADS, kb_rows, N_GROUPS * tq), jnp.bfloat16),
            pltpu.VMEM((N_KV_HEADS, SUBLANES, LANES), jnp.float32),
        ],
        compiler_params=pltpu.CompilerParams(dimension_semantics=("arbitrary", "arbitrary"),
                                             vmem_limit_bytes=VMEM_LIMIT),
        name="dsa_attention_causal" if causal else "dsa_attention_full",
    )(qt, qit, wit, kb, kib, vt, tri)


def _mix_out_kernel(x_ref, o_ref, u_ref, halo_ref, cb_ref, g0_ref, g1_ref, cw_ref,
                    wg_ref, wao_ref, wco_ref, wo_ref, x1_ref, *, n_seg):
    x = x_ref[...]
    h = _rms(x, g0_ref[...]).astype(jnp.bfloat16)
    gates = _dot(h, wg_ref[...])
    y_a = _dot(o_ref[...], wao_ref[...])
    u = u_ref[...]
    halo = halo_ref[0]
    t = u.shape[0]
    seg = t // n_seg
    row = lax.broadcasted_iota(jnp.int32, (t, 1), 0)
    u1 = pltpu.roll(u, 1, 0)
    u2 = pltpu.roll(u, 2, 0)
    for s in range(n_seg):
        h6 = halo[s * SUBLANES + 6:s * SUBLANES + 7]
        h7 = halo[s * SUBLANES + 7:s * SUBLANES + 8]
        u1 = jnp.where(row == s * seg, h7, u1)
        u2 = jnp.where(row == s * seg, h6, jnp.where(row == s * seg + 1, h7, u2))
    conv = cw_ref[0:1] * u2 + cw_ref[1:2] * u1 + cw_ref[2:3] * u
    y_b = _dot((cb_ref[...] * conv).astype(jnp.bfloat16), wco_ref[...])
    m = _sigmoid(gates[:, :D_MODEL]) * y_a + _sigmoid(gates[:, D_MODEL:]) * y_b
    a = _dot(m.astype(jnp.bfloat16), wo_ref[...])
    x1_ref[...] = x + _rms(a, g1_ref[...])


def _mix_out(x, o, u, halo, cb, g0, g1, cw, wg, wao, wco, wo):
    n = x.shape[0]
    tm = ROW_TILE
    n_seg = halo.shape[1] // SUBLANES
    full = lambda shape: pl.BlockSpec(shape, lambda i: (0,) * len(shape))
    rows = lambda w: pl.BlockSpec((tm, w), lambda i: (i, 0))
    return pl.pallas_call(
        functools.partial(_mix_out_kernel, n_seg=n_seg),
        grid=(n // tm,),
        in_specs=[rows(D_MODEL), rows(Q_COLS), rows(CONV_DIM),
                  pl.BlockSpec((1, n_seg * SUBLANES, CONV_DIM), lambda i: (i, 0, 0)), rows(CONV_DIM),
                  full((1, D_MODEL)), full((1, D_MODEL)), full((CONV_WIDTH, CONV_DIM)),
                  full((D_MODEL, 2 * D_MODEL)), full((Q_COLS, D_MODEL)), full((CONV_DIM, D_MODEL)),
                  full((D_MODEL, D_MODEL))],
        out_specs=rows(D_MODEL),
        out_shape=jax.ShapeDtypeStruct((n, D_MODEL), jnp.float32),
        compiler_params=pltpu.CompilerParams(dimension_semantics=("arbitrary",), vmem_limit_bytes=VMEM_LIMIT),
        name="mix_out",
    )(x, o, u, halo, cb, g0, g1, cw, wg, wao, wco, wo)


def _mlp_kernel(x_ref, g2_ref, g3_ref, wup_ref, wdn_ref, x2_ref):
    x = x_ref[...]
    h = _rms(x, g2_ref[...]).astype(jnp.bfloat16)
    up = jnp.maximum(_dot(h, wup_ref[...]), 0.0)
    f = _dot((up * up).astype(jnp.bfloat16), wdn_ref[...])
    x2_ref[...] = x + _rms(f, g3_ref[...])


def _mlp(x, g2, g3, wup, wdn):
    n = x.shape[0]
    tm = ROW_TILE
    full = lambda shape: pl.BlockSpec(shape, lambda i: (0,) * len(shape))
    rows = lambda w: pl.BlockSpec((tm, w), lambda i: (i, 0))
    return pl.pallas_call(
        _mlp_kernel,
        grid=(n // tm,),
        in_specs=[rows(D_MODEL), full((1, D_MODEL)), full((1, D_MODEL)), full((D_MODEL, D_FF)),
                  full((D_FF, D_MODEL))],
        out_specs=rows(D_MODEL),
        out_shape=jax.ShapeDtypeStruct((n, D_MODEL), jnp.float32),
        compiler_params=pltpu.CompilerParams(dimension_semantics=("arbitrary",), vmem_limit_bytes=VMEM_LIMIT),
        name="mlp",
    )(x, g2, g3, wup, wdn)


def _layer_weights(w_in, conv_w, w_attn_out, w_conv_out, w_o, w_up, w_down):
    bf16 = jnp.bfloat16
    wt = jnp.concatenate([
        w_in[:, :END_Q] * (HEAD_DIM ** -0.5),
        w_in[:, END_Q:END_V],
        w_in[:, END_V:END_IQ] * (IDX_DIM ** -0.5),
        w_in[:, END_IQ:END_IK],
        w_in[:, END_IK:END_IW] * (IDX_HEADS ** -0.5),
        jnp.zeros((D_MODEL, IW_ROWS - IDX_HEADS), w_in.dtype),
    ], axis=1).T.astype(bf16)
    return dict(
        wt=wt,
        wc=w_in[:, END_IW:END_CX].astype(bf16),
        wg=w_in[:, END_CX:].astype(bf16),
        cw=conv_w,
        wao=w_attn_out.astype(bf16), wco=w_conv_out.astype(bf16), wo=w_o.astype(bf16),
        wup=w_up.astype(bf16), wdn=w_down.astype(bf16))


def _rope_tables(pos):
    inv = jnp.exp(-math.log(ROPE_THETA) * jnp.arange(ROT_HALF, dtype=jnp.float32) * (2.0 / ROT_DIM))
    ang = inv[:, None] * pos.astype(jnp.float32)[None, :]
    return jnp.cos(ang), jnp.sin(ang)


def _halo(u, n_seq, past):
    n = u.shape[0]
    per_seq = n // n_seq
    first = jnp.concatenate([jnp.zeros((n_seq, SUBLANES - (CONV_WIDTH - 1), CONV_DIM), u.dtype), past], axis=1)
    if per_seq >= ROW_TILE:
        tiles = per_seq // ROW_TILE
        tails = u.reshape(n_seq, tiles, ROW_TILE, CONV_DIM)[:, :tiles - 1, ROW_TILE - SUBLANES:]
        return jnp.concatenate([first[:, None], tails], axis=1).reshape(n_seq * tiles, SUBLANES, CONV_DIM)
    n_seg = ROW_TILE // per_seq
    return first.reshape(n_seq // n_seg, n_seg * SUBLANES, CONV_DIM)


def _with_ones_column(k):
    tail = jnp.zeros(k.shape[:-1] + (KEY_LANES - HEAD_DIM,), k.dtype).at[..., 0].set(1)
    return jnp.concatenate([k, tail], axis=-1)


def _pad_lanes(a, n_seq, per_seq, width):
    a = a.reshape(a.shape[0], n_seq, per_seq)
    a = jnp.pad(a, ((0, 0), (0, 0), (0, width - per_seq)))
    return a.reshape(a.shape[0], n_seq * width)


def kernel(x_prompt, x_sample, cache_k, cache_v, cache_kidx, state_conv, meta_tokens, norm_gains, w_in, conv_w,
           w_attn_out, w_conv_out, w_o, w_up, w_down):
    f32, bf16 = jnp.float32, jnp.bfloat16
    depth = w_in.shape[0]
    bp, seq, _ = x_prompt.shape
    bs, s_len, _ = x_sample.shape
    past_len = cache_k.shape[2]
    assert seq % Q_TILE == 0 and Q_TILE % CHUNK == 0 and ROW_TILE == Q_TILE and N_META <= Q_TILE
    assert (bs * s_len) % ROW_TILE == 0 and ROW_TILE % s_len == 0
    assert CONV_WIDTH - 1 <= s_len <= LANES and (past_len + s_len) % SUBLANES == 0

    pad = Q_TILE - N_META
    rp = Q_TILE + seq
    xp = jnp.concatenate([
        jnp.zeros((bp, pad, D_MODEL), f32),
        jnp.broadcast_to(meta_tokens[None].astype(f32), (bp, N_META, D_MODEL)),
        x_prompt.astype(f32)], axis=1).reshape(bp * rp, D_MODEL)
    cos_p, sin_p = _rope_tables(jnp.tile(jnp.arange(rp, dtype=jnp.int32) - pad, bp))
    topk_p = min(TOPK_MAX, seq // 4)

    xs = x_sample.astype(f32).reshape(bs * s_len, D_MODEL)
    ls = past_len + s_len
    cos_s, sin_s = _rope_tables(jnp.tile(past_len + jnp.arange(s_len, dtype=jnp.int32), bs))
    topk_s = min(TOPK_MAX, ls // 4)
    conv_zero = jnp.zeros((bp, CONV_WIDTH - 1, CONV_DIM), f32)

    outs = [[] for _ in range(8)]
    for l in range(depth):
        w = _layer_weights(w_in[l], conv_w[l], w_attn_out[l], w_conv_out[l], w_o[l], w_up[l], w_down[l])
        g = [norm_gains[l, i][None, :] for i in range(4)]

        qt, qit, wit, vt, kb, kib, k32, v32, ki32, u, cb = _in_proj(xp, g[0], w["wt"], w["wc"], cos_p, sin_p)
        o = _attention(qt, qit, wit, kb, kib, vt, n_seq=bp, lq=rp, lk=rp, tq=Q_TILE, kb_rows=ROW_TILE,
                       causal=True, first_key=pad, topk=topk_p)
        x1 = _mix_out(xp, o, u, _halo(u, bp, conv_zero), cb, g[0], g[1], w["cw"], w["wg"], w["wao"], w["wco"],
                      w["wo"])
        xp = _mlp(x1, g[2], g[3], w["wup"], w["wdn"])
        outs[0].append(k32.reshape(bp, rp, N_KV_HEADS, HEAD_DIM)[:, pad:])
        outs[1].append(v32.reshape(bp, rp, N_KV_HEADS, HEAD_DIM)[:, pad:])
        outs[2].append(ki32.reshape(bp, rp, IDX_DIM)[:, pad:])
        outs[3].append(u.reshape(bp, rp, CONV_DIM)[:, rp - (CONV_WIDTH - 1):])

        qt, qit, wit, vt, kb, kib, k32, v32, ki32, u, cb = _in_proj(xs, g[0], w["wt"], w["wc"], cos_s, sin_s)
        k_new = k32.reshape(bs, s_len, N_KV_HEADS, HEAD_DIM)
        v_new = v32.reshape(bs, s_len, N_KV_HEADS, HEAD_DIM)
        ki_new = ki32.reshape(bs, s_len, IDX_DIM)
        k_all = jnp.concatenate([cache_k[l], k_new], axis=1).astype(bf16)
        v_all = jnp.concatenate([cache_v[l], v_new], axis=1).astype(bf16)
        ki_all = jnp.concatenate([cache_kidx[l], ki_new], axis=1).astype(bf16)
        o = _attention(
            _pad_lanes(qt, bs, s_len, LANES), _pad_lanes(qit, bs, s_len, LANES), _pad_lanes(wit, bs, s_len, LANES),
            _with_ones_column(k_all.transpose(2, 0, 1, 3).reshape(N_KV_HEADS, bs * ls, HEAD_DIM)),
            ki_all.reshape(bs * ls, IDX_DIM),
            v_all.reshape(bs, ls, KV_COLS).transpose(0, 2, 1),
            n_seq=bs, lq=LANES, lk=ls, tq=LANES, kb_rows=ls, causal=False, first_key=0, topk=topk_s)
        o = o.reshape(bs, LANES, Q_COLS)[:, :s_len].reshape(bs * s_len, Q_COLS)
        x1 = _mix_out(xs, o, u, _halo(u, bs, state_conv[l].astype(f32)), cb, g[0], g[1], w["cw"], w["wg"],
                      w["wao"], w["wco"], w["wo"])
        xs = _mlp(x1, g[2], g[3], w["wup"], w["wdn"])
        outs[4].append(k_new)
        outs[5].append(v_new)
        outs[6].append(ki_new)
        outs[7].append(u.reshape(bs, s_len, CONV_DIM)[:, s_len - (CONV_WIDTH - 1):])

    y_prompt = xp.reshape(bp, rp, D_MODEL)[:, Q_TILE:]
    y_sample = xs.reshape(bs, s_len, D_MODEL)
    return (y_prompt, y_sample) + tuple(jnp.stack(o) for o in outs)
```

```python
import functools
import math

import jax
import jax.numpy as jnp
from jax import lax
from jax.experimental import pallas as pl
from jax.experimental.pallas import tpu as pltpu

D_MODEL = 1024
CHUNK = 64
CHUNK_SHIFT = CHUNK.bit_length() - 1
N_META = 16
HEAD_DIM = 64
N_HEADS = 8
N_KV_HEADS = 2
N_GROUPS = N_HEADS // N_KV_HEADS
IDX_HEADS = 4
IDX_DIM = 64
ROT_DIM = HEAD_DIM // 4
ROT_HALF = ROT_DIM // 2
ROPE_THETA = 500000.0
CONV_DIM = D_MODEL // 2
CONV_WIDTH = 3
D_FF = 4 * D_MODEL
TOPK_MAX = 256
EPS = 1e-6

Q_COLS = N_HEADS * HEAD_DIM
KV_COLS = N_KV_HEADS * HEAD_DIM
IQ_COLS = IDX_HEADS * IDX_DIM
END_Q = Q_COLS
END_K = END_Q + KV_COLS
END_V = END_K + KV_COLS
END_IQ = END_V + IQ_COLS
END_IK = END_IQ + IDX_DIM
END_IW = END_IK + IDX_HEADS
END_CB = END_IW + CONV_DIM
END_CC = END_CB + CONV_DIM
END_CX = END_CC + CONV_DIM
END_GA = END_CX + D_MODEL
D_IN = END_GA + D_MODEL

SUBLANES = 8
LANES = 128
IW_ROWS = SUBLANES
ZT_Q = 0
ZT_K = ZT_Q + Q_COLS
ZT_V = ZT_K + KV_COLS
ZT_IQ = ZT_V + KV_COLS
ZT_IK = ZT_IQ + IQ_COLS
ZT_IW = ZT_IK + IDX_DIM
ZT_ROWS = ZT_IW + IW_ROWS

ROW_TILE = 256
Q_TILE = 256
NEG_BIAS = -1e30
LOG2_E = math.log2(math.e)
KEY_LANES = 128
BOUND_MARGIN = 1.01
MIN_DENOMINATOR = 2.0 ** -60
INT_MIN = -(2 ** 31)
NEG_INF_KEY = INT_MIN + 0x7FFFFF
MIN_FINITE_KEY = NEG_INF_KEY + 1
POS_INF_KEY = 0x7F800000
INT_MAX = 2 ** 31 - 1
ZERO_KEY = 0
TAKE_ALL = 2 ** 30
INTERP_STEPS = 20
STALL_STEPS = 2
SUM_CHAINS = 4
SEARCH_CAP = INTERP_STEPS + 34
VMEM_LIMIT = 56 * 1024 * 1024


def _rms(x, g):
    return x * lax.rsqrt(jnp.mean(x * x, axis=-1, keepdims=True) + EPS) * g


def _sigmoid(x):
    return 1.0 / (1.0 + jnp.exp(-x))


def _dot(a, b):
    return jnp.dot(a, b, preferred_element_type=jnp.float32)


def _dot_nt(a, b):
    return lax.dot_general(a, b, (((1,), (1,)), ((), ())), preferred_element_type=jnp.float32)


def _rope_rows(zt, n_heads, cos, sin):
    pieces = []
    for h in range(n_heads):
        o = h * HEAD_DIM
        x1 = zt[o:o + ROT_HALF]
        x2 = zt[o + ROT_HALF:o + ROT_DIM]
        pieces.append(x1 * cos - x2 * sin)
        pieces.append(x2 * cos + x1 * sin)
        pieces.append(zt[o + ROT_DIM:o + HEAD_DIM])
    return jnp.concatenate(pieces, axis=0)


def _in_proj_kernel(x_ref, g_ref, wt_ref, wc_ref, cos_ref, sin_ref,
                    qt_ref, qit_ref, wit_ref, vt_ref, kb_ref, kib_ref,
                    k_ref, v_ref, ki_ref, u_ref, cb_ref):
    h = _rms(x_ref[...], g_ref[...]).astype(jnp.bfloat16)
    zt = _dot_nt(wt_ref[...], h)
    cos = cos_ref[...]
    sin = sin_ref[...]
    qt_ref[...] = (_rope_rows(zt[ZT_Q:ZT_K], N_HEADS, cos, sin) * LOG2_E).astype(jnp.bfloat16)
    k = _rope_rows(zt[ZT_K:ZT_V], N_KV_HEADS, cos, sin).T
    k_ref[...] = k
    tail = jnp.where(lax.broadcasted_iota(jnp.int32, (k.shape[0], KEY_LANES - HEAD_DIM), 1) == 0, 1.0, 0.0)
    for g in range(N_KV_HEADS):
        kb_ref[g] = jnp.concatenate([k[:, g * HEAD_DIM:(g + 1) * HEAD_DIM], tail], axis=1).astype(jnp.bfloat16)
    vt = zt[ZT_V:ZT_IQ]
    vt_ref[0] = vt.astype(jnp.bfloat16)
    v_ref[...] = vt.T
    qit_ref[...] = _rope_rows(zt[ZT_IQ:ZT_IK], IDX_HEADS, cos, sin).astype(jnp.bfloat16)
    ki = _rope_rows(zt[ZT_IK:ZT_IW], 1, cos, sin).T
    ki_ref[...] = ki
    kib_ref[...] = ki.astype(jnp.bfloat16)
    wit_ref[...] = zt[ZT_IW:ZT_ROWS]
    zc = _dot(h, wc_ref[...])
    cb_ref[...] = zc[:, :CONV_DIM]
    u_ref[...] = zc[:, CONV_DIM:2 * CONV_DIM] * zc[:, 2 * CONV_DIM:]


def _in_proj(x, g0, wt, wc, cos_t, sin_t):
    n = x.shape[0]
    tm = ROW_TILE
    nb = n // tm
    f32, bf16 = jnp.float32, jnp.bfloat16
    full = lambda shape: pl.BlockSpec(shape, lambda i: (0,) * len(shape))
    rows = lambda w: pl.BlockSpec((tm, w), lambda i: (i, 0))
    cols = lambda r: pl.BlockSpec((r, tm), lambda i: (0, i))
    out_shape = (
        jax.ShapeDtypeStruct((Q_COLS, n), bf16),
        jax.ShapeDtypeStruct((IQ_COLS, n), bf16),
        jax.ShapeDtypeStruct((IW_ROWS, n), f32),
        jax.ShapeDtypeStruct((nb, KV_COLS, tm), bf16),
        jax.ShapeDtypeStruct((N_KV_HEADS, n, KEY_LANES), bf16),
        jax.ShapeDtypeStruct((n, IDX_DIM), bf16),
        jax.ShapeDtypeStruct((n, KV_COLS), f32),
        jax.ShapeDtypeStruct((n, KV_COLS), f32),
        jax.ShapeDtypeStruct((n, IDX_DIM), f32),
        jax.ShapeDtypeStruct((n, CONV_DIM), f32),
        jax.ShapeDtypeStruct((n, CONV_DIM), f32),
    )
    out_specs = (
        cols(Q_COLS), cols(IQ_COLS), cols(IW_ROWS),
        pl.BlockSpec((1, KV_COLS, tm), lambda i: (i, 0, 0)),
        pl.BlockSpec((N_KV_HEADS, tm, KEY_LANES), lambda i: (0, i, 0)),
        rows(IDX_DIM), rows(KV_COLS), rows(KV_COLS), rows(IDX_DIM), rows(CONV_DIM), rows(CONV_DIM),
    )
    return pl.pallas_call(
        _in_proj_kernel,
        grid=(nb,),
        in_specs=[rows(D_MODEL), full((1, D_MODEL)), full((ZT_ROWS, D_MODEL)), full((D_MODEL, 3 * CONV_DIM)),
                  cols(ROT_HALF), cols(ROT_HALF)],
        out_specs=out_specs,
        out_shape=out_shape,
        compiler_params=pltpu.CompilerParams(dimension_semantics=("arbitrary",), vmem_limit_bytes=VMEM_LIMIT),
        name="in_proj",
    )(x, g0, wt, wc, cos_t, sin_t)


def _attn_kernel(qt_ref, qit_ref, wit_ref, kb_ref, kib_ref, vt_ref, tri_ref, o_ref,
                 skey_ref, ot_ref, qg_ref, acc_ref, lg_ref, p_ref, kmax_ref, *, tq, kb_rows, n_kblocks, count_blocks,
                 causal, first_key, topk):
    j = pl.program_id(1)
    nkb = (j + 1) if causal else n_kblocks
    i32 = jnp.int32

    q_row = j * tq + lax.broadcasted_iota(i32, (1, tq), 1)
    if causal:
        lim = jnp.where(q_row >= first_key, ((q_row >> CHUNK_SHIFT) << CHUNK_SHIFT) + CHUNK, 0)
    else:
        lim = jnp.full((1, tq), n_kblocks * kb_rows, i32)

    def key_rows(kb):
        return kb * kb_rows + lax.broadcasted_iota(i32, (kb_rows, 1), 0)

    def kslice(kb):
        return pl.ds(pl.multiple_of(kb * kb_rows, SUBLANES), kb_rows)

    def to_key(f):
        bits = lax.bitcast_convert_type(f, i32)
        return bits ^ ((bits >> 31) & 0x7FFFFFFF)

    def to_f32(k):
        return lax.bitcast_convert_type(k ^ ((k >> 31) & 0x7FFFFFFF), jnp.float32)

    def score_block(kb, edge):
        ki = kib_ref[kslice(kb), :]
        acc = jnp.zeros((kb_rows, tq), jnp.float32)
        for h in range(IDX_HEADS):
            s = _dot(ki, qit_ref[h * IDX_DIM:(h + 1) * IDX_DIM, :])
            acc = acc + wit_ref[h:h + 1, :] * jnp.maximum(s, 0.0)
        acc = jnp.where(acc == 0.0, 0.0, acc)
        if edge:
            rows = key_rows(kb)
            adm = jnp.logical_and(rows >= first_key, rows < lim)
            sk = to_key(jnp.where(adm, acc, -jnp.inf))
            masked = jnp.where(adm, acc, 0.0)
            scale = (jnp.sum(masked * masked, axis=0, keepdims=True),
                     jnp.sum(jnp.where(adm, 1.0, 0.0), axis=0, keepdims=True))
        else:
            sk = to_key(acc)
            scale = None
        skey_ref[kslice(kb), :] = sk
        return jnp.sum(jnp.where(sk > ZERO_KEY, 1, 0), axis=0, keepdims=True), scale

    zero = jnp.zeros((1, tq), jnp.float32)
    izero = jnp.zeros((1, tq), i32)
    c_pos, (s2, s0) = score_block(0, True)
    n_inner_pairs = (nkb - 2) // 2
    c_pos = lax.fori_loop(
        0, n_inner_pairs,
        lambda p, c: c + score_block(2 * p + 1, False)[0] + score_block(2 * p + 2, False)[0], c_pos)
    n_left = nkb - 1 - 2 * jnp.maximum(n_inner_pairs, 0)

    def last_block(c):
        c_l, (s2_l, s0_l) = score_block(nkb - 1, True)
        return c + c_l, s2 + s2_l, s0 + s0_l

    c_pos, s2, s0 = lax.cond(
        n_left == 2, lambda: last_block(c_pos + score_block(nkb - 2, False)[0]),
        lambda: lax.cond(n_left == 1, lambda: last_block(c_pos), lambda: (c_pos, s2, s0)))

    cslice_rows = count_blocks * kb_rows
    n_steps = (nkb + count_blocks - 1) // count_blocks
    if count_blocks > 1:
        skey_ref[kslice(nkb), :] = jnp.full((kb_rows, tq), INT_MIN, i32)

    assert cslice_rows % (SUM_CHAINS * SUBLANES) == 0

    def count_ge(cand):
        def body(c, acc):
            sk = skey_ref[pl.ds(pl.multiple_of(c * cslice_rows, SUBLANES), cslice_rows), :]
            ind = jnp.where(sk >= cand, 1, 0)
            return acc + jnp.sum(ind.reshape(SUM_CHAINS, -1, SUBLANES, tq), axis=1)
        acc = lax.fori_loop(0, n_steps, body, jnp.zeros((SUM_CHAINS, SUBLANES, tq), i32))
        return jnp.sum(acc.reshape(SUM_CHAINS * SUBLANES, tq), axis=0, keepdims=True)

    n_adm = jnp.maximum(lim - first_key, 0)
    n_f = jnp.maximum(n_adm, 1).astype(jnp.float32)
    sigma = jnp.sqrt(s2 / jnp.maximum(s0, 1.0))
    tail = jnp.clip(topk / n_f, 1e-6, 1.0 - 1e-6)
    tq_ = jnp.sqrt(-2.0 * jnp.log(jnp.minimum(tail, 1.0 - tail)))
    zq = tq_ - (2.515517 + 0.802853 * tq_ + 0.010328 * tq_ * tq_) / (
        1.0 + 1.432788 * tq_ + 0.189269 * tq_ * tq_ + 0.001308 * tq_ * tq_ * tq_)
    zq = jnp.where(tail < 0.5, zq, -zq)
    step_scale = 1.5 * sigma / jnp.maximum(jnp.abs(zq), 0.5)
    log_k = math.log(topk + 0.5)

    def search_cond(st):
        return jnp.logical_and(st[0] < SEARCH_CAP, st[1] > 0)

    def search_body(st):
        it, _, lo, hi, c_lo, c_hi, lo_real, hi_real, w_lo, w_hi, last, stall, done = st
        f_lo, f_hi = to_f32(lo), to_f32(hi)
        g_lo = jnp.log(c_lo.astype(jnp.float32)) - log_k
        g_hi = log_k - jnp.log(jnp.maximum(c_hi.astype(jnp.float32), 0.5))
        t_in = f_lo + (f_hi - f_lo) * (w_lo * g_lo / (w_lo * g_lo + w_hi * g_hi))
        t_up = f_lo + step_scale * g_lo
        t_dn = f_hi - step_scale * g_hi
        t = jnp.where(lo_real > 0, jnp.where(hi_real > 0, t_in, t_up), t_dn)
        cand = jnp.minimum(jnp.maximum(to_key(t), lo + 1), hi - 1)
        mid = (lo >> 1) + (hi >> 1) + (lo & hi & 1)
        cand = jnp.where(stall >= STALL_STEPS, mid, jnp.where(it >= INTERP_STEPS, mid, cand))
        cand = jnp.where(it == 0, jnp.where(below, ZERO_KEY, cand), cand)
        cnt = count_ge(cand)
        up = jnp.where(done > 0, 0, jnp.where(cnt >= topk, 1, 0))
        dn = jnp.where(done > 0, 0, jnp.where(cnt >= topk, 0, 1))
        both = lo_real * hi_real
        stall = stall + both * jnp.where(cnt == c_lo, 1, jnp.where(cnt == c_hi, 1, 0))
        lo = jnp.where(up > 0, cand, lo)
        c_lo = jnp.where(up > 0, cnt, c_lo)
        lo_real = jnp.maximum(lo_real, up)
        hi = jnp.where(dn > 0, cand, hi)
        c_hi = jnp.where(dn > 0, cnt, c_hi)
        hi_real = jnp.maximum(hi_real, dn)
        w_hi = jnp.where(up > 0, jnp.where(last > 0, 0.5 * w_hi, 1.0), jnp.where(dn > 0, 1.0, w_hi))
        w_lo = jnp.where(dn > 0, jnp.where(last < 0, 0.5 * w_lo, 1.0), jnp.where(up > 0, 1.0, w_lo))
        last = up - dn + (1 - up - dn) * last
        done = jnp.where(c_lo == topk, 1, jnp.where(hi == lo + 1, 1, done))
        return (it + 1, jnp.sum(1 - done), lo, hi, c_lo, c_hi, lo_real, hi_real, w_lo, w_hi, last, stall, done)

    small = n_adm <= topk
    above = c_pos >= topk
    below = jnp.where(small, 0, jnp.where(above, 0, 1)) > 0
    lo0 = jnp.where(small, NEG_INF_KEY, jnp.where(above, ZERO_KEY + 1, MIN_FINITE_KEY))
    hi0 = jnp.where(small, NEG_INF_KEY + 1, jnp.where(above, POS_INF_KEY, ZERO_KEY + 1))
    c_lo0 = jnp.where(small, topk, jnp.where(above, c_pos, n_adm))
    c_hi0 = jnp.where(small, 0, jnp.where(above, 0, c_pos))
    lo_real0 = jnp.where(small, 0, jnp.where(above, 1, 0))
    hi_real0 = jnp.where(small, 0, jnp.where(above, 0, 1))
    done0 = jnp.where(c_lo0 == topk, 1, jnp.where(hi0 == lo0 + 1, 1, 0))
    one = jnp.ones((1, tq), jnp.float32)
    st = lax.while_loop(search_cond, search_body,
                        (i32(0), jnp.sum(1 - done0), lo0, hi0, c_lo0, c_hi0, lo_real0, hi_real0, one, one,
                         izero, izero, done0))
    thr, hi, c_hi = st[2], st[3], st[5]
    need = jnp.where(small, 0, jnp.where(hi == thr + 1, topk - c_hi, TAKE_ALL)).astype(jnp.float32)

    def block_bias(kb, run):
        sk = skey_ref[kslice(kb), :]
        tie = jnp.where(sk == thr, 1.0, 0.0)
        rank = _dot(tri_ref[...], tie.astype(jnp.bfloat16)) + run
        tie_bias = jnp.where(rank <= need, 0.0, NEG_BIAS)
        bias = jnp.where(sk > thr, 0.0, jnp.where(sk == thr, tie_bias, NEG_BIAS))
        return bias, rank[kb_rows - 1:kb_rows, :]

    gw = N_GROUPS * tq
    groups = range(N_KV_HEADS)
    lane = lax.broadcasted_iota(i32, (1, LANES), 1)

    @pl.when(j == 0)
    def _():
        def norm_body(kb, best):
            out = []
            for g in groups:
                k = jnp.where(lane < HEAD_DIM, kb_ref[g, kslice(kb), :].astype(jnp.float32), 0.0)
                n2 = jnp.max(jnp.sum(k * k, axis=1, keepdims=True), axis=0, keepdims=True)
                out.append(jnp.maximum(best[g], n2))
            return tuple(out)
        best = lax.fori_loop(0, n_kblocks, norm_body, (jnp.zeros((1, 1), jnp.float32),) * N_KV_HEADS)
        for g in groups:
            kmax_ref[g] = jnp.broadcast_to(best[g], (SUBLANES, LANES))

    pad_row = lax.broadcasted_iota(i32, (KEY_LANES - HEAD_DIM, 1), 0)
    for g in groups:
        q = jnp.concatenate([qt_ref[h * HEAD_DIM:(h + 1) * HEAD_DIM, :]
                             for h in range(g * N_GROUPS, (g + 1) * N_GROUPS)], axis=1)
        qf = q.astype(jnp.float32)
        bound = jnp.sqrt(jnp.sum(qf * qf, axis=0, keepdims=True) * kmax_ref[g][0:1, 0:1]) * BOUND_MARGIN
        pad = jnp.where(pad_row == 0, -bound, 0.0).astype(jnp.bfloat16)
        qg_ref[g] = jnp.concatenate([q, pad], axis=0)

    def value_rows(kb, g):
        ones = jnp.ones((SUBLANES, kb_rows), jnp.bfloat16)
        return jnp.concatenate([vt_ref[kb, g * HEAD_DIM:(g + 1) * HEAD_DIM, :], ones], axis=0)

    def pipelined(logits_stage, value_stage, ms):
        def step(kb, slot, ms, staged, prefetch):
            extra, run = staged
            nxt = logits_stage(kb + 1, 1 - slot, run) if prefetch else None
            return tuple(value_stage(kb, g, slot, ms[g], extra[g]) for g in groups), nxt

        def pair_body(i, carry):
            ms, staged = carry
            ms, staged = step(2 * i, 0, ms, staged, True)
            return step(2 * i + 1, 1, ms, staged, True)

        n_pairs = (nkb - 1) // 2
        ms, staged = lax.fori_loop(0, n_pairs, pair_body, (ms, logits_stage(0, 0, zero)))
        last = 2 * n_pairs

        def two_left():
            ms1, staged1 = step(last, 0, ms, staged, True)
            return step(last + 1, 1, ms1, staged1, False)[0]

        return lax.cond(nkb - last == 2, two_left, lambda: step(last, 0, ms, staged, False)[0])

    def logits(kb, g, bias):
        return _dot(kb_ref[g, kslice(kb), :], qg_ref[g]) + bias

    def bounded_logits_stage(kb, slot, run):
        bias, run = block_bias(kb, run)
        bias = jnp.concatenate([bias] * N_GROUPS, axis=1)
        for g in groups:
            p_ref[slot, g] = jnp.exp2(logits(kb, g, bias)).astype(jnp.bfloat16)
        return (izero,) * N_KV_HEADS, run

    def bounded_value_stage(kb, g, slot, m, unused):
        acc_ref[g] = acc_ref[g] + _dot(value_rows(kb, g), p_ref[slot, g])
        return m

    def online_logits_stage(kb, slot, run):
        bias, run = block_bias(kb, run)
        bias = jnp.concatenate([bias] * N_GROUPS, axis=1)
        m_blks = []
        for g in groups:
            lg = logits(kb, g, bias)
            lg_ref[slot, g] = lg
            m_blks.append(jnp.max(lg, axis=0, keepdims=True))
        return tuple(m_blks), run

    def online_value_stage(kb, g, slot, m, m_blk):
        m_new = jnp.maximum(m, m_blk)
        p = jnp.exp2(lg_ref[slot, g] - m_new).astype(jnp.bfloat16)
        acc_ref[g] = jnp.exp2(m - m_new) * acc_ref[g] + _dot(value_rows(kb, g), p)
        return m_new

    has_key = jnp.concatenate([n_adm] * N_GROUPS, axis=1) > 0
    acc_ref[...] = jnp.zeros_like(acc_ref)
    pipelined(bounded_logits_stage, bounded_value_stage, (izero,) * N_KV_HEADS)
    underflow = izero[:, 0:1]
    for g in groups:
        denom = acc_ref[g][HEAD_DIM:HEAD_DIM + 1, :]
        underflow = underflow + jnp.sum(jnp.where(has_key, jnp.where(denom < MIN_DENOMINATOR, 1, 0), 0),
                                        axis=1, keepdims=True)

    @pl.when(underflow[0, 0] > 0)
    def _():
        acc_ref[...] = jnp.zeros_like(acc_ref)
        for g in groups:
            qg_ref[g, HEAD_DIM:, :] = jnp.zeros((KEY_LANES - HEAD_DIM, gw), jnp.bfloat16)
        pipelined(online_logits_stage, online_value_stage,
                  (jnp.full((1, gw), NEG_BIAS, jnp.float32),) * N_KV_HEADS)

    for g in groups:
        acc = acc_ref[g]
        og = jnp.where(has_key, acc[:HEAD_DIM] / acc[HEAD_DIM:HEAD_DIM + 1], 0.0)
        for hh in range(N_GROUPS):
            h = g * N_GROUPS + hh
            ot_ref[h * HEAD_DIM:(h + 1) * HEAD_DIM, :] = og[:, hh * tq:(hh + 1) * tq]

    o_ref[...] = ot_ref[...].T.astype(o_ref.dtype)


def _attention(qt, qit, wit, kb, kib, vt, *, n_seq, lq, lk, tq, kb_rows, causal, first_key, topk):
    nq = lq // tq
    n_kblocks = lk // kb_rows
    tri = jnp.tril(jnp.ones((kb_rows, kb_rows), jnp.bfloat16))
    count_blocks = 2 if causal else 1
    kernel = functools.partial(_attn_kernel, tq=tq, kb_rows=kb_rows, n_kblocks=n_kblocks,
                               count_blocks=count_blocks, causal=causal, first_key=first_key, topk=topk)
    return pl.pallas_call(
        kernel,
        grid=(n_seq, nq),
        in_specs=[
            pl.BlockSpec((Q_COLS, tq), lambda b, j: (0, b * nq + j)),
            pl.BlockSpec((IQ_COLS, tq), lambda b, j: (0, b * nq + j)),
            pl.BlockSpec((IW_ROWS, tq), lambda b, j: (0, b * nq + j)),
            pl.BlockSpec((N_KV_HEADS, lk, KEY_LANES), lambda b, j: (0, b, 0)),
            pl.BlockSpec((lk, IDX_DIM), lambda b, j: (b, 0)),
            pl.BlockSpec((n_kblocks, KV_COLS, kb_rows), lambda b, j: (b, 0, 0)),
            pl.BlockSpec((kb_rows, kb_rows), lambda b, j: (0, 0)),
        ],
        out_specs=pl.BlockSpec((tq, Q_COLS), lambda b, j: (b * nq + j, 0)),
        out_shape=jax.ShapeDtypeStruct((n_seq * lq, Q_COLS), jnp.bfloat16),
        scratch_shapes=[
            pltpu.VMEM((lk + (count_blocks - 1) * kb_rows, tq), jnp.int32),
            pltpu.VMEM((Q_COLS, tq), jnp.float32),
            pltpu.VMEM((N_KV_HEADS, KEY_LANES, N_GROUPS * tq), jnp.bfloat16),
            pltpu.VMEM((N_KV_HEADS, HEAD_DIM + SUBLANES, N_GROUPS * tq), jnp.float32),
            pltpu.VMEM((2, N_KV_HEADS, kb_rows, N_GROUPS * tq), jnp.float32),
            pltpu.VMEM((2, N_KV_HEADS, kb_rows, N_GROUPS * tq), jnp.bfloat16),
            pltpu.VMEM((N_KV_HEADS, SUBLANES, LANES), jnp.float32),
        ],
        compiler_params=pltpu.CompilerParams(dimension_semantics=("arbitrary", "arbitrary"),
                                             vmem_limit_bytes=VMEM_LIMIT),
        name="dsa_attention_causal" if causal else "dsa_attention_full",
    )(qt, qit, wit, kb, kib, vt, tri)


def _mix_out_kernel(x_ref, o_ref, u_ref, halo_ref, cb_ref, g0_ref, g1_ref, cw_ref,
                    wg_ref, wao_ref, wco_ref, wo_ref, x1_ref, *, n_seg):
    x = x_ref[...]
    h = _rms(x, g0_ref[...]).astype(jnp.bfloat16)
    gates = _dot(h, wg_ref[...])
    y_a = _dot(o_ref[...], wao_ref[...])
    u = u_ref[...]
    halo = halo_ref[0]
    t = u.shape[0]
    seg = t // n_seg
    row = lax.broadcasted_iota(jnp.int32, (t, 1), 0)
    u1 = pltpu.roll(u, 1, 0)
    u2 = pltpu.roll(u, 2, 0)
    for s in range(n_seg):
        h6 = halo[s * SUBLANES + 6:s * SUBLANES + 7]
        h7 = halo[s * SUBLANES + 7:s * SUBLANES + 8]
        u1 = jnp.where(row == s * seg, h7, u1)
        u2 = jnp.where(row == s * seg, h6, jnp.where(row == s * seg + 1, h7, u2))
    conv = cw_ref[0:1] * u2 + cw_ref[1:2] * u1 + cw_ref[2:3] * u
    y_b = _dot((cb_ref[...] * conv).astype(jnp.bfloat16), wco_ref[...])
    m = _sigmoid(gates[:, :D_MODEL]) * y_a + _sigmoid(gates[:, D_MODEL:]) * y_b
    a = _dot(m.astype(jnp.bfloat16), wo_ref[...])
    x1_ref[...] = x + _rms(a, g1_ref[...])


def _mix_out(x, o, u, halo, cb, g0, g1, cw, wg, wao, wco, wo):
    n = x.shape[0]
    tm = ROW_TILE
    n_seg = halo.shape[1] // SUBLANES
    full = lambda shape: pl.BlockSpec(shape, lambda i: (0,) * len(shape))
    rows = lambda w: pl.BlockSpec((tm, w), lambda i: (i, 0))
    return pl.pallas_call(
        functools.partial(_mix_out_kernel, n_seg=n_seg),
        grid=(n // tm,),
        in_specs=[rows(D_MODEL), rows(Q_COLS), rows(CONV_DIM),
                  pl.BlockSpec((1, n_seg * SUBLANES, CONV_DIM), lambda i: (i, 0, 0)), rows(CONV_DIM),
                  full((1, D_MODEL)), full((1, D_MODEL)), full((CONV_WIDTH, CONV_DIM)),
                  full((D_MODEL, 2 * D_MODEL)), full((Q_COLS, D_MODEL)), full((CONV_DIM, D_MODEL)),
                  full((D_MODEL, D_MODEL))],
        out_specs=rows(D_MODEL),
        out_shape=jax.ShapeDtypeStruct((n, D_MODEL), jnp.float32),
        compiler_params=pltpu.CompilerParams(dimension_semantics=("arbitrary",), vmem_limit_bytes=VMEM_LIMIT),
        name="mix_out",
    )(x, o, u, halo, cb, g0, g1, cw, wg, wao, wco, wo)


def _mlp_kernel(x_ref, g2_ref, g3_ref, wup_ref, wdn_ref, x2_ref):
    x = x_ref[...]
    h = _rms(x, g2_ref[...]).astype(jnp.bfloat16)
    up = jnp.maximum(_dot(h, wup_ref[...]), 0.0)
    f = _dot((up * up).astype(jnp.bfloat16), wdn_ref[...])
    x2_ref[...] = x + _rms(f, g3_ref[...])


def _mlp(x, g2, g3, wup, wdn, drop_head_tiles_of=None):
    n = x.shape[0]
    tm = ROW_TILE
    full = lambda shape: pl.BlockSpec(shape, lambda i: (0,) * len(shape))
    rows = lambda w: pl.BlockSpec((tm, w), lambda i: (i, 0))
    if drop_head_tiles_of is None:
        out_rows, out_spec = n, rows(D_MODEL)
    else:
        tps = drop_head_tiles_of
        out_rows = n // tps * (tps - 1)
        out_spec = pl.BlockSpec((tm, D_MODEL), lambda i: (i // tps * (tps - 1) + jnp.maximum(i % tps - 1, 0), 0))
    return pl.pallas_call(
        _mlp_kernel,
        grid=(n // tm,),
        in_specs=[rows(D_MODEL), full((1, D_MODEL)), full((1, D_MODEL)), full((D_MODEL, D_FF)),
                  full((D_FF, D_MODEL))],
        out_specs=out_spec,
        out_shape=jax.ShapeDtypeStruct((out_rows, D_MODEL), jnp.float32),
        compiler_params=pltpu.CompilerParams(dimension_semantics=("arbitrary",), vmem_limit_bytes=VMEM_LIMIT),
        name="mlp",
    )(x, g2, g3, wup, wdn)


def _prepare_weights(w_in, conv_w, w_attn_out, w_conv_out, w_o, w_up, w_down):
    bf16 = jnp.bfloat16
    wt = jnp.concatenate([
        w_in[:, :, :END_Q] * (HEAD_DIM ** -0.5),
        w_in[:, :, END_Q:END_V],
        w_in[:, :, END_V:END_IQ] * (IDX_DIM ** -0.5),
        w_in[:, :, END_IQ:END_IK],
        w_in[:, :, END_IK:END_IW] * (IDX_HEADS ** -0.5),
        jnp.zeros(w_in.shape[:2] + (IW_ROWS - IDX_HEADS,), w_in.dtype),
    ], axis=2).astype(bf16).transpose(0, 2, 1)
    return dict(
        wt=wt,
        wc=w_in[:, :, END_IW:END_CX].astype(bf16),
        wg=w_in[:, :, END_CX:].astype(bf16),
        cw=conv_w,
        wao=w_attn_out.astype(bf16), wco=w_conv_out.astype(bf16), wo=w_o.astype(bf16),
        wup=w_up.astype(bf16), wdn=w_down.astype(bf16))


def _rope_tables(pos):
    inv = jnp.exp(-math.log(ROPE_THETA) * jnp.arange(ROT_HALF, dtype=jnp.float32) * (2.0 / ROT_DIM))
    ang = inv[:, None] * pos.astype(jnp.float32)[None, :]
    return jnp.cos(ang), jnp.sin(ang)


def _halo(u, n_seq, past):
    n = u.shape[0]
    per_seq = n // n_seq
    first = jnp.concatenate([jnp.zeros((n_seq, SUBLANES - (CONV_WIDTH - 1), CONV_DIM), u.dtype), past], axis=1)
    if per_seq >= ROW_TILE:
        tiles = per_seq // ROW_TILE
        tails = u.reshape(n_seq, tiles, ROW_TILE, CONV_DIM)[:, :tiles - 1, ROW_TILE - SUBLANES:]
        return jnp.concatenate([first[:, None], tails], axis=1).reshape(n_seq * tiles, SUBLANES, CONV_DIM)
    n_seg = ROW_TILE // per_seq
    return first.reshape(n_seq // n_seg, n_seg * SUBLANES, CONV_DIM)


def _with_ones_column(k):
    tail = jnp.zeros(k.shape[:-1] + (KEY_LANES - HEAD_DIM,), k.dtype).at[..., 0].set(1)
    return jnp.concatenate([k, tail], axis=-1)


def _pad_lanes(a, n_seq, per_seq, width):
    a = a.reshape(a.shape[0], n_seq, per_seq)
    a = jnp.pad(a, ((0, 0), (0, 0), (0, width - per_seq)))
    return a.reshape(a.shape[0], n_seq * width)


def kernel(x_prompt, x_sample, cache_k, cache_v, cache_kidx, state_conv, meta_tokens, norm_gains, w_in, conv_w,
           w_attn_out, w_conv_out, w_o, w_up, w_down):
    f32, bf16 = jnp.float32, jnp.bfloat16
    depth = w_in.shape[0]
    bp, seq, _ = x_prompt.shape
    bs, s_len, _ = x_sample.shape
    past_len = cache_k.shape[2]
    assert seq % Q_TILE == 0 and Q_TILE % CHUNK == 0 and ROW_TILE == Q_TILE and N_META <= Q_TILE
    assert (bs * s_len) % ROW_TILE == 0 and ROW_TILE % s_len == 0
    assert CONV_WIDTH - 1 <= s_len <= LANES and (past_len + s_len) % SUBLANES == 0

    pad = Q_TILE - N_META
    rp = Q_TILE + seq
    xp = jnp.concatenate([
        jnp.zeros((bp, pad, D_MODEL), f32),
        jnp.broadcast_to(meta_tokens[None].astype(f32), (bp, N_META, D_MODEL)),
        x_prompt.astype(f32)], axis=1).reshape(bp * rp, D_MODEL)
    cos_p, sin_p = _rope_tables(jnp.tile(jnp.arange(rp, dtype=jnp.int32) - pad, bp))
    topk_p = min(TOPK_MAX, seq // 4)

    xs = x_sample.astype(f32).reshape(bs * s_len, D_MODEL)
    ls = past_len + s_len
    cos_s, sin_s = _rope_tables(jnp.tile(past_len + jnp.arange(s_len, dtype=jnp.int32), bs))
    topk_s = min(TOPK_MAX, ls // 4)
    conv_zero = jnp.zeros((bp, CONV_WIDTH - 1, CONV_DIM), f32)

    weights = _prepare_weights(w_in, conv_w, w_attn_out, w_conv_out, w_o, w_up, w_down)
    outs = [[] for _ in range(8)]
    for l in range(depth):
        w = {name: value[l] for name, value in weights.items()}
        g = [norm_gains[l, i][None, :] for i in range(4)]

        qt, qit, wit, vt, kb, kib, k32, v32, ki32, u, cb = _in_proj(xp, g[0], w["wt"], w["wc"], cos_p, sin_p)
        o = _attention(qt, qit, wit, kb, kib, vt, n_seq=bp, lq=rp, lk=rp, tq=Q_TILE, kb_rows=ROW_TILE,
                       causal=True, first_key=pad, topk=topk_p)
        x1 = _mix_out(xp, o, u, _halo(u, bp, conv_zero), cb, g[0], g[1], w["cw"], w["wg"], w["wao"], w["wco"],
                      w["wo"])
        xp = _mlp(x1, g[2], g[3], w["wup"], w["wdn"], drop_head_tiles_of=rp // ROW_TILE if l == depth - 1 else None)
        outs[0].append(k32.reshape(bp, rp, N_KV_HEADS, HEAD_DIM)[:, pad:])
        outs[1].append(v32.reshape(bp, rp, N_KV_HEADS, HEAD_DIM)[:, pad:])
        outs[2].append(ki32.reshape(bp, rp, IDX_DIM)[:, pad:])
        outs[3].append(u.reshape(bp, rp, CONV_DIM)[:, rp - (CONV_WIDTH - 1):])

        qt, qit, wit, vt, kb, kib, k32, v32, ki32, u, cb = _in_proj(xs, g[0], w["wt"], w["wc"], cos_s, sin_s)
        k_new = k32.reshape(bs, s_len, N_KV_HEADS, HEAD_DIM)
        v_new = v32.reshape(bs, s_len, N_KV_HEADS, HEAD_DIM)
        ki_new = ki32.reshape(bs, s_len, IDX_DIM)
        k_all = jnp.concatenate([cache_k[l], k_new], axis=1).astype(bf16)
        v_all = jnp.concatenate([cache_v[l], v_new], axis=1).astype(bf16)
        ki_all = jnp.concatenate([cache_kidx[l], ki_new], axis=1).astype(bf16)
        o = _attention(
            _pad_lanes(qt, bs, s_len, LANES), _pad_lanes(qit, bs, s_len, LANES), _pad_lanes(wit, bs, s_len, LANES),
            _with_ones_column(k_all.transpose(2, 0, 1, 3).reshape(N_KV_HEADS, bs * ls, HEAD_DIM)),
            ki_all.reshape(bs * ls, IDX_DIM),
            v_all.reshape(bs, ls, KV_COLS).transpose(0, 2, 1),
            n_seq=bs, lq=LANES, lk=ls, tq=LANES, kb_rows=ls, causal=False, first_key=0, topk=topk_s)
        o = o.reshape(bs, LANES, Q_COLS)[:, :s_len].reshape(bs * s_len, Q_COLS)
        x1 = _mix_out(xs, o, u, _halo(u, bs, state_conv[l].astype(f32)), cb, g[0], g[1], w["cw"], w["wg"],
                      w["wao"], w["wco"], w["wo"])
        xs = _mlp(x1, g[2], g[3], w["wup"], w["wdn"])
        outs[4].append(k_new)
        outs[5].append(v_new)
        outs[6].append(ki_new)
        outs[7].append(u.reshape(bs, s_len, CONV_DIM)[:, s_len - (CONV_WIDTH - 1):])

    y_prompt = xp.reshape(bp, seq, D_MODEL)
    y_sample = xs.reshape(bs, s_len, D_MODEL)
    return (y_prompt, y_sample) + tuple(jnp.stack(o) for o in outs)
```

```python
import functools
import math

import jax
import jax.numpy as jnp
from jax import lax
from jax.experimental import pallas as pl
from jax.experimental.pallas import tpu as pltpu

D_MODEL = 1024
CHUNK = 64
CHUNK_SHIFT = CHUNK.bit_length() - 1
N_META = 16
HEAD_DIM = 64
N_HEADS = 8
N_KV_HEADS = 2
N_GROUPS = N_HEADS // N_KV_HEADS
IDX_HEADS = 4
IDX_DIM = 64
ROT_DIM = HEAD_DIM // 4
ROT_HALF = ROT_DIM // 2
ROPE_THETA = 500000.0
CONV_DIM = D_MODEL // 2
CONV_WIDTH = 3
D_FF = 4 * D_MODEL
TOPK_MAX = 256
EPS = 1e-6

Q_COLS = N_HEADS * HEAD_DIM
KV_COLS = N_KV_HEADS * HEAD_DIM
IQ_COLS = IDX_HEADS * IDX_DIM
END_Q = Q_COLS
END_K = END_Q + KV_COLS
END_V = END_K + KV_COLS
END_IQ = END_V + IQ_COLS
END_IK = END_IQ + IDX_DIM
END_IW = END_IK + IDX_HEADS
END_CB = END_IW + CONV_DIM
END_CC = END_CB + CONV_DIM
END_CX = END_CC + CONV_DIM
END_GA = END_CX + D_MODEL
D_IN = END_GA + D_MODEL

SUBLANES = 8
LANES = 128
IW_ROWS = SUBLANES
ZT_Q = 0
ZT_K = ZT_Q + Q_COLS
ZT_V = ZT_K + KV_COLS
ZT_IQ = ZT_V + KV_COLS
ZT_IK = ZT_IQ + IQ_COLS
ZT_IW = ZT_IK + IDX_DIM
ZT_ROWS = ZT_IW + IW_ROWS

ROW_TILE = 256
Q_TILE = 256
NEG_BIAS = -1e30
LOG2_E = math.log2(math.e)
KEY_LANES = 128
BOUND_MARGIN = 1.01
MIN_DENOMINATOR = 2.0 ** -60
INT_MIN = -(2 ** 31)
NEG_INF_KEY = INT_MIN + 0x7FFFFF
MIN_FINITE_KEY = NEG_INF_KEY + 1
POS_INF_KEY = 0x7F800000
INT_MAX = 2 ** 31 - 1
ZERO_KEY = 0
TAKE_ALL = 2 ** 30
INTERP_STEPS = 20
STALL_STEPS = 2
SUM_CHAINS = 4
SEARCH_CAP = INTERP_STEPS + 34
VMEM_LIMIT = 56 * 1024 * 1024


def _rms(x, g):
    return x * lax.rsqrt(jnp.mean(x * x, axis=-1, keepdims=True) + EPS) * g


def _sigmoid(x):
    return 1.0 / (1.0 + jnp.exp(-x))


def _dot(a, b):
    return jnp.dot(a, b, preferred_element_type=jnp.float32)


def _dot_nt(a, b):
    return lax.dot_general(a, b, (((1,), (1,)), ((), ())), preferred_element_type=jnp.float32)


def _rope_rows(zt, n_heads, cos, sin):
    pieces = []
    for h in range(n_heads):
        o = h * HEAD_DIM
        x1 = zt[o:o + ROT_HALF]
        x2 = zt[o + ROT_HALF:o + ROT_DIM]
        pieces.append(x1 * cos - x2 * sin)
        pieces.append(x2 * cos + x1 * sin)
        pieces.append(zt[o + ROT_DIM:o + HEAD_DIM])
    return jnp.concatenate(pieces, axis=0)


def _in_proj_kernel(x_ref, g_ref, wt_ref, wc_ref, cos_ref, sin_ref,
                    qt_ref, qit_ref, wit_ref, vt_ref, kb_ref, kib_ref,
                    kt_ref, vt32_ref, kit_ref, u_ref, cb_ref):
    h = _rms(x_ref[...], g_ref[...]).astype(jnp.bfloat16)
    zt = _dot_nt(wt_ref[...], h)
    cos = cos_ref[...]
    sin = sin_ref[...]
    qt_ref[...] = (_rope_rows(zt[ZT_Q:ZT_K], N_HEADS, cos, sin) * LOG2_E).astype(jnp.bfloat16)
    kt = _rope_rows(zt[ZT_K:ZT_V], N_KV_HEADS, cos, sin)
    kt_ref[...] = kt
    k = kt.T
    tail = jnp.where(lax.broadcasted_iota(jnp.int32, (k.shape[0], KEY_LANES - HEAD_DIM), 1) == 0, 1.0, 0.0)
    for g in range(N_KV_HEADS):
        kb_ref[g] = jnp.concatenate([k[:, g * HEAD_DIM:(g + 1) * HEAD_DIM], tail], axis=1).astype(jnp.bfloat16)
    vt = zt[ZT_V:ZT_IQ]
    vt_ref[0] = vt.astype(jnp.bfloat16)
    vt32_ref[...] = vt
    qit_ref[...] = _rope_rows(zt[ZT_IQ:ZT_IK], IDX_HEADS, cos, sin).astype(jnp.bfloat16)
    kit = _rope_rows(zt[ZT_IK:ZT_IW], 1, cos, sin)
    kit_ref[...] = kit
    kib_ref[...] = kit.T.astype(jnp.bfloat16)
    wit_ref[...] = zt[ZT_IW:ZT_ROWS]
    zc = _dot(h, wc_ref[...])
    cb_ref[...] = zc[:, :CONV_DIM]
    u_ref[...] = zc[:, CONV_DIM:2 * CONV_DIM] * zc[:, 2 * CONV_DIM:]


def _in_proj(x, g0, wt, wc, cos_t, sin_t):
    n = x.shape[0]
    tm = ROW_TILE
    nb = n // tm
    f32, bf16 = jnp.float32, jnp.bfloat16
    full = lambda shape: pl.BlockSpec(shape, lambda i: (0,) * len(shape))
    rows = lambda w: pl.BlockSpec((tm, w), lambda i: (i, 0))
    cols = lambda r: pl.BlockSpec((r, tm), lambda i: (0, i))
    out_shape = (
        jax.ShapeDtypeStruct((Q_COLS, n), bf16),
        jax.ShapeDtypeStruct((IQ_COLS, n), bf16),
        jax.ShapeDtypeStruct((IW_ROWS, n), f32),
        jax.ShapeDtypeStruct((nb, KV_COLS, tm), bf16),
        jax.ShapeDtypeStruct((N_KV_HEADS, n, KEY_LANES), bf16),
        jax.ShapeDtypeStruct((n, IDX_DIM), bf16),
        jax.ShapeDtypeStruct((KV_COLS, n), f32),
        jax.ShapeDtypeStruct((KV_COLS, n), f32),
        jax.ShapeDtypeStruct((IDX_DIM, n), f32),
        jax.ShapeDtypeStruct((n, CONV_DIM), f32),
        jax.ShapeDtypeStruct((n, CONV_DIM), f32),
    )
    out_specs = (
        cols(Q_COLS), cols(IQ_COLS), cols(IW_ROWS),
        pl.BlockSpec((1, KV_COLS, tm), lambda i: (i, 0, 0)),
        pl.BlockSpec((N_KV_HEADS, tm, KEY_LANES), lambda i: (0, i, 0)),
        rows(IDX_DIM), cols(KV_COLS), cols(KV_COLS), cols(IDX_DIM), rows(CONV_DIM), rows(CONV_DIM),
    )
    return pl.pallas_call(
        _in_proj_kernel,
        grid=(nb,),
        in_specs=[rows(D_MODEL), full((1, D_MODEL)), full((ZT_ROWS, D_MODEL)), full((D_MODEL, 3 * CONV_DIM)),
                  cols(ROT_HALF), cols(ROT_HALF)],
        out_specs=out_specs,
        out_shape=out_shape,
        compiler_params=pltpu.CompilerParams(dimension_semantics=("arbitrary",), vmem_limit_bytes=VMEM_LIMIT),
        name="in_proj",
    )(x, g0, wt, wc, cos_t, sin_t)


def _attn_kernel(qt_ref, qit_ref, wit_ref, kb_ref, kib_ref, vt_ref, tri_ref, o_ref,
                 skey_ref, ot_ref, qg_ref, acc_ref, lg_ref, p_ref, kmax_ref, *, tq, kb_rows, n_kblocks, count_blocks,
                 causal, first_key, topk):
    j = pl.program_id(1)
    nkb = (j + 1) if causal else n_kblocks
    i32 = jnp.int32

    q_row = j * tq + lax.broadcasted_iota(i32, (1, tq), 1)
    if causal:
        lim = jnp.where(q_row >= first_key, ((q_row >> CHUNK_SHIFT) << CHUNK_SHIFT) + CHUNK, 0)
    else:
        lim = jnp.full((1, tq), n_kblocks * kb_rows, i32)

    def key_rows(kb):
        return kb * kb_rows + lax.broadcasted_iota(i32, (kb_rows, 1), 0)

    def kslice(kb):
        return pl.ds(pl.multiple_of(kb * kb_rows, SUBLANES), kb_rows)

    def to_key(f):
        bits = lax.bitcast_convert_type(f, i32)
        return bits ^ ((bits >> 31) & 0x7FFFFFFF)

    def to_f32(k):
        return lax.bitcast_convert_type(k ^ ((k >> 31) & 0x7FFFFFFF), jnp.float32)

    def score_block(kb, edge):
        ki = kib_ref[kslice(kb), :]
        acc = jnp.zeros((kb_rows, tq), jnp.float32)
        for h in range(IDX_HEADS):
            s = _dot(ki, qit_ref[h * IDX_DIM:(h + 1) * IDX_DIM, :])
            acc = acc + wit_ref[h:h + 1, :] * jnp.maximum(s, 0.0)
        acc = jnp.where(acc == 0.0, 0.0, acc)
        if edge:
            rows = key_rows(kb)
            adm = jnp.logical_and(rows >= first_key, rows < lim)
            sk = to_key(jnp.where(adm, acc, -jnp.inf))
            masked = jnp.where(adm, acc, 0.0)
            scale = (jnp.sum(masked * masked, axis=0, keepdims=True),
                     jnp.sum(jnp.where(adm, 1.0, 0.0), axis=0, keepdims=True))
        else:
            sk = to_key(acc)
            scale = None
        skey_ref[kslice(kb), :] = sk
        return jnp.sum(jnp.where(sk > ZERO_KEY, 1, 0), axis=0, keepdims=True), scale

    zero = jnp.zeros((1, tq), jnp.float32)
    izero = jnp.zeros((1, tq), i32)
    c_pos, (s2, s0) = score_block(0, True)
    n_inner_pairs = (nkb - 2) // 2
    c_pos = lax.fori_loop(
        0, n_inner_pairs,
        lambda p, c: c + score_block(2 * p + 1, False)[0] + score_block(2 * p + 2, False)[0], c_pos)
    n_left = nkb - 1 - 2 * jnp.maximum(n_inner_pairs, 0)

    def last_block(c):
        c_l, (s2_l, s0_l) = score_block(nkb - 1, True)
        return c + c_l, s2 + s2_l, s0 + s0_l

    c_pos, s2, s0 = lax.cond(
        n_left == 2, lambda: last_block(c_pos + score_block(nkb - 2, False)[0]),
        lambda: lax.cond(n_left == 1, lambda: last_block(c_pos), lambda: (c_pos, s2, s0)))

    cslice_rows = count_blocks * kb_rows
    n_steps = (nkb + count_blocks - 1) // count_blocks
    if count_blocks > 1:
        skey_ref[kslice(nkb), :] = jnp.full((kb_rows, tq), INT_MIN, i32)

    assert cslice_rows % (SUM_CHAINS * SUBLANES) == 0

    def count_ge(cand):
        def body(c, acc):
            sk = skey_ref[pl.ds(pl.multiple_of(c * cslice_rows, SUBLANES), cslice_rows), :]
            ind = jnp.where(sk >= cand, 1, 0)
            return acc + jnp.sum(ind.reshape(SUM_CHAINS, -1, SUBLANES, tq), axis=1)
        acc = lax.fori_loop(0, n_steps, body, jnp.zeros((SUM_CHAINS, SUBLANES, tq), i32))
        return jnp.sum(acc.reshape(SUM_CHAINS * SUBLANES, tq), axis=0, keepdims=True)

    n_adm = jnp.maximum(lim - first_key, 0)
    n_f = jnp.maximum(n_adm, 1).astype(jnp.float32)
    sigma = jnp.sqrt(s2 / jnp.maximum(s0, 1.0))
    tail = jnp.clip(topk / n_f, 1e-6, 1.0 - 1e-6)
    tq_ = jnp.sqrt(-2.0 * jnp.log(jnp.minimum(tail, 1.0 - tail)))
    zq = tq_ - (2.515517 + 0.802853 * tq_ + 0.010328 * tq_ * tq_) / (
        1.0 + 1.432788 * tq_ + 0.189269 * tq_ * tq_ + 0.001308 * tq_ * tq_ * tq_)
    zq = jnp.where(tail < 0.5, zq, -zq)
    step_scale = 1.5 * sigma / jnp.maximum(jnp.abs(zq), 0.5)
    log_k = math.log(topk + 0.5)

    def search_cond(st):
        return jnp.logical_and(st[0] < SEARCH_CAP, st[1] > 0)

    def search_body(st):
        it, _, lo, hi, c_lo, c_hi, lo_real, hi_real, w_lo, w_hi, last, stall, done = st
        f_lo, f_hi = to_f32(lo), to_f32(hi)
        g_lo = jnp.log(c_lo.astype(jnp.float32)) - log_k
        g_hi = log_k - jnp.log(jnp.maximum(c_hi.astype(jnp.float32), 0.5))
        t_in = f_lo + (f_hi - f_lo) * (w_lo * g_lo / (w_lo * g_lo + w_hi * g_hi))
        t_up = f_lo + step_scale * g_lo
        t_dn = f_hi - step_scale * g_hi
        t = jnp.where(lo_real > 0, jnp.where(hi_real > 0, t_in, t_up), t_dn)
        cand = jnp.minimum(jnp.maximum(to_key(t), lo + 1), hi - 1)
        mid = (lo >> 1) + (hi >> 1) + (lo & hi & 1)
        cand = jnp.where(stall >= STALL_STEPS, mid, jnp.where(it >= INTERP_STEPS, mid, cand))
        cand = jnp.where(it == 0, jnp.where(below, ZERO_KEY, cand), cand)
        cnt = count_ge(cand)
        up = jnp.where(done > 0, 0, jnp.where(cnt >= topk, 1, 0))
        dn = jnp.where(done > 0, 0, jnp.where(cnt >= topk, 0, 1))
        both = lo_real * hi_real
        stall = stall + both * jnp.where(cnt == c_lo, 1, jnp.where(cnt == c_hi, 1, 0))
        lo = jnp.where(up > 0, cand, lo)
        c_lo = jnp.where(up > 0, cnt, c_lo)
        lo_real = jnp.maximum(lo_real, up)
        hi = jnp.where(dn > 0, cand, hi)
        c_hi = jnp.where(dn > 0, cnt, c_hi)
        hi_real = jnp.maximum(hi_real, dn)
        w_hi = jnp.where(up > 0, jnp.where(last > 0, 0.5 * w_hi, 1.0), jnp.where(dn > 0, 1.0, w_hi))
        w_lo = jnp.where(dn > 0, jnp.where(last < 0, 0.5 * w_lo, 1.0), jnp.where(up > 0, 1.0, w_lo))
        last = up - dn + (1 - up - dn) * last
        done = jnp.where(c_lo == topk, 1, jnp.where(hi == lo + 1, 1, done))
        return (it + 1, jnp.sum(1 - done), lo, hi, c_lo, c_hi, lo_real, hi_real, w_lo, w_hi, last, stall, done)

    small = n_adm <= topk
    above = c_pos >= topk
    below = jnp.where(small, 0, jnp.where(above, 0, 1)) > 0
    lo0 = jnp.where(small, NEG_INF_KEY, jnp.where(above, ZERO_KEY + 1, MIN_FINITE_KEY))
    hi0 = jnp.where(small, NEG_INF_KEY + 1, jnp.where(above, POS_INF_KEY, ZERO_KEY + 1))
    c_lo0 = jnp.where(small, topk, jnp.where(above, c_pos, n_adm))
    c_hi0 = jnp.where(small, 0, jnp.where(above, 0, c_pos))
    lo_real0 = jnp.where(small, 0, jnp.where(above, 1, 0))
    hi_real0 = jnp.where(small, 0, jnp.where(above, 0, 1))
    done0 = jnp.where(c_lo0 == topk, 1, jnp.where(hi0 == lo0 + 1, 1, 0))
    one = jnp.ones((1, tq), jnp.float32)
    st = lax.while_loop(search_cond, search_body,
                        (i32(0), jnp.sum(1 - done0), lo0, hi0, c_lo0, c_hi0, lo_real0, hi_real0, one, one,
                         izero, izero, done0))
    thr, hi, c_hi = st[2], st[3], st[5]
    need = jnp.where(small, 0, jnp.where(hi == thr + 1, topk - c_hi, TAKE_ALL)).astype(jnp.float32)

    def block_bias(kb, run):
        sk = skey_ref[kslice(kb), :]
        tie = jnp.where(sk == thr, 1.0, 0.0)
        rank = _dot(tri_ref[...], tie.astype(jnp.bfloat16)) + run
        tie_bias = jnp.where(rank <= need, 0.0, NEG_BIAS)
        bias = jnp.where(sk > thr, 0.0, jnp.where(sk == thr, tie_bias, NEG_BIAS))
        return bias, rank[kb_rows - 1:kb_rows, :]

    gw = N_GROUPS * tq
    groups = range(N_KV_HEADS)
    lane = lax.broadcasted_iota(i32, (1, LANES), 1)

    @pl.when(j == 0)
    def _():
        def norm_body(kb, best):
            out = []
            for g in groups:
                k = jnp.where(lane < HEAD_DIM, kb_ref[g, kslice(kb), :].astype(jnp.float32), 0.0)
                n2 = jnp.max(jnp.sum(k * k, axis=1, keepdims=True), axis=0, keepdims=True)
                out.append(jnp.maximum(best[g], n2))
            return tuple(out)
        best = lax.fori_loop(0, n_kblocks, norm_body, (jnp.zeros((1, 1), jnp.float32),) * N_KV_HEADS)
        for g in groups:
            kmax_ref[g] = jnp.broadcast_to(best[g], (SUBLANES, LANES))

    pad_row = lax.broadcasted_iota(i32, (KEY_LANES - HEAD_DIM, 1), 0)
    for g in groups:
        q = jnp.concatenate([qt_ref[h * HEAD_DIM:(h + 1) * HEAD_DIM, :]
                             for h in range(g * N_GROUPS, (g + 1) * N_GROUPS)], axis=1)
        qf = q.astype(jnp.float32)
        bound = jnp.sqrt(jnp.sum(qf * qf, axis=0, keepdims=True) * kmax_ref[g][0:1, 0:1]) * BOUND_MARGIN
        pad = jnp.where(pad_row == 0, -bound, 0.0).astype(jnp.bfloat16)
        qg_ref[g] = jnp.concatenate([q, pad], axis=0)

    def value_rows(kb, g):
        ones = jnp.ones((SUBLANES, kb_rows), jnp.bfloat16)
        return jnp.concatenate([vt_ref[kb, g * HEAD_DIM:(g + 1) * HEAD_DIM, :], ones], axis=0)

    def pipelined(logits_stage, value_stage, ms):
        def step(kb, slot, ms, staged, prefetch):
            extra, run = staged
            nxt = logits_stage(kb + 1, 1 - slot, run) if prefetch else None
            return tuple(value_stage(kb, g, slot, ms[g], extra[g]) for g in groups), nxt

        def pair_body(i, carry):
            ms, staged = carry
            ms, staged = step(2 * i, 0, ms, staged, True)
            return step(2 * i + 1, 1, ms, staged, True)

        n_pairs = (nkb - 1) // 2
        ms, staged = lax.fori_loop(0, n_pairs, pair_body, (ms, logits_stage(0, 0, zero)))
        last = 2 * n_pairs

        def two_left():
            ms1, staged1 = step(last, 0, ms, staged, True)
            return step(last + 1, 1, ms1, staged1, False)[0]

        return lax.cond(nkb - last == 2, two_left, lambda: step(last, 0, ms, staged, False)[0])

    def logits(kb, g, bias):
        return _dot(kb_ref[g, kslice(kb), :], qg_ref[g]) + bias

    def bounded_logits_stage(kb, slot, run):
        bias, run = block_bias(kb, run)
        bias = jnp.concatenate([bias] * N_GROUPS, axis=1)
        for g in groups:
            p_ref[slot, g] = jnp.exp2(logits(kb, g, bias)).astype(jnp.bfloat16)
        return (izero,) * N_KV_HEADS, run

    def bounded_value_stage(kb, g, slot, m, unused):
        acc_ref[g] = acc_ref[g] + _dot(value_rows(kb, g), p_ref[slot, g])
        return m

    def online_logits_stage(kb, slot, run):
        bias, run = block_bias(kb, run)
        bias = jnp.concatenate([bias] * N_GROUPS, axis=1)
        m_blks = []
        for g in groups:
            lg = logits(kb, g, bias)
            lg_ref[slot, g] = lg
            m_blks.append(jnp.max(lg, axis=0, keepdims=True))
        return tuple(m_blks), run

    def online_value_stage(kb, g, slot, m, m_blk):
        m_new = jnp.maximum(m, m_blk)
        p = jnp.exp2(lg_ref[slot, g] - m_new).astype(jnp.bfloat16)
        acc_ref[g] = jnp.exp2(m - m_new) * acc_ref[g] + _dot(value_rows(kb, g), p)
        return m_new

    has_key = jnp.concatenate([n_adm] * N_GROUPS, axis=1) > 0
    acc_ref[...] = jnp.zeros_like(acc_ref)
    pipelined(bounded_logits_stage, bounded_value_stage, (izero,) * N_KV_HEADS)
    underflow = izero[:, 0:1]
    for g in groups:
        denom = acc_ref[g][HEAD_DIM:HEAD_DIM + 1, :]
        underflow = underflow + jnp.sum(jnp.where(has_key, jnp.where(denom < MIN_DENOMINATOR, 1, 0), 0),
                                        axis=1, keepdims=True)

    @pl.when(underflow[0, 0] > 0)
    def _():
        acc_ref[...] = jnp.zeros_like(acc_ref)
        for g in groups:
            qg_ref[g, HEAD_DIM:, :] = jnp.zeros((KEY_LANES - HEAD_DIM, gw), jnp.bfloat16)
        pipelined(online_logits_stage, online_value_stage,
                  (jnp.full((1, gw), NEG_BIAS, jnp.float32),) * N_KV_HEADS)

    for g in groups:
        acc = acc_ref[g]
        og = jnp.where(has_key, acc[:HEAD_DIM] / acc[HEAD_DIM:HEAD_DIM + 1], 0.0)
        for hh in range(N_GROUPS):
            h = g * N_GROUPS + hh
            ot_ref[h * HEAD_DIM:(h + 1) * HEAD_DIM, :] = og[:, hh * tq:(hh + 1) * tq]

    o_ref[...] = ot_ref[...].T.astype(o_ref.dtype)


def _attention(qt, qit, wit, kb, kib, vt, *, n_seq, lq, lk, tq, kb_rows, causal, first_key, topk):
    nq = lq // tq
    n_kblocks = lk // kb_rows
    tri = jnp.tril(jnp.ones((kb_rows, kb_rows), jnp.bfloat16))
    count_blocks = 2 if causal else 1
    kernel = functools.partial(_attn_kernel, tq=tq, kb_rows=kb_rows, n_kblocks=n_kblocks,
                               count_blocks=count_blocks, causal=causal, first_key=first_key, topk=topk)
    return pl.pallas_call(
        kernel,
        grid=(n_seq, nq),
        in_specs=[
            pl.BlockSpec((Q_COLS, tq), lambda b, j: (0, b * nq + j)),
            pl.BlockSpec((IQ_COLS, tq), lambda b, j: (0, b * nq + j)),
            pl.BlockSpec((IW_ROWS, tq), lambda b, j: (0, b * nq + j)),
            pl.BlockSpec((N_KV_HEADS, lk, KEY_LANES), lambda b, j: (0, b, 0)),
            pl.BlockSpec((lk, IDX_DIM), lambda b, j: (b, 0)),
            pl.BlockSpec((n_kblocks, KV_COLS, kb_rows), lambda b, j: (b, 0, 0)),
            pl.BlockSpec((kb_rows, kb_rows), lambda b, j: (0, 0)),
        ],
        out_specs=pl.BlockSpec((tq, Q_COLS), lambda b, j: (b * nq + j, 0)),
        out_shape=jax.ShapeDtypeStruct((n_seq * lq, Q_COLS), jnp.bfloat16),
        scratch_shapes=[
            pltpu.VMEM((lk + (count_blocks - 1) * kb_rows, tq), jnp.int32),
            pltpu.VMEM((Q_COLS, tq), jnp.float32),
            pltpu.VMEM((N_KV_HEADS, KEY_LANES, N_GROUPS * tq), jnp.bfloat16),
            pltpu.VMEM((N_KV_HEADS, HEAD_DIM + SUBLANES, N_GROUPS * tq), jnp.float32),
            pltpu.VMEM((2, N_KV_HEADS, kb_rows, N_GROUPS * tq), jnp.float32),
            pltpu.VMEM((2, N_KV_HEADS, kb_rows, N_GROUPS * tq), jnp.bfloat16),
            pltpu.VMEM((N_KV_HEADS, SUBLANES, LANES), jnp.float32),
        ],
        compiler_params=pltpu.CompilerParams(dimension_semantics=("arbitrary", "arbitrary"),
                                             vmem_limit_bytes=VMEM_LIMIT),
        name="dsa_attention_causal" if causal else "dsa_attention_full",
    )(qt, qit, wit, kb, kib, vt, tri)


def _mix_out_kernel(x_ref, o_ref, u_ref, halo_ref, cb_ref, g0_ref, g1_ref, cw_ref,
                    wg_ref, wao_ref, wco_ref, wo_ref, x1_ref, *, n_seg):
    x = x_ref[...]
    h = _rms(x, g0_ref[...]).astype(jnp.bfloat16)
    gates = _dot(h, wg_ref[...])
    y_a = _dot(o_ref[...], wao_ref[...])
    u = u_ref[...]
    halo = halo_ref[0]
    t = u.shape[0]
    seg = t // n_seg
    row = lax.broadcasted_iota(jnp.int32, (t, 1), 0)
    u1 = pltpu.roll(u, 1, 0)
    u2 = pltpu.roll(u, 2, 0)
    for s in range(n_seg):
        h6 = halo[s * SUBLANES + 6:s * SUBLANES + 7]
        h7 = halo[s * SUBLANES + 7:s * SUBLANES + 8]
        u1 = jnp.where(row == s * seg, h7, u1)
        u2 = jnp.where(row == s * seg, h6, jnp.where(row == s * seg + 1, h7, u2))
    conv = cw_ref[0:1] * u2 + cw_ref[1:2] * u1 + cw_ref[2:3] * u
    y_b = _dot((cb_ref[...] * conv).astype(jnp.bfloat16), wco_ref[...])
    m = _sigmoid(gates[:, :D_MODEL]) * y_a + _sigmoid(gates[:, D_MODEL:]) * y_b
    a = _dot(m.astype(jnp.bfloat16), wo_ref[...])
    x1_ref[...] = x + _rms(a, g1_ref[...])


def _mix_out(x, o, u, halo, cb, g0, g1, cw, wg, wao, wco, wo):
    n = x.shape[0]
    tm = ROW_TILE
    n_seg = halo.shape[1] // SUBLANES
    full = lambda shape: pl.BlockSpec(shape, lambda i: (0,) * len(shape))
    rows = lambda w: pl.BlockSpec((tm, w), lambda i: (i, 0))
    return pl.pallas_call(
        functools.partial(_mix_out_kernel, n_seg=n_seg),
        grid=(n // tm,),
        in_specs=[rows(D_MODEL), rows(Q_COLS), rows(CONV_DIM),
                  pl.BlockSpec((1, n_seg * SUBLANES, CONV_DIM), lambda i: (i, 0, 0)), rows(CONV_DIM),
                  full((1, D_MODEL)), full((1, D_MODEL)), full((CONV_WIDTH, CONV_DIM)),
                  full((D_MODEL, 2 * D_MODEL)), full((Q_COLS, D_MODEL)), full((CONV_DIM, D_MODEL)),
                  full((D_MODEL, D_MODEL))],
        out_specs=rows(D_MODEL),
        out_shape=jax.ShapeDtypeStruct((n, D_MODEL), jnp.float32),
        compiler_params=pltpu.CompilerParams(dimension_semantics=("arbitrary",), vmem_limit_bytes=VMEM_LIMIT),
        name="mix_out",
    )(x, o, u, halo, cb, g0, g1, cw, wg, wao, wco, wo)


def _mlp_kernel(x_ref, g2_ref, g3_ref, wup_ref, wdn_ref, x2_ref):
    x = x_ref[...]
    h = _rms(x, g2_ref[...]).astype(jnp.bfloat16)
    up = jnp.maximum(_dot(h, wup_ref[...]), 0.0)
    f = _dot((up * up).astype(jnp.bfloat16), wdn_ref[...])
    x2_ref[...] = x + _rms(f, g3_ref[...])


def _mlp(x, g2, g3, wup, wdn, drop_head_tiles_of=None):
    n = x.shape[0]
    tm = ROW_TILE
    full = lambda shape: pl.BlockSpec(shape, lambda i: (0,) * len(shape))
    rows = lambda w: pl.BlockSpec((tm, w), lambda i: (i, 0))
    if drop_head_tiles_of is None:
        out_rows, out_spec = n, rows(D_MODEL)
    else:
        tps = drop_head_tiles_of
        out_rows = n // tps * (tps - 1)
        out_spec = pl.BlockSpec((tm, D_MODEL), lambda i: (i // tps * (tps - 1) + jnp.maximum(i % tps - 1, 0), 0))
    return pl.pallas_call(
        _mlp_kernel,
        grid=(n // tm,),
        in_specs=[rows(D_MODEL), full((1, D_MODEL)), full((1, D_MODEL)), full((D_MODEL, D_FF)),
                  full((D_FF, D_MODEL))],
        out_specs=out_spec,
        out_shape=jax.ShapeDtypeStruct((out_rows, D_MODEL), jnp.float32),
        compiler_params=pltpu.CompilerParams(dimension_semantics=("arbitrary",), vmem_limit_bytes=VMEM_LIMIT),
        name="mlp",
    )(x, g2, g3, wup, wdn)


def _prepare_weights(w_in, conv_w, w_attn_out, w_conv_out, w_o, w_up, w_down):
    bf16 = jnp.bfloat16
    wt = jnp.concatenate([
        w_in[:, :, :END_Q] * (HEAD_DIM ** -0.5),
        w_in[:, :, END_Q:END_V],
        w_in[:, :, END_V:END_IQ] * (IDX_DIM ** -0.5),
        w_in[:, :, END_IQ:END_IK],
        w_in[:, :, END_IK:END_IW] * (IDX_HEADS ** -0.5),
        jnp.zeros(w_in.shape[:2] + (IW_ROWS - IDX_HEADS,), w_in.dtype),
    ], axis=2).astype(bf16).transpose(0, 2, 1)
    return dict(
        wt=wt,
        wc=w_in[:, :, END_IW:END_CX].astype(bf16),
        wg=w_in[:, :, END_CX:].astype(bf16),
        cw=conv_w,
        wao=w_attn_out.astype(bf16), wco=w_conv_out.astype(bf16), wo=w_o.astype(bf16),
        wup=w_up.astype(bf16), wdn=w_down.astype(bf16))


def _rope_tables(pos):
    inv = jnp.exp(-math.log(ROPE_THETA) * jnp.arange(ROT_HALF, dtype=jnp.float32) * (2.0 / ROT_DIM))
    ang = inv[:, None] * pos.astype(jnp.float32)[None, :]
    return jnp.cos(ang), jnp.sin(ang)


def _halo(u, n_seq, past):
    n = u.shape[0]
    per_seq = n // n_seq
    first = jnp.concatenate([jnp.zeros((n_seq, SUBLANES - (CONV_WIDTH - 1), CONV_DIM), u.dtype), past], axis=1)
    if per_seq >= ROW_TILE:
        tiles = per_seq // ROW_TILE
        tails = u.reshape(n_seq, tiles, ROW_TILE, CONV_DIM)[:, :tiles - 1, ROW_TILE - SUBLANES:]
        return jnp.concatenate([first[:, None], tails], axis=1).reshape(n_seq * tiles, SUBLANES, CONV_DIM)
    n_seg = ROW_TILE // per_seq
    return first.reshape(n_seq // n_seg, n_seg * SUBLANES, CONV_DIM)


def _with_ones_column(k):
    tail = jnp.zeros(k.shape[:-1] + (KEY_LANES - HEAD_DIM,), k.dtype).at[..., 0].set(1)
    return jnp.concatenate([k, tail], axis=-1)


def _pad_lanes(a, n_seq, per_seq, width):
    a = a.reshape(a.shape[0], n_seq, per_seq)
    a = jnp.pad(a, ((0, 0), (0, 0), (0, width - per_seq)))
    return a.reshape(a.shape[0], n_seq * width)


def kernel(x_prompt, x_sample, cache_k, cache_v, cache_kidx, state_conv, meta_tokens, norm_gains, w_in, conv_w,
           w_attn_out, w_conv_out, w_o, w_up, w_down):
    f32, bf16 = jnp.float32, jnp.bfloat16
    depth = w_in.shape[0]
    bp, seq, _ = x_prompt.shape
    bs, s_len, _ = x_sample.shape
    past_len = cache_k.shape[2]
    assert seq % Q_TILE == 0 and Q_TILE % CHUNK == 0 and ROW_TILE == Q_TILE and N_META <= Q_TILE
    assert (bs * s_len) % ROW_TILE == 0 and ROW_TILE % s_len == 0
    assert CONV_WIDTH - 1 <= s_len <= LANES and (past_len + s_len) % SUBLANES == 0

    pad = Q_TILE - N_META
    rp = Q_TILE + seq
    xp = jnp.concatenate([
        jnp.zeros((bp, pad, D_MODEL), f32),
        jnp.broadcast_to(meta_tokens[None].astype(f32), (bp, N_META, D_MODEL)),
        x_prompt.astype(f32)], axis=1).reshape(bp * rp, D_MODEL)
    cos_p, sin_p = _rope_tables(jnp.tile(jnp.arange(rp, dtype=jnp.int32) - pad, bp))
    topk_p = min(TOPK_MAX, seq // 4)

    xs = x_sample.astype(f32).reshape(bs * s_len, D_MODEL)
    ls = past_len + s_len
    cos_s, sin_s = _rope_tables(jnp.tile(past_len + jnp.arange(s_len, dtype=jnp.int32), bs))
    topk_s = min(TOPK_MAX, ls // 4)
    conv_zero = jnp.zeros((bp, CONV_WIDTH - 1, CONV_DIM), f32)

    weights = _prepare_weights(w_in, conv_w, w_attn_out, w_conv_out, w_o, w_up, w_down)
    outs = [[] for _ in range(8)]
    for l in range(depth):
        w = {name: value[l] for name, value in weights.items()}
        g = [norm_gains[l, i][None, :] for i in range(4)]

        qt, qit, wit, vt, kb, kib, kt32, vt32, kit32, u, cb = _in_proj(xp, g[0], w["wt"], w["wc"], cos_p, sin_p)
        o = _attention(qt, qit, wit, kb, kib, vt, n_seq=bp, lq=rp, lk=rp, tq=Q_TILE, kb_rows=ROW_TILE,
                       causal=True, first_key=pad, topk=topk_p)
        x1 = _mix_out(xp, o, u, _halo(u, bp, conv_zero), cb, g[0], g[1], w["cw"], w["wg"], w["wao"], w["wco"],
                      w["wo"])
        xp = _mlp(x1, g[2], g[3], w["wup"], w["wdn"], drop_head_tiles_of=rp // ROW_TILE if l == depth - 1 else None)
        outs[0].append(kt32.reshape(N_KV_HEADS, HEAD_DIM, bp, rp)[..., pad:].transpose(2, 3, 0, 1))
        outs[1].append(vt32.reshape(N_KV_HEADS, HEAD_DIM, bp, rp)[..., pad:].transpose(2, 3, 0, 1))
        outs[2].append(kit32.reshape(IDX_DIM, bp, rp)[..., pad:].transpose(1, 2, 0))
        outs[3].append(u.reshape(bp, rp, CONV_DIM)[:, rp - (CONV_WIDTH - 1):])

        qt, qit, wit, vt, kb, kib, kt32, vt32, kit32, u, cb = _in_proj(xs, g[0], w["wt"], w["wc"], cos_s, sin_s)
        k_new = kt32.T.reshape(bs, s_len, N_KV_HEADS, HEAD_DIM)
        v_new = vt32.T.reshape(bs, s_len, N_KV_HEADS, HEAD_DIM)
        ki_new = kit32.T.reshape(bs, s_len, IDX_DIM)
        k_all = jnp.concatenate([cache_k[l], k_new], axis=1).astype(bf16)
        v_all = jnp.concatenate([cache_v[l], v_new], axis=1).astype(bf16)
        ki_all = jnp.concatenate([cache_kidx[l], ki_new], axis=1).astype(bf16)
        o = _attention(
            _pad_lanes(qt, bs, s_len, LANES), _pad_lanes(qit, bs, s_len, LANES), _pad_lanes(wit, bs, s_len, LANES),
            _with_ones_column(k_all.transpose(2, 0, 1, 3).reshape(N_KV_HEADS, bs * ls, HEAD_DIM)),
            ki_all.reshape(bs * ls, IDX_DIM),
            v_all.reshape(bs, ls, KV_COLS).transpose(0, 2, 1),
            n_seq=bs, lq=LANES, lk=ls, tq=LANES, kb_rows=ls, causal=False, first_key=0, topk=topk_s)
        o = o.reshape(bs, LANES, Q_COLS)[:, :s_len].reshape(bs * s_len, Q_COLS)
        x1 = _mix_out(xs, o, u, _halo(u, bs, state_conv[l].astype(f32)), cb, g[0], g[1], w["cw"], w["wg"],
                      w["wao"], w["wco"], w["wo"])
        xs = _mlp(x1, g[2], g[3], w["wup"], w["wdn"])
        outs[4].append(k_new)
        outs[5].append(v_new)
        outs[6].append(ki_new)
        outs[7].append(u.reshape(bs, s_len, CONV_DIM)[:, s_len - (CONV_WIDTH - 1):])

    y_prompt = xp.reshape(bp, seq, D_MODEL)
    y_sample = xs.reshape(bs, s_len, D_MODEL)
    return (y_prompt, y_sample) + tuple(jnp.stack(o) for o in outs)
```

```python
import functools
import math

import jax
import jax.numpy as jnp
from jax import lax
from jax.experimental import pallas as pl
from jax.experimental.pallas import tpu as pltpu

D_MODEL = 1024
CHUNK = 64
CHUNK_SHIFT = CHUNK.bit_length() - 1
N_META = 16
HEAD_DIM = 64
N_HEADS = 8
N_KV_HEADS = 2
N_GROUPS = N_HEADS // N_KV_HEADS
IDX_HEADS = 4
IDX_DIM = 64
ROT_DIM = HEAD_DIM // 4
ROT_HALF = ROT_DIM // 2
ROPE_THETA = 500000.0
CONV_DIM = D_MODEL // 2
CONV_WIDTH = 3
D_FF = 4 * D_MODEL
TOPK_MAX = 256
EPS = 1e-6

Q_COLS = N_HEADS * HEAD_DIM
KV_COLS = N_KV_HEADS * HEAD_DIM
IQ_COLS = IDX_HEADS * IDX_DIM
END_Q = Q_COLS
END_K = END_Q + KV_COLS
END_V = END_K + KV_COLS
END_IQ = END_V + IQ_COLS
END_IK = END_IQ + IDX_DIM
END_IW = END_IK + IDX_HEADS
END_CB = END_IW + CONV_DIM
END_CC = END_CB + CONV_DIM
END_CX = END_CC + CONV_DIM
END_GA = END_CX + D_MODEL
D_IN = END_GA + D_MODEL

SUBLANES = 8
LANES = 128
IW_ROWS = SUBLANES
ZT_Q = 0
ZT_K = ZT_Q + Q_COLS
ZT_V = ZT_K + KV_COLS
ZT_IQ = ZT_V + KV_COLS
ZT_IK = ZT_IQ + IQ_COLS
ZT_IW = ZT_IK + IDX_DIM
ZT_ROWS = ZT_IW + IW_ROWS

ROW_TILE = 256
Q_TILE = 256
NEG_BIAS = -1e30
LOG2_E = math.log2(math.e)
KEY_LANES = 128
BOUND_MARGIN = 1.01
MIN_DENOMINATOR = 2.0 ** -60
INT_MIN = -(2 ** 31)
NEG_INF_KEY = INT_MIN + 0x7FFFFF
MIN_FINITE_KEY = NEG_INF_KEY + 1
POS_INF_KEY = 0x7F800000
INT_MAX = 2 ** 31 - 1
ZERO_KEY = 0
TAKE_ALL = 2 ** 30
INTERP_STEPS = 20
STALL_STEPS = 2
SUM_CHAINS = 4
SEARCH_CAP = INTERP_STEPS + 34
SEARCH_UNROLL = 2
VMEM_LIMIT = 56 * 1024 * 1024


def _rms(x, g):
    return x * lax.rsqrt(jnp.mean(x * x, axis=-1, keepdims=True) + EPS) * g


def _sigmoid(x):
    return 1.0 / (1.0 + jnp.exp(-x))


def _dot(a, b):
    return jnp.dot(a, b, preferred_element_type=jnp.float32)


def _dot_nt(a, b):
    return lax.dot_general(a, b, (((1,), (1,)), ((), ())), preferred_element_type=jnp.float32)


def _rope_rows(zt, n_heads, cos, sin):
    pieces = []
    for h in range(n_heads):
        o = h * HEAD_DIM
        x1 = zt[o:o + ROT_HALF]
        x2 = zt[o + ROT_HALF:o + ROT_DIM]
        pieces.append(x1 * cos - x2 * sin)
        pieces.append(x2 * cos + x1 * sin)
        pieces.append(zt[o + ROT_DIM:o + HEAD_DIM])
    return jnp.concatenate(pieces, axis=0)


def _in_proj_kernel(x_ref, g_ref, wt_ref, wc_ref, cos_ref, sin_ref,
                    qt_ref, qit_ref, wit_ref, vt_ref, kb_ref, kib_ref,
                    kt_ref, vt32_ref, kit_ref, u_ref, cb_ref):
    h = _rms(x_ref[...], g_ref[...]).astype(jnp.bfloat16)
    zt = _dot_nt(wt_ref[...], h)
    cos = cos_ref[...]
    sin = sin_ref[...]
    qt_ref[...] = (_rope_rows(zt[ZT_Q:ZT_K], N_HEADS, cos, sin) * LOG2_E).astype(jnp.bfloat16)
    kt = _rope_rows(zt[ZT_K:ZT_V], N_KV_HEADS, cos, sin)
    kt_ref[...] = kt
    k = kt.T
    tail = jnp.where(lax.broadcasted_iota(jnp.int32, (k.shape[0], KEY_LANES - HEAD_DIM), 1) == 0, 1.0, 0.0)
    for g in range(N_KV_HEADS):
        kb_ref[g] = jnp.concatenate([k[:, g * HEAD_DIM:(g + 1) * HEAD_DIM], tail], axis=1).astype(jnp.bfloat16)
    vt = zt[ZT_V:ZT_IQ]
    vt_ref[0] = vt.astype(jnp.bfloat16)
    vt32_ref[...] = vt
    qit_ref[...] = _rope_rows(zt[ZT_IQ:ZT_IK], IDX_HEADS, cos, sin).astype(jnp.bfloat16)
    kit = _rope_rows(zt[ZT_IK:ZT_IW], 1, cos, sin)
    kit_ref[...] = kit
    kib_ref[...] = kit.T.astype(jnp.bfloat16)
    wit_ref[...] = zt[ZT_IW:ZT_ROWS]
    zc = _dot(h, wc_ref[...])
    cb_ref[...] = zc[:, :CONV_DIM]
    u_ref[...] = zc[:, CONV_DIM:2 * CONV_DIM] * zc[:, 2 * CONV_DIM:]


def _in_proj(x, g0, wt, wc, cos_t, sin_t):
    n = x.shape[0]
    tm = ROW_TILE
    nb = n // tm
    f32, bf16 = jnp.float32, jnp.bfloat16
    full = lambda shape: pl.BlockSpec(shape, lambda i: (0,) * len(shape))
    rows = lambda w: pl.BlockSpec((tm, w), lambda i: (i, 0))
    cols = lambda r: pl.BlockSpec((r, tm), lambda i: (0, i))
    out_shape = (
        jax.ShapeDtypeStruct((Q_COLS, n), bf16),
        jax.ShapeDtypeStruct((IQ_COLS, n), bf16),
        jax.ShapeDtypeStruct((IW_ROWS, n), f32),
        jax.ShapeDtypeStruct((nb, KV_COLS, tm), bf16),
        jax.ShapeDtypeStruct((N_KV_HEADS, n, KEY_LANES), bf16),
        jax.ShapeDtypeStruct((n, IDX_DIM), bf16),
        jax.ShapeDtypeStruct((KV_COLS, n), f32),
        jax.ShapeDtypeStruct((KV_COLS, n), f32),
        jax.ShapeDtypeStruct((IDX_DIM, n), f32),
        jax.ShapeDtypeStruct((n, CONV_DIM), f32),
        jax.ShapeDtypeStruct((n, CONV_DIM), f32),
    )
    out_specs = (
        cols(Q_COLS), cols(IQ_COLS), cols(IW_ROWS),
        pl.BlockSpec((1, KV_COLS, tm), lambda i: (i, 0, 0)),
        pl.BlockSpec((N_KV_HEADS, tm, KEY_LANES), lambda i: (0, i, 0)),
        rows(IDX_DIM), cols(KV_COLS), cols(KV_COLS), cols(IDX_DIM), rows(CONV_DIM), rows(CONV_DIM),
    )
    return pl.pallas_call(
        _in_proj_kernel,
        grid=(nb,),
        in_specs=[rows(D_MODEL), full((1, D_MODEL)), full((ZT_ROWS, D_MODEL)), full((D_MODEL, 3 * CONV_DIM)),
                  cols(ROT_HALF), cols(ROT_HALF)],
        out_specs=out_specs,
        out_shape=out_shape,
        compiler_params=pltpu.CompilerParams(dimension_semantics=("arbitrary",), vmem_limit_bytes=VMEM_LIMIT),
        name="in_proj",
    )(x, g0, wt, wc, cos_t, sin_t)


def _attn_kernel(qt_ref, qit_ref, wit_ref, kb_ref, kib_ref, vt_ref, tri_ref, o_ref,
                 skey_ref, ot_ref, qg_ref, acc_ref, lg_ref, p_ref, kmax_ref, *, tq, kb_rows, n_kblocks, count_blocks,
                 causal, first_key, topk):
    j = pl.program_id(1)
    nkb = (j + 1) if causal else n_kblocks
    i32 = jnp.int32

    q_row = j * tq + lax.broadcasted_iota(i32, (1, tq), 1)
    if causal:
        lim = jnp.where(q_row >= first_key, ((q_row >> CHUNK_SHIFT) << CHUNK_SHIFT) + CHUNK, 0)
    else:
        lim = jnp.full((1, tq), n_kblocks * kb_rows, i32)

    def key_rows(kb):
        return kb * kb_rows + lax.broadcasted_iota(i32, (kb_rows, 1), 0)

    def kslice(kb):
        return pl.ds(pl.multiple_of(kb * kb_rows, SUBLANES), kb_rows)

    def to_key(f):
        bits = lax.bitcast_convert_type(f, i32)
        return bits ^ ((bits >> 31) & 0x7FFFFFFF)

    def to_f32(k):
        return lax.bitcast_convert_type(k ^ ((k >> 31) & 0x7FFFFFFF), jnp.float32)

    def score_block(kb, edge):
        ki = kib_ref[kslice(kb), :]
        acc = jnp.zeros((kb_rows, tq), jnp.float32)
        for h in range(IDX_HEADS):
            s = _dot(ki, qit_ref[h * IDX_DIM:(h + 1) * IDX_DIM, :])
            acc = acc + wit_ref[h:h + 1, :] * jnp.maximum(s, 0.0)
        acc = jnp.where(acc == 0.0, 0.0, acc)
        if edge:
            rows = key_rows(kb)
            adm = jnp.logical_and(rows >= first_key, rows < lim)
            sk = to_key(jnp.where(adm, acc, -jnp.inf))
            masked = jnp.where(adm, acc, 0.0)
            scale = (jnp.sum(masked * masked, axis=0, keepdims=True),
                     jnp.sum(jnp.where(adm, 1.0, 0.0), axis=0, keepdims=True))
        else:
            sk = to_key(acc)
            scale = None
        skey_ref[kslice(kb), :] = sk
        return jnp.sum(jnp.where(sk > ZERO_KEY, 1, 0), axis=0, keepdims=True), scale

    zero = jnp.zeros((1, tq), jnp.float32)
    izero = jnp.zeros((1, tq), i32)
    c_pos, (s2, s0) = score_block(0, True)
    n_inner_pairs = (nkb - 2) // 2
    c_pos = lax.fori_loop(
        0, n_inner_pairs,
        lambda p, c: c + score_block(2 * p + 1, False)[0] + score_block(2 * p + 2, False)[0], c_pos)
    n_left = nkb - 1 - 2 * jnp.maximum(n_inner_pairs, 0)

    def last_block(c):
        c_l, (s2_l, s0_l) = score_block(nkb - 1, True)
        return c + c_l, s2 + s2_l, s0 + s0_l

    c_pos, s2, s0 = lax.cond(
        n_left == 2, lambda: last_block(c_pos + score_block(nkb - 2, False)[0]),
        lambda: lax.cond(n_left == 1, lambda: last_block(c_pos), lambda: (c_pos, s2, s0)))

    cslice_rows = count_blocks * kb_rows
    n_steps = (nkb + count_blocks - 1) // count_blocks
    if count_blocks > 1:
        skey_ref[kslice(nkb), :] = jnp.full((kb_rows, tq), INT_MIN, i32)

    assert cslice_rows % (SUM_CHAINS * SUBLANES) == 0

    def count_ge(cand):
        def body(c, acc):
            sk = skey_ref[pl.ds(pl.multiple_of(c * cslice_rows, SUBLANES), cslice_rows), :]
            ind = jnp.where(sk >= cand, 1, 0)
            return acc + jnp.sum(ind.reshape(SUM_CHAINS, -1, SUBLANES, tq), axis=1)
        acc = lax.fori_loop(0, n_steps, body, jnp.zeros((SUM_CHAINS, SUBLANES, tq), i32))
        return jnp.sum(acc.reshape(SUM_CHAINS * SUBLANES, tq), axis=0, keepdims=True)

    n_adm = jnp.maximum(lim - first_key, 0)
    n_f = jnp.maximum(n_adm, 1).astype(jnp.float32)
    sigma = jnp.sqrt(s2 / jnp.maximum(s0, 1.0))
    tail = jnp.clip(topk / n_f, 1e-6, 1.0 - 1e-6)
    tq_ = jnp.sqrt(-2.0 * jnp.log(jnp.minimum(tail, 1.0 - tail)))
    zq = tq_ - (2.515517 + 0.802853 * tq_ + 0.010328 * tq_ * tq_) / (
        1.0 + 1.432788 * tq_ + 0.189269 * tq_ * tq_ + 0.001308 * tq_ * tq_ * tq_)
    zq = jnp.where(tail < 0.5, zq, -zq)
    step_scale = 1.5 * sigma / jnp.maximum(jnp.abs(zq), 0.5)
    log_k = math.log(topk + 0.5)

    def search_cond(st):
        return jnp.logical_and(st[0] < SEARCH_CAP, st[1] > 0)

    def search_step(st):
        it, lo, hi, c_lo, c_hi, lo_real, hi_real, w_lo, w_hi, last, stall, done = st
        f_lo, f_hi = to_f32(lo), to_f32(hi)
        g_lo = jnp.log(c_lo.astype(jnp.float32)) - log_k
        g_hi = log_k - jnp.log(jnp.maximum(c_hi.astype(jnp.float32), 0.5))
        t_in = f_lo + (f_hi - f_lo) * (w_lo * g_lo / (w_lo * g_lo + w_hi * g_hi))
        t_up = f_lo + step_scale * g_lo
        t_dn = f_hi - step_scale * g_hi
        t = jnp.where(lo_real > 0, jnp.where(hi_real > 0, t_in, t_up), t_dn)
        cand = jnp.minimum(jnp.maximum(to_key(t), lo + 1), hi - 1)
        mid = (lo >> 1) + (hi >> 1) + (lo & hi & 1)
        cand = jnp.where(stall >= STALL_STEPS, mid, jnp.where(it >= INTERP_STEPS, mid, cand))
        cand = jnp.where(it == 0, jnp.where(below, ZERO_KEY, cand), cand)
        cnt = count_ge(cand)
        up = jnp.where(done > 0, 0, jnp.where(cnt >= topk, 1, 0))
        dn = jnp.where(done > 0, 0, jnp.where(cnt >= topk, 0, 1))
        both = lo_real * hi_real
        stall = stall + both * jnp.where(cnt == c_lo, 1, jnp.where(cnt == c_hi, 1, 0))
        lo = jnp.where(up > 0, cand, lo)
        c_lo = jnp.where(up > 0, cnt, c_lo)
        lo_real = jnp.maximum(lo_real, up)
        hi = jnp.where(dn > 0, cand, hi)
        c_hi = jnp.where(dn > 0, cnt, c_hi)
        hi_real = jnp.maximum(hi_real, dn)
        w_hi = jnp.where(up > 0, jnp.where(last > 0, 0.5 * w_hi, 1.0), jnp.where(dn > 0, 1.0, w_hi))
        w_lo = jnp.where(dn > 0, jnp.where(last < 0, 0.5 * w_lo, 1.0), jnp.where(up > 0, 1.0, w_lo))
        last = up - dn + (1 - up - dn) * last
        done = jnp.where(c_lo == topk, 1, jnp.where(hi == lo + 1, 1, done))
        return it + 1, lo, hi, c_lo, c_hi, lo_real, hi_real, w_lo, w_hi, last, stall, done

    def search_body(st):
        inner = (st[0],) + st[2:]
        for _ in range(SEARCH_UNROLL):
            inner = search_step(inner)
        return (inner[0], jnp.sum(1 - inner[-1])) + inner[1:]

    small = n_adm <= topk
    above = c_pos >= topk
    below = jnp.where(small, 0, jnp.where(above, 0, 1)) > 0
    lo0 = jnp.where(small, NEG_INF_KEY, jnp.where(above, ZERO_KEY + 1, MIN_FINITE_KEY))
    hi0 = jnp.where(small, NEG_INF_KEY + 1, jnp.where(above, POS_INF_KEY, ZERO_KEY + 1))
    c_lo0 = jnp.where(small, topk, jnp.where(above, c_pos, n_adm))
    c_hi0 = jnp.where(small, 0, jnp.where(above, 0, c_pos))
    lo_real0 = jnp.where(small, 0, jnp.where(above, 1, 0))
    hi_real0 = jnp.where(small, 0, jnp.where(above, 0, 1))
    done0 = jnp.where(c_lo0 == topk, 1, jnp.where(hi0 == lo0 + 1, 1, 0))
    one = jnp.ones((1, tq), jnp.float32)
    st = lax.while_loop(search_cond, search_body,
                        (i32(0), jnp.sum(1 - done0), lo0, hi0, c_lo0, c_hi0, lo_real0, hi_real0, one, one,
                         izero, izero, done0))
    thr, hi, c_hi = st[2], st[3], st[5]
    need = jnp.where(small, 0, jnp.where(hi == thr + 1, topk - c_hi, TAKE_ALL)).astype(jnp.float32)

    def block_bias(kb, run):
        sk = skey_ref[kslice(kb), :]
        tie = jnp.where(sk == thr, 1.0, 0.0)
        rank = _dot(tri_ref[...], tie.astype(jnp.bfloat16)) + run
        tie_bias = jnp.where(rank <= need, 0.0, NEG_BIAS)
        bias = jnp.where(sk > thr, 0.0, jnp.where(sk == thr, tie_bias, NEG_BIAS))
        return bias, rank[kb_rows - 1:kb_rows, :]

    gw = N_GROUPS * tq
    groups = range(N_KV_HEADS)
    lane = lax.broadcasted_iota(i32, (1, LANES), 1)

    @pl.when(j == 0)
    def _():
        def norm_body(kb, best):
            out = []
            for g in groups:
                k = jnp.where(lane < HEAD_DIM, kb_ref[g, kslice(kb), :].astype(jnp.float32), 0.0)
                n2 = jnp.max(jnp.sum(k * k, axis=1, keepdims=True), axis=0, keepdims=True)
                out.append(jnp.maximum(best[g], n2))
            return tuple(out)
        best = lax.fori_loop(0, n_kblocks, norm_body, (jnp.zeros((1, 1), jnp.float32),) * N_KV_HEADS)
        for g in groups:
            kmax_ref[g] = jnp.broadcast_to(best[g], (SUBLANES, LANES))

    pad_row = lax.broadcasted_iota(i32, (KEY_LANES - HEAD_DIM, 1), 0)
    for g in groups:
        q = jnp.concatenate([qt_ref[h * HEAD_DIM:(h + 1) * HEAD_DIM, :]
                             for h in range(g * N_GROUPS, (g + 1) * N_GROUPS)], axis=1)
        qf = q.astype(jnp.float32)
        bound = jnp.sqrt(jnp.sum(qf * qf, axis=0, keepdims=True) * kmax_ref[g][0:1, 0:1]) * BOUND_MARGIN
        pad = jnp.where(pad_row == 0, -bound, 0.0).astype(jnp.bfloat16)
        qg_ref[g] = jnp.concatenate([q, pad], axis=0)

    def value_rows(kb, g):
        ones = jnp.ones((SUBLANES, kb_rows), jnp.bfloat16)
        return jnp.concatenate([vt_ref[kb, g * HEAD_DIM:(g + 1) * HEAD_DIM, :], ones], axis=0)

    def pipelined(logits_stage, value_stage, ms):
        def step(kb, slot, ms, staged, prefetch):
            extra, run = staged
            nxt = logits_stage(kb + 1, 1 - slot, run) if prefetch else None
            return tuple(value_stage(kb, g, slot, ms[g], extra[g]) for g in groups), nxt

        def pair_body(i, carry):
            ms, staged = carry
            ms, staged = step(2 * i, 0, ms, staged, True)
            return step(2 * i + 1, 1, ms, staged, True)

        n_pairs = (nkb - 1) // 2
        ms, staged = lax.fori_loop(0, n_pairs, pair_body, (ms, logits_stage(0, 0, zero)))
        last = 2 * n_pairs

        def two_left():
            ms1, staged1 = step(last, 0, ms, staged, True)
            return step(last + 1, 1, ms1, staged1, False)[0]

        return lax.cond(nkb - last == 2, two_left, lambda: step(last, 0, ms, staged, False)[0])

    def logits(kb, g, bias):
        return _dot(kb_ref[g, kslice(kb), :], qg_ref[g]) + bias

    def bounded_logits_stage(kb, slot, run):
        bias, run = block_bias(kb, run)
        bias = jnp.concatenate([bias] * N_GROUPS, axis=1)
        for g in groups:
            p_ref[slot, g] = jnp.exp2(logits(kb, g, bias)).astype(jnp.bfloat16)
        return (izero,) * N_KV_HEADS, run

    def bounded_value_stage(kb, g, slot, m, unused):
        acc_ref[g] = acc_ref[g] + _dot(value_rows(kb, g), p_ref[slot, g])
        return m

    def online_logits_stage(kb, slot, run):
        bias, run = block_bias(kb, run)
        bias = jnp.concatenate([bias] * N_GROUPS, axis=1)
        m_blks = []
        for g in groups:
            lg = logits(kb, g, bias)
            lg_ref[slot, g] = lg
            m_blks.append(jnp.max(lg, axis=0, keepdims=True))
        return tuple(m_blks), run

    def online_value_stage(kb, g, slot, m, m_blk):
        m_new = jnp.maximum(m, m_blk)
        p = jnp.exp2(lg_ref[slot, g] - m_new).astype(jnp.bfloat16)
        acc_ref[g] = jnp.exp2(m - m_new) * acc_ref[g] + _dot(value_rows(kb, g), p)
        return m_new

    has_key = jnp.concatenate([n_adm] * N_GROUPS, axis=1) > 0
    acc_ref[...] = jnp.zeros_like(acc_ref)
    pipelined(bounded_logits_stage, bounded_value_stage, (izero,) * N_KV_HEADS)
    underflow = izero[:, 0:1]
    for g in groups:
        denom = acc_ref[g][HEAD_DIM:HEAD_DIM + 1, :]
        underflow = underflow + jnp.sum(jnp.where(has_key, jnp.where(denom < MIN_DENOMINATOR, 1, 0), 0),
                                        axis=1, keepdims=True)

    @pl.when(underflow[0, 0] > 0)
    def _():
        acc_ref[...] = jnp.zeros_like(acc_ref)
        for g in groups:
            qg_ref[g, HEAD_DIM:, :] = jnp.zeros((KEY_LANES - HEAD_DIM, gw), jnp.bfloat16)
        pipelined(online_logits_stage, online_value_stage,
                  (jnp.full((1, gw), NEG_BIAS, jnp.float32),) * N_KV_HEADS)

    for g in groups:
        acc = acc_ref[g]
        og = jnp.where(has_key, acc[:HEAD_DIM] / acc[HEAD_DIM:HEAD_DIM + 1], 0.0)
        for hh in range(N_GROUPS):
            h = g * N_GROUPS + hh
            ot_ref[h * HEAD_DIM:(h + 1) * HEAD_DIM, :] = og[:, hh * tq:(hh + 1) * tq]

    o_ref[...] = ot_ref[...].T.astype(o_ref.dtype)


def _attention(qt, qit, wit, kb, kib, vt, *, n_seq, lq, lk, tq, kb_rows, causal, first_key, topk):
    nq = lq // tq
    n_kblocks = lk // kb_rows
    tri = jnp.tril(jnp.ones((kb_rows, kb_rows), jnp.bfloat16))
    count_blocks = 2 if causal else 1
    kernel = functools.partial(_attn_kernel, tq=tq, kb_rows=kb_rows, n_kblocks=n_kblocks,
                               count_blocks=count_blocks, causal=causal, first_key=first_key, topk=topk)
    return pl.pallas_call(
        kernel,
        grid=(n_seq, nq),
        in_specs=[
            pl.BlockSpec((Q_COLS, tq), lambda b, j: (0, b * nq + j)),
            pl.BlockSpec((IQ_COLS, tq), lambda b, j: (0, b * nq + j)),
            pl.BlockSpec((IW_ROWS, tq), lambda b, j: (0, b * nq + j)),
            pl.BlockSpec((N_KV_HEADS, lk, KEY_LANES), lambda b, j: (0, b, 0)),
            pl.BlockSpec((lk, IDX_DIM), lambda b, j: (b, 0)),
            pl.BlockSpec((n_kblocks, KV_COLS, kb_rows), lambda b, j: (b, 0, 0)),
            pl.BlockSpec((kb_rows, kb_rows), lambda b, j: (0, 0)),
        ],
        out_specs=pl.BlockSpec((tq, Q_COLS), lambda b, j: (b * nq + j, 0)),
        out_shape=jax.ShapeDtypeStruct((n_seq * lq, Q_COLS), jnp.bfloat16),
        scratch_shapes=[
            pltpu.VMEM((lk + (count_blocks - 1) * kb_rows, tq), jnp.int32),
            pltpu.VMEM((Q_COLS, tq), jnp.float32),
            pltpu.VMEM((N_KV_HEADS, KEY_LANES, N_GROUPS * tq), jnp.bfloat16),
            pltpu.VMEM((N_KV_HEADS, HEAD_DIM + SUBLANES, N_GROUPS * tq), jnp.float32),
            pltpu.VMEM((2, N_KV_HEADS, kb_rows, N_GROUPS * tq), jnp.float32),
            pltpu.VMEM((2, N_KV_HEADS, kb_rows, N_GROUPS * tq), jnp.bfloat16),
            pltpu.VMEM((N_KV_HEADS, SUBLANES, LANES), jnp.float32),
        ],
        compiler_params=pltpu.CompilerParams(dimension_semantics=("arbitrary", "arbitrary"),
                                             vmem_limit_bytes=VMEM_LIMIT),
        name="dsa_attention_causal" if causal else "dsa_attention_full",
    )(qt, qit, wit, kb, kib, vt, tri)


def _mix_out_kernel(x_ref, o_ref, u_ref, halo_ref, cb_ref, g0_ref, g1_ref, cw_ref,
                    wg_ref, wao_ref, wco_ref, wo_ref, x1_ref, *, n_seg):
    x = x_ref[...]
    h = _rms(x, g0_ref[...]).astype(jnp.bfloat16)
    gates = _dot(h, wg_ref[...])
    y_a = _dot(o_ref[...], wao_ref[...])
    u = u_ref[...]
    halo = halo_ref[0]
    t = u.shape[0]
    seg = t // n_seg
    row = lax.broadcasted_iota(jnp.int32, (t, 1), 0)
    u1 = pltpu.roll(u, 1, 0)
    u2 = pltpu.roll(u, 2, 0)
    for s in range(n_seg):
        h6 = halo[s * SUBLANES + 6:s * SUBLANES + 7]
        h7 = halo[s * SUBLANES + 7:s * SUBLANES + 8]
        u1 = jnp.where(row == s * seg, h7, u1)
        u2 = jnp.where(row == s * seg, h6, jnp.where(row == s * seg + 1, h7, u2))
    conv = cw_ref[0:1] * u2 + cw_ref[1:2] * u1 + cw_ref[2:3] * u
    y_b = _dot((cb_ref[...] * conv).astype(jnp.bfloat16), wco_ref[...])
    m = _sigmoid(gates[:, :D_MODEL]) * y_a + _sigmoid(gates[:, D_MODEL:]) * y_b
    a = _dot(m.astype(jnp.bfloat16), wo_ref[...])
    x1_ref[...] = x + _rms(a, g1_ref[...])


def _mix_out(x, o, u, halo, cb, g0, g1, cw, wg, wao, wco, wo):
    n = x.shape[0]
    tm = ROW_TILE
    n_seg = halo.shape[1] // SUBLANES
    full = lambda shape: pl.BlockSpec(shape, lambda i: (0,) * len(shape))
    rows = lambda w: pl.BlockSpec((tm, w), lambda i: (i, 0))
    return pl.pallas_call(
        functools.partial(_mix_out_kernel, n_seg=n_seg),
        grid=(n // tm,),
        in_specs=[rows(D_MODEL), rows(Q_COLS), rows(CONV_DIM),
                  pl.BlockSpec((1, n_seg * SUBLANES, CONV_DIM), lambda i: (i, 0, 0)), rows(CONV_DIM),
                  full((1, D_MODEL)), full((1, D_MODEL)), full((CONV_WIDTH, CONV_DIM)),
                  full((D_MODEL, 2 * D_MODEL)), full((Q_COLS, D_MODEL)), full((CONV_DIM, D_MODEL)),
                  full((D_MODEL, D_MODEL))],
        out_specs=rows(D_MODEL),
        out_shape=jax.ShapeDtypeStruct((n, D_MODEL), jnp.float32),
        compiler_params=pltpu.CompilerParams(dimension_semantics=("arbitrary",), vmem_limit_bytes=VMEM_LIMIT),
        name="mix_out",
    )(x, o, u, halo, cb, g0, g1, cw, wg, wao, wco, wo)


def _mlp_kernel(x_ref, g2_ref, g3_ref, wup_ref, wdn_ref, x2_ref):
    x = x_ref[...]
    h = _rms(x, g2_ref[...]).astype(jnp.bfloat16)
    up = jnp.maximum(_dot(h, wup_ref[...]), 0.0)
    f = _dot((up * up).astype(jnp.bfloat16), wdn_ref[...])
    x2_ref[...] = x + _rms(f, g3_ref[...])


def _mlp(x, g2, g3, wup, wdn, drop_head_tiles_of=None):
    n = x.shape[0]
    tm = ROW_TILE
    full = lambda shape: pl.BlockSpec(shape, lambda i: (0,) * len(shape))
    rows = lambda w: pl.BlockSpec((tm, w), lambda i: (i, 0))
    if drop_head_tiles_of is None:
        out_rows, out_spec = n, rows(D_MODEL)
    else:
        tps = drop_head_tiles_of
        out_rows = n // tps * (tps - 1)
        out_spec = pl.BlockSpec((tm, D_MODEL), lambda i: (i // tps * (tps - 1) + jnp.maximum(i % tps - 1, 0), 0))
    return pl.pallas_call(
        _mlp_kernel,
        grid=(n // tm,),
        in_specs=[rows(D_MODEL), full((1, D_MODEL)), full((1, D_MODEL)), full((D_MODEL, D_FF)),
                  full((D_FF, D_MODEL))],
        out_specs=out_spec,
        out_shape=jax.ShapeDtypeStruct((out_rows, D_MODEL), jnp.float32),
        compiler_params=pltpu.CompilerParams(dimension_semantics=("arbitrary",), vmem_limit_bytes=VMEM_LIMIT),
        name="mlp",
    )(x, g2, g3, wup, wdn)


def _prepare_weights(w_in, conv_w, w_attn_out, w_conv_out, w_o, w_up, w_down):
    bf16 = jnp.bfloat16
    wt = jnp.concatenate([
        w_in[:, :, :END_Q] * (HEAD_DIM ** -0.5),
        w_in[:, :, END_Q:END_V],
        w_in[:, :, END_V:END_IQ] * (IDX_DIM ** -0.5),
        w_in[:, :, END_IQ:END_IK],
        w_in[:, :, END_IK:END_IW] * (IDX_HEADS ** -0.5),
        jnp.zeros(w_in.shape[:2] + (IW_ROWS - IDX_HEADS,), w_in.dtype),
    ], axis=2).astype(bf16).transpose(0, 2, 1)
    return dict(
        wt=wt,
        wc=w_in[:, :, END_IW:END_CX].astype(bf16),
        wg=w_in[:, :, END_CX:].astype(bf16),
        cw=conv_w,
        wao=w_attn_out.astype(bf16), wco=w_conv_out.astype(bf16), wo=w_o.astype(bf16),
        wup=w_up.astype(bf16), wdn=w_down.astype(bf16))


def _rope_tables(pos):
    inv = jnp.exp(-math.log(ROPE_THETA) * jnp.arange(ROT_HALF, dtype=jnp.float32) * (2.0 / ROT_DIM))
    ang = inv[:, None] * pos.astype(jnp.float32)[None, :]
    return jnp.cos(ang), jnp.sin(ang)


def _halo(u, n_seq, past):
    n = u.shape[0]
    per_seq = n // n_seq
    first = jnp.concatenate([jnp.zeros((n_seq, SUBLANES - (CONV_WIDTH - 1), CONV_DIM), u.dtype), past], axis=1)
    if per_seq >= ROW_TILE:
        tiles = per_seq // ROW_TILE
        tails = u.reshape(n_seq, tiles, ROW_TILE, CONV_DIM)[:, :tiles - 1, ROW_TILE - SUBLANES:]
        return jnp.concatenate([first[:, None], tails], axis=1).reshape(n_seq * tiles, SUBLANES, CONV_DIM)
    n_seg = ROW_TILE // per_seq
    return first.reshape(n_seq // n_seg, n_seg * SUBLANES, CONV_DIM)


def _with_ones_column(k):
    tail = jnp.zeros(k.shape[:-1] + (KEY_LANES - HEAD_DIM,), k.dtype).at[..., 0].set(1)
    return jnp.concatenate([k, tail], axis=-1)


def _pad_lanes(a, n_seq, per_seq, width):
    a = a.reshape(a.shape[0], n_seq, per_seq)
    a = jnp.pad(a, ((0, 0), (0, 0), (0, width - per_seq)))
    return a.reshape(a.shape[0], n_seq * width)


def kernel(x_prompt, x_sample, cache_k, cache_v, cache_kidx, state_conv, meta_tokens, norm_gains, w_in, conv_w,
           w_attn_out, w_conv_out, w_o, w_up, w_down):
    f32, bf16 = jnp.float32, jnp.bfloat16
    depth = w_in.shape[0]
    bp, seq, _ = x_prompt.shape
    bs, s_len, _ = x_sample.shape
    past_len = cache_k.shape[2]
    assert seq % Q_TILE == 0 and Q_TILE % CHUNK == 0 and ROW_TILE == Q_TILE and N_META <= Q_TILE
    assert (bs * s_len) % ROW_TILE == 0 and ROW_TILE % s_len == 0
    assert CONV_WIDTH - 1 <= s_len <= LANES and (past_len + s_len) % SUBLANES == 0

    pad = Q_TILE - N_META
    rp = Q_TILE + seq
    xp = jnp.concatenate([
        jnp.zeros((bp, pad, D_MODEL), f32),
        jnp.broadcast_to(meta_tokens[None].astype(f32), (bp, N_META, D_MODEL)),
        x_prompt.astype(f32)], axis=1).reshape(bp * rp, D_MODEL)
    cos_p, sin_p = _rope_tables(jnp.tile(jnp.arange(rp, dtype=jnp.int32) - pad, bp))
    topk_p = min(TOPK_MAX, seq // 4)

    xs = x_sample.astype(f32).reshape(bs * s_len, D_MODEL)
    ls = past_len + s_len
    cos_s, sin_s = _rope_tables(jnp.tile(past_len + jnp.arange(s_len, dtype=jnp.int32), bs))
    topk_s = min(TOPK_MAX, ls // 4)
    conv_zero = jnp.zeros((bp, CONV_WIDTH - 1, CONV_DIM), f32)

    weights = _prepare_weights(w_in, conv_w, w_attn_out, w_conv_out, w_o, w_up, w_down)
    outs = [[] for _ in range(8)]
    for l in range(depth):
        w = {name: value[l] for name, value in weights.items()}
        g = [norm_gains[l, i][None, :] for i in range(4)]

        qt, qit, wit, vt, kb, kib, kt32, vt32, kit32, u, cb = _in_proj(xp, g[0], w["wt"], w["wc"], cos_p, sin_p)
        o = _attention(qt, qit, wit, kb, kib, vt, n_seq=bp, lq=rp, lk=rp, tq=Q_TILE, kb_rows=ROW_TILE,
                       causal=True, first_key=pad, topk=topk_p)
        x1 = _mix_out(xp, o, u, _halo(u, bp, conv_zero), cb, g[0], g[1], w["cw"], w["wg"], w["wao"], w["wco"],
                      w["wo"])
        xp = _mlp(x1, g[2], g[3], w["wup"], w["wdn"], drop_head_tiles_of=rp // ROW_TILE if l == depth - 1 else None)
        outs[0].append(kt32.reshape(N_KV_HEADS, HEAD_DIM, bp, rp)[..., pad:].transpose(2, 3, 0, 1))
        outs[1].append(vt32.reshape(N_KV_HEADS, HEAD_DIM, bp, rp)[..., pad:].transpose(2, 3, 0, 1))
        outs[2].append(kit32.reshape(IDX_DIM, bp, rp)[..., pad:].transpose(1, 2, 0))
        outs[3].append(u.reshape(bp, rp, CONV_DIM)[:, rp - (CONV_WIDTH - 1):])

        qt, qit, wit, vt, kb, kib, kt32, vt32, kit32, u, cb = _in_proj(xs, g[0], w["wt"], w["wc"], cos_s, sin_s)
        k_new = kt32.T.reshape(bs, s_len, N_KV_HEADS, HEAD_DIM)
        v_new = vt32.T.reshape(bs, s_len, N_KV_HEADS, HEAD_DIM)
        ki_new = kit32.T.reshape(bs, s_len, IDX_DIM)
        k_all = jnp.concatenate([cache_k[l], k_new], axis=1).astype(bf16)
        v_all = jnp.concatenate([cache_v[l], v_new], axis=1).astype(bf16)
        ki_all = jnp.concatenate([cache_kidx[l], ki_new], axis=1).astype(bf16)
        o = _attention(
            _pad_lanes(qt, bs, s_len, LANES), _pad_lanes(qit, bs, s_len, LANES), _pad_lanes(wit, bs, s_len, LANES),
            _with_ones_column(k_all.transpose(2, 0, 1, 3).reshape(N_KV_HEADS, bs * ls, HEAD_DIM)),
            ki_all.reshape(bs * ls, IDX_DIM),
            v_all.reshape(bs, ls, KV_COLS).transpose(0, 2, 1),
            n_seq=bs, lq=LANES, lk=ls, tq=LANES, kb_rows=ls, causal=False, first_key=0, topk=topk_s)
        o = o.reshape(bs, LANES, Q_COLS)[:, :s_len].reshape(bs * s_len, Q_COLS)
        x1 = _mix_out(xs, o, u, _halo(u, bs, state_conv[l].astype(f32)), cb, g[0], g[1], w["cw"], w["wg"],
                      w["wao"], w["wco"], w["wo"])
        xs = _mlp(x1, g[2], g[3], w["wup"], w["wdn"])
        outs[4].append(k_new)
        outs[5].append(v_new)
        outs[6].append(ki_new)
        outs[7].append(u.reshape(bs, s_len, CONV_DIM)[:, s_len - (CONV_WIDTH - 1):])

    y_prompt = xp.reshape(bp, seq, D_MODEL)
    y_sample = xs.reshape(bs, s_len, D_MODEL)
    return (y_prompt, y_sample) + tuple(jnp.stack(o) for o in outs)
```

```python
import functools
import math

import jax
import jax.numpy as jnp
from jax import lax
from jax.experimental import pallas as pl
from jax.experimental.pallas import tpu as pltpu

D_MODEL = 1024
CHUNK = 64
CHUNK_SHIFT = CHUNK.bit_length() - 1
N_META = 16
HEAD_DIM = 64
N_HEADS = 8
N_KV_HEADS = 2
N_GROUPS = N_HEADS // N_KV_HEADS
IDX_HEADS = 4
IDX_DIM = 64
ROT_DIM = HEAD_DIM // 4
ROT_HALF = ROT_DIM // 2
ROPE_THETA = 500000.0
CONV_DIM = D_MODEL // 2
CONV_WIDTH = 3
D_FF = 4 * D_MODEL
TOPK_MAX = 256
EPS = 1e-6

Q_COLS = N_HEADS * HEAD_DIM
KV_COLS = N_KV_HEADS * HEAD_DIM
IQ_COLS = IDX_HEADS * IDX_DIM
END_Q = Q_COLS
END_K = END_Q + KV_COLS
END_V = END_K + KV_COLS
END_IQ = END_V + IQ_COLS
END_IK = END_IQ + IDX_DIM
END_IW = END_IK + IDX_HEADS
END_CB = END_IW + CONV_DIM
END_CC = END_CB + CONV_DIM
END_CX = END_CC + CONV_DIM

SUBLANES = 8
LANES = 128
BF16_ROWS = 16
IW_ROWS = SUBLANES
ZT_Q = 0
ZT_K = ZT_Q + Q_COLS
ZT_V = ZT_K + KV_COLS
ZT_IQ = ZT_V + KV_COLS
ZT_IK = ZT_IQ + IQ_COLS
ZT_IW = ZT_IK + IDX_DIM
ZT_ROWS = ZT_IW + IW_ROWS

ROW_TILE = 256
MLP_TILE = 512
MIX_TILE = 512
PROJ_TILE = 512
Q_TILE = 256
NEG_BIAS = -1e30
LOG2_E = math.log2(math.e)
KEY_LANES = 128
BOUND_MARGIN = 1.01
MIN_DENOMINATOR = 2.0 ** -60
INT_MIN = -(2 ** 31)
NEG_INF_KEY = INT_MIN + 0x7FFFFF
MIN_FINITE_KEY = NEG_INF_KEY + 1
POS_INF_KEY = 0x7F800000
ZERO_KEY = 0
TAKE_ALL = 2 ** 30
INTERP_STEPS = 20
STALL_STEPS = 2
SUM_CHAINS = 4
SEARCH_CAP = INTERP_STEPS + 34
SEARCH_UNROLL = 2
V7X_VMEM_BYTES = 64 * 1024 * 1024
VMEM_LIMIT = V7X_VMEM_BYTES - 8 * 1024 * 1024


def _rms(x, g):
    return x * lax.rsqrt(jnp.mean(x * x, axis=-1, keepdims=True) + EPS) * g


def _sigmoid(x):
    return 1.0 / (1.0 + jnp.exp(-x))


def _dot(a, b):
    return jnp.dot(a, b, preferred_element_type=jnp.float32)


def _dot_nt(a, b):
    return lax.dot_general(a, b, (((1,), (1,)), ((), ())), preferred_element_type=jnp.float32)


def _rope_rows(zt, n_heads, cos, sin):
    pieces = []
    for h in range(n_heads):
        o = h * HEAD_DIM
        x1 = zt[o:o + ROT_HALF]
        x2 = zt[o + ROT_HALF:o + ROT_DIM]
        pieces.append(x1 * cos - x2 * sin)
        pieces.append(x2 * cos + x1 * sin)
        pieces.append(zt[o + ROT_DIM:o + HEAD_DIM])
    return jnp.concatenate(pieces, axis=0)


def _in_proj_kernel(x_ref, g_ref, wt_ref, wc_ref, cos_ref, sin_ref,
                    qt_ref, qit_ref, wit_ref, vt_ref, kb_ref, kib_ref,
                    kt_ref, vt32_ref, kit_ref, u_ref, cb_ref):
    h = _rms(x_ref[...], g_ref[...]).astype(jnp.bfloat16)
    zt = _dot_nt(wt_ref[...], h)
    cos = cos_ref[...]
    sin = sin_ref[...]
    qt_ref[...] = (_rope_rows(zt[ZT_Q:ZT_K], N_HEADS, cos, sin) * LOG2_E).astype(jnp.bfloat16)
    kt = _rope_rows(zt[ZT_K:ZT_V], N_KV_HEADS, cos, sin)
    kt_ref[...] = kt
    k = kt.T
    tail = jnp.where(lax.broadcasted_iota(jnp.int32, (k.shape[0], KEY_LANES - HEAD_DIM), 1) == 0, 1.0, 0.0)
    for g in range(N_KV_HEADS):
        kb_ref[g] = jnp.concatenate([k[:, g * HEAD_DIM:(g + 1) * HEAD_DIM], tail], axis=1).astype(jnp.bfloat16)
    vt = zt[ZT_V:ZT_IQ]
    for i in range(vt_ref.shape[0]):
        vt_ref[i] = vt[:, i * ROW_TILE:(i + 1) * ROW_TILE].astype(jnp.bfloat16)
    vt32_ref[...] = vt
    qit_ref[...] = _rope_rows(zt[ZT_IQ:ZT_IK], IDX_HEADS, cos, sin).astype(jnp.bfloat16)
    kit = _rope_rows(zt[ZT_IK:ZT_IW], 1, cos, sin)
    kit_ref[...] = kit
    kib_ref[...] = kit.T.astype(jnp.bfloat16)
    wit_ref[...] = zt[ZT_IW:ZT_ROWS]
    zc = _dot(h, wc_ref[...])
    cb_ref[...] = zc[:, :CONV_DIM]
    u_ref[...] = zc[:, CONV_DIM:2 * CONV_DIM] * zc[:, 2 * CONV_DIM:]


def _in_proj(x, g0, wt, wc, cos_t, sin_t):
    n = x.shape[0]
    tm = PROJ_TILE if n % PROJ_TILE == 0 else ROW_TILE
    nb = n // tm
    f32, bf16 = jnp.float32, jnp.bfloat16
    full = lambda shape: pl.BlockSpec(shape, lambda i: (0,) * len(shape))
    rows = lambda w: pl.BlockSpec((tm, w), lambda i: (i, 0))
    cols = lambda r: pl.BlockSpec((r, tm), lambda i: (0, i))
    out_shape = (
        jax.ShapeDtypeStruct((Q_COLS, n), bf16),
        jax.ShapeDtypeStruct((IQ_COLS, n), bf16),
        jax.ShapeDtypeStruct((IW_ROWS, n), f32),
        jax.ShapeDtypeStruct((n // ROW_TILE, KV_COLS, ROW_TILE), bf16),
        jax.ShapeDtypeStruct((N_KV_HEADS, n, KEY_LANES), bf16),
        jax.ShapeDtypeStruct((n, IDX_DIM), bf16),
        jax.ShapeDtypeStruct((KV_COLS, n), f32),
        jax.ShapeDtypeStruct((KV_COLS, n), f32),
        jax.ShapeDtypeStruct((IDX_DIM, n), f32),
        jax.ShapeDtypeStruct((n, CONV_DIM), f32),
        jax.ShapeDtypeStruct((n, CONV_DIM), f32),
    )
    out_specs = (
        cols(Q_COLS), cols(IQ_COLS), cols(IW_ROWS),
        pl.BlockSpec((tm // ROW_TILE, KV_COLS, ROW_TILE), lambda i: (i, 0, 0)),
        pl.BlockSpec((N_KV_HEADS, tm, KEY_LANES), lambda i: (0, i, 0)),
        rows(IDX_DIM), cols(KV_COLS), cols(KV_COLS), cols(IDX_DIM), rows(CONV_DIM), rows(CONV_DIM),
    )
    return pl.pallas_call(
        _in_proj_kernel,
        grid=(nb,),
        in_specs=[rows(D_MODEL), full((1, D_MODEL)), full((ZT_ROWS, D_MODEL)), full((D_MODEL, 3 * CONV_DIM)),
                  cols(ROT_HALF), cols(ROT_HALF)],
        out_specs=out_specs,
        out_shape=out_shape,
        compiler_params=pltpu.CompilerParams(dimension_semantics=("arbitrary",), vmem_limit_bytes=VMEM_LIMIT),
        name="in_proj",
    )(x, g0, wt, wc, cos_t, sin_t)


def _attn_kernel(qt_ref, qit_ref, wit_ref, qit_next_ref, wit_next_ref, kb_ref, kib_ref, vt_ref, tri_ref, o_ref,
                 skey_ref, ot_ref, qg_ref, acc_ref, lg_ref, p_ref, kmax_ref, next_skey_ref, next_cpos_ref, *, tq, kb_rows, n_kblocks, count_blocks,
                 causal, first_key, topk):
    j = pl.program_id(1)
    nkb = (j + 1) if causal else n_kblocks
    score_ahead = causal
    i32 = jnp.int32

    q_row = j * tq + lax.broadcasted_iota(i32, (1, tq), 1)
    if causal:
        lim = jnp.where(q_row >= first_key, ((q_row >> CHUNK_SHIFT) << CHUNK_SHIFT) + CHUNK, 0)
    else:
        lim = jnp.full((1, tq), n_kblocks * kb_rows, i32)

    def key_rows(kb):
        return kb * kb_rows + lax.broadcasted_iota(i32, (kb_rows, 1), 0)

    def kslice(kb):
        return pl.ds(pl.multiple_of(kb * kb_rows, SUBLANES), kb_rows)

    def to_key(f):
        bits = lax.bitcast_convert_type(f, i32)
        return bits ^ ((bits >> 31) & 0x7FFFFFFF)

    def to_f32(k):
        return lax.bitcast_convert_type(k ^ ((k >> 31) & 0x7FFFFFFF), jnp.float32)

    def score_block(kb, edge, skip=0, ahead=False):
        q_ref, w_ref, dst_ref = (qit_next_ref, wit_next_ref, next_skey_ref) if ahead else (qit_ref, wit_ref, skey_ref)
        rows_here = pl.ds(pl.multiple_of(kb * kb_rows, SUBLANES) + skip, kb_rows - skip)
        ki = kib_ref[rows_here, :]
        acc = jnp.zeros((kb_rows - skip, tq), jnp.float32)
        for h in range(IDX_HEADS):
            s = _dot(ki, q_ref[h * IDX_DIM:(h + 1) * IDX_DIM, :])
            acc = acc + w_ref[h:h + 1, :] * jnp.maximum(s, 0.0)
        acc = jnp.where(acc == 0.0, 0.0, acc)
        if edge:
            rows = key_rows(kb)[skip:]
            adm = jnp.logical_and(rows >= first_key, rows < lim)
            sk = to_key(jnp.where(adm, acc, -jnp.inf))
            masked = jnp.where(adm, acc, 0.0)
            scale = (jnp.sum(masked * masked, axis=0, keepdims=True),
                     jnp.sum(jnp.where(adm, 1.0, 0.0), axis=0, keepdims=True))
        else:
            sk = to_key(acc)
            scale = None
        dst_ref[rows_here, :] = sk
        return jnp.sum(jnp.where(sk > ZERO_KEY, 1, 0), axis=0, keepdims=True), scale

    zero = jnp.zeros((1, tq), jnp.float32)
    izero = jnp.zeros((1, tq), i32)
    head_skip = first_key // BF16_ROWS * BF16_ROWS
    if head_skip:
        @pl.when(j == 0)
        def _():
            skey_ref[0:head_skip, :] = jnp.full((head_skip, tq), INT_MIN, i32)
    c_pos, (s2, s0) = score_block(0, True, head_skip)
    if score_ahead:
        def fetch(kb, carry):
            skey_ref[kslice(kb), :] = next_skey_ref[kslice(kb), :]
            return carry
        lax.fori_loop(1, nkb - 1, fetch, 0)
        c_pos = c_pos + jnp.where(j > 0, next_cpos_ref[0:1, :], 0)
        n_left = jnp.minimum(nkb - 1, 1)
    else:
        n_inner_pairs = (nkb - 2) // 2
        c_pos = lax.fori_loop(
            0, n_inner_pairs,
            lambda p, c: c + score_block(2 * p + 1, False)[0] + score_block(2 * p + 2, False)[0], c_pos)
        n_left = nkb - 1 - 2 * jnp.maximum(n_inner_pairs, 0)

    def last_block(c):
        c_l, (s2_l, s0_l) = score_block(nkb - 1, True)
        return c + c_l, s2 + s2_l, s0 + s0_l

    c_pos, s2, s0 = lax.cond(
        n_left == 2, lambda: last_block(c_pos + score_block(nkb - 2, False)[0]),
        lambda: lax.cond(n_left == 1, lambda: last_block(c_pos), lambda: (c_pos, s2, s0)))

    cslice_rows = count_blocks * kb_rows
    n_steps = (nkb + count_blocks - 1) // count_blocks
    if count_blocks > 1:
        skey_ref[kslice(nkb), :] = jnp.full((kb_rows, tq), INT_MIN, i32)

    assert cslice_rows % (SUM_CHAINS * SUBLANES) == 0

    def count_ge(cand):
        def body(c, acc):
            sk = skey_ref[pl.ds(pl.multiple_of(c * cslice_rows, SUBLANES), cslice_rows), :]
            ind = jnp.where(sk >= cand, 1, 0)
            return acc + jnp.sum(ind.reshape(SUM_CHAINS, -1, SUBLANES, tq), axis=1)
        acc = lax.fori_loop(0, n_steps, body, jnp.zeros((SUM_CHAINS, SUBLANES, tq), i32))
        return jnp.sum(acc.reshape(SUM_CHAINS * SUBLANES, tq), axis=0, keepdims=True)

    n_adm = jnp.maximum(lim - first_key, 0)
    n_f = jnp.maximum(n_adm, 1).astype(jnp.float32)
    sigma = jnp.sqrt(s2 / jnp.maximum(s0, 1.0))
    tail = jnp.clip(topk / n_f, 1e-6, 1.0 - 1e-6)
    tq_ = jnp.sqrt(-2.0 * jnp.log(jnp.minimum(tail, 1.0 - tail)))
    zq = tq_ - (2.515517 + 0.802853 * tq_ + 0.010328 * tq_ * tq_) / (
        1.0 + 1.432788 * tq_ + 0.189269 * tq_ * tq_ + 0.001308 * tq_ * tq_ * tq_)
    zq = jnp.where(tail < 0.5, zq, -zq)
    step_scale = 1.5 * sigma / jnp.maximum(jnp.abs(zq), 0.5)
    log_k = math.log(topk + 0.5)

    def search_cond(st):
        return jnp.logical_and(st[0] < SEARCH_CAP, st[1] > 0)

    def search_step(st):
        it, lo, hi, c_lo, c_hi, lo_real, hi_real, w_lo, w_hi, last, stall, done = st
        f_lo, f_hi = to_f32(lo), to_f32(hi)
        g_lo = jnp.log(c_lo.astype(jnp.float32)) - log_k
        g_hi = log_k - jnp.log(jnp.maximum(c_hi.astype(jnp.float32), 0.5))
        t_in = f_lo + (f_hi - f_lo) * (w_lo * g_lo / (w_lo * g_lo + w_hi * g_hi))
        t_up = f_lo + step_scale * g_lo
        t_dn = f_hi - step_scale * g_hi
        t = jnp.where(lo_real > 0, jnp.where(hi_real > 0, t_in, t_up), t_dn)
        cand = jnp.minimum(jnp.maximum(to_key(t), lo + 1), hi - 1)
        mid = (lo >> 1) + (hi >> 1) + (lo & hi & 1)
        cand = jnp.where(stall >= STALL_STEPS, mid, jnp.where(it >= INTERP_STEPS, mid, cand))
        cand = jnp.where(it == 0, jnp.where(below, ZERO_KEY, cand), cand)
        cnt = count_ge(cand)
        up = jnp.where(done > 0, 0, jnp.where(cnt >= topk, 1, 0))
        dn = jnp.where(done > 0, 0, jnp.where(cnt >= topk, 0, 1))
        both = lo_real * hi_real
        stall = stall + both * jnp.where(cnt == c_lo, 1, jnp.where(cnt == c_hi, 1, 0))
        lo = jnp.where(up > 0, cand, lo)
        c_lo = jnp.where(up > 0, cnt, c_lo)
        lo_real = jnp.maximum(lo_real, up)
        hi = jnp.where(dn > 0, cand, hi)
        c_hi = jnp.where(dn > 0, cnt, c_hi)
        hi_real = jnp.maximum(hi_real, dn)
        w_hi = jnp.where(up > 0, jnp.where(last > 0, 0.5 * w_hi, 1.0), jnp.where(dn > 0, 1.0, w_hi))
        w_lo = jnp.where(dn > 0, jnp.where(last < 0, 0.5 * w_lo, 1.0), jnp.where(up > 0, 1.0, w_lo))
        last = up - dn + (1 - up - dn) * last
        done = jnp.where(c_lo == topk, 1, jnp.where(hi == lo + 1, 1, done))
        return it + 1, lo, hi, c_lo, c_hi, lo_real, hi_real, w_lo, w_hi, last, stall, done

    def search_body(st):
        inner = (st[0],) + st[2:]
        for _ in range(SEARCH_UNROLL):
            inner = search_step(inner)
        return (inner[0], jnp.sum(1 - inner[-1])) + inner[1:]

    small = n_adm <= topk
    above = c_pos >= topk
    below = jnp.where(small, 0, jnp.where(above, 0, 1)) > 0
    lo0 = jnp.where(small, NEG_INF_KEY, jnp.where(above, ZERO_KEY + 1, MIN_FINITE_KEY))
    hi0 = jnp.where(small, NEG_INF_KEY + 1, jnp.where(above, POS_INF_KEY, ZERO_KEY + 1))
    c_lo0 = jnp.where(small, topk, jnp.where(above, c_pos, n_adm))
    c_hi0 = jnp.where(small, 0, jnp.where(above, 0, c_pos))
    lo_real0 = jnp.where(small, 0, jnp.where(above, 1, 0))
    hi_real0 = jnp.where(small, 0, jnp.where(above, 0, 1))
    done0 = jnp.where(c_lo0 == topk, 1, jnp.where(hi0 == lo0 + 1, 1, 0))
    one = jnp.ones((1, tq), jnp.float32)
    st = lax.while_loop(search_cond, search_body,
                        (i32(0), jnp.sum(1 - done0), lo0, hi0, c_lo0, c_hi0, lo_real0, hi_real0, one, one,
                         izero, izero, done0))
    thr, hi, c_hi = st[2], st[3], st[5]
    need = jnp.where(small, 0, jnp.where(hi == thr + 1, topk - c_hi, TAKE_ALL)).astype(jnp.float32)

    def block_bias(kb, run):
        sk = skey_ref[kslice(kb), :]
        tie = jnp.where(sk == thr, 1.0, 0.0)
        rank = _dot(tri_ref[...], tie.astype(jnp.bfloat16)) + run
        tie_bias = jnp.where(rank <= need, 0.0, NEG_BIAS)
        bias = jnp.where(sk > thr, 0.0, jnp.where(sk == thr, tie_bias, NEG_BIAS))
        return bias, rank[kb_rows - 1:kb_rows, :]

    gw = N_GROUPS * tq
    groups = range(N_KV_HEADS)
    lane = lax.broadcasted_iota(i32, (1, LANES), 1)

    @pl.when(j == 0)
    def _():
        def norm_body(kb, best):
            out = []
            for g in groups:
                k = jnp.where(lane < HEAD_DIM, kb_ref[g, kslice(kb), :].astype(jnp.float32), 0.0)
                n2 = jnp.max(jnp.sum(k * k, axis=1, keepdims=True), axis=0, keepdims=True)
                out.append(jnp.maximum(best[g], n2))
            return tuple(out)
        best = lax.fori_loop(0, n_kblocks, norm_body, (jnp.zeros((1, 1), jnp.float32),) * N_KV_HEADS)
        for g in groups:
            kmax_ref[g] = jnp.broadcast_to(best[g], (SUBLANES, LANES))

    pad_row = lax.broadcasted_iota(i32, (KEY_LANES - HEAD_DIM, 1), 0)
    for g in groups:
        q = jnp.concatenate([qt_ref[h * HEAD_DIM:(h + 1) * HEAD_DIM, :]
                             for h in range(g * N_GROUPS, (g + 1) * N_GROUPS)], axis=1)
        qf = q.astype(jnp.float32)
        bound = jnp.sqrt(jnp.sum(qf * qf, axis=0, keepdims=True) * kmax_ref[g][0:1, 0:1]) * BOUND_MARGIN
        pad = jnp.where(pad_row == 0, -bound, 0.0).astype(jnp.bfloat16)
        qg_ref[g] = jnp.concatenate([q, pad], axis=0)

    def value_rows(kb, g):
        ones = jnp.ones((SUBLANES, kb_rows), jnp.bfloat16)
        return jnp.concatenate([vt_ref[kb, g * HEAD_DIM:(g + 1) * HEAD_DIM, :], ones], axis=0)

    def pipelined(logits_stage, value_stage, ms, ahead):
        def step(kb, slot, ms, staged, prefetch):
            ms, cpos = ms
            if ahead:
                cpos = cpos + jnp.where(kb > 0, score_block(kb, False, ahead=True)[0], 0)
            extra, run = staged
            nxt = logits_stage(kb + 1, 1 - slot, run) if prefetch else None
            return (tuple(value_stage(kb, g, slot, ms[g], extra[g]) for g in groups), cpos), nxt

        def pair_body(i, carry):
            ms, staged = carry
            ms, staged = step(2 * i, 0, ms, staged, True)
            return step(2 * i + 1, 1, ms, staged, True)

        n_pairs = (nkb - 1) // 2
        ms, staged = lax.fori_loop(0, n_pairs, pair_body, ((ms, izero), logits_stage(0, 0, zero)))
        last = 2 * n_pairs

        def two_left():
            ms1, staged1 = step(last, 0, ms, staged, True)
            return step(last + 1, 1, ms1, staged1, False)[0]

        ms, cpos = lax.cond(nkb - last == 2, two_left, lambda: step(last, 0, ms, staged, False)[0])
        if ahead:
            next_cpos_ref[...] = jnp.broadcast_to(cpos, next_cpos_ref.shape)
        return ms

    def logits(kb, g, bias):
        return _dot(kb_ref[g, kslice(kb), :], qg_ref[g]) + bias

    def bounded_logits_stage(kb, slot, run):
        bias, run = block_bias(kb, run)
        bias = jnp.concatenate([bias] * N_GROUPS, axis=1)
        for g in groups:
            p_ref[slot, g] = jnp.exp2(logits(kb, g, bias)).astype(jnp.bfloat16)
        return (izero,) * N_KV_HEADS, run

    def bounded_value_stage(kb, g, slot, m, unused):
        acc_ref[g] = acc_ref[g] + _dot(value_rows(kb, g), p_ref[slot, g])
        return m

    def online_logits_stage(kb, slot, run):
        bias, run = block_bias(kb, run)
        bias = jnp.concatenate([bias] * N_GROUPS, axis=1)
        m_blks = []
        for g in groups:
            lg = logits(kb, g, bias)
            lg_ref[slot, g] = lg
            m_blks.append(jnp.max(lg, axis=0, keepdims=True))
        return tuple(m_blks), run

    def online_value_stage(kb, g, slot, m, m_blk):
        m_new = jnp.maximum(m, m_blk)
        p = jnp.exp2(lg_ref[slot, g] - m_new).astype(jnp.bfloat16)
        acc_ref[g] = jnp.exp2(m - m_new) * acc_ref[g] + _dot(value_rows(kb, g), p)
        return m_new

    has_key = jnp.concatenate([n_adm] * N_GROUPS, axis=1) > 0
    acc_ref[...] = jnp.zeros_like(acc_ref)
    pipelined(bounded_logits_stage, bounded_value_stage, (izero,) * N_KV_HEADS, score_ahead)
    underflow = izero[:, 0:1]
    for g in groups:
        denom = acc_ref[g][HEAD_DIM:HEAD_DIM + 1, :]
        underflow = underflow + jnp.sum(jnp.where(has_key, jnp.where(denom < MIN_DENOMINATOR, 1, 0), 0),
                                        axis=1, keepdims=True)

    @pl.when(underflow[0, 0] > 0)
    def _():
        acc_ref[...] = jnp.zeros_like(acc_ref)
        for g in groups:
            qg_ref[g, HEAD_DIM:, :] = jnp.zeros((KEY_LANES - HEAD_DIM, gw), jnp.bfloat16)
        pipelined(online_logits_stage, online_value_stage,
                  (jnp.full((1, gw), NEG_BIAS, jnp.float32),) * N_KV_HEADS, False)

    for g in groups:
        acc = acc_ref[g]
        og = jnp.where(has_key, acc[:HEAD_DIM] / acc[HEAD_DIM:HEAD_DIM + 1], 0.0)
        for hh in range(N_GROUPS):
            h = g * N_GROUPS + hh
            ot_ref[h * HEAD_DIM:(h + 1) * HEAD_DIM, :] = og[:, hh * tq:(hh + 1) * tq]

    o_ref[...] = ot_ref[...].T.astype(o_ref.dtype)


def _attention(qt, qit, wit, kb, kib, vt, *, n_seq, lq, lk, tq, kb_rows, causal, first_key, topk):
    nq = lq // tq
    n_kblocks = lk // kb_rows
    tri = jnp.tril(jnp.ones((kb_rows, kb_rows), jnp.bfloat16))
    count_blocks = 2 if causal else 1
    kernel = functools.partial(_attn_kernel, tq=tq, kb_rows=kb_rows, n_kblocks=n_kblocks,
                               count_blocks=count_blocks, causal=causal, first_key=first_key, topk=topk)
    return pl.pallas_call(
        kernel,
        grid=(n_seq, nq),
        in_specs=[
            pl.BlockSpec((Q_COLS, tq), lambda b, j: (0, b * nq + j)),
            pl.BlockSpec((IQ_COLS, tq), lambda b, j: (0, b * nq + j)),
            pl.BlockSpec((IW_ROWS, tq), lambda b, j: (0, b * nq + j)),
            pl.BlockSpec((IQ_COLS, tq), lambda b, j: (0, b * nq + jnp.minimum(j + 1, nq - 1))),
            pl.BlockSpec((IW_ROWS, tq), lambda b, j: (0, b * nq + jnp.minimum(j + 1, nq - 1))),
            pl.BlockSpec((N_KV_HEADS, lk, KEY_LANES), lambda b, j: (0, b, 0)),
            pl.BlockSpec((lk, IDX_DIM), lambda b, j: (b, 0)),
            pl.BlockSpec((n_kblocks, KV_COLS, kb_rows), lambda b, j: (b, 0, 0)),
            pl.BlockSpec((kb_rows, kb_rows), lambda b, j: (0, 0)),
        ],
        out_specs=pl.BlockSpec((tq, Q_COLS), lambda b, j: (b * nq + j, 0)),
        out_shape=jax.ShapeDtypeStruct((n_seq * lq, Q_COLS), jnp.bfloat16),
        scratch_shapes=[
            pltpu.VMEM((lk + (count_blocks - 1) * kb_rows, tq), jnp.int32),
            pltpu.VMEM((Q_COLS, tq), jnp.float32),
            pltpu.VMEM((N_KV_HEADS, KEY_LANES, N_GROUPS * tq), jnp.bfloat16),
            pltpu.VMEM((N_KV_HEADS, HEAD_DIM + SUBLANES, N_GROUPS * tq), jnp.float32),
            pltpu.VMEM((2, N_KV_HEADS, kb_rows, N_GROUPS * tq), jnp.float32),
            pltpu.VMEM((2, N_KV_HEADS, kb_rows, N_GROUPS * tq), jnp.bfloat16),
            pltpu.VMEM((N_KV_HEADS, SUBLANES, LANES), jnp.float32),
            pltpu.VMEM((lk if causal else SUBLANES, tq), jnp.int32),
            pltpu.VMEM((SUBLANES, tq), jnp.int32),
        ],
        compiler_params=pltpu.CompilerParams(dimension_semantics=("arbitrary", "arbitrary"),
                                             vmem_limit_bytes=VMEM_LIMIT),
        name="dsa_attention_causal" if causal else "dsa_attention_full",
    )(qt, qit, wit, qit, wit, kb, kib, vt, tri)


def _mix_out_kernel(x_ref, o_ref, u_ref, halo_ref, cb_ref, g0_ref, g1_ref, cw_ref,
                    wg_ref, wao_ref, wco_ref, wo_ref, x1_ref, *, n_seg):
    x = x_ref[...]
    h = _rms(x, g0_ref[...]).astype(jnp.bfloat16)
    gates = _dot(h, wg_ref[...])
    y_a = _dot(o_ref[...], wao_ref[...])
    u = u_ref[...]
    halo = halo_ref[0]
    t = u.shape[0]
    seg = t // n_seg
    row = lax.broadcasted_iota(jnp.int32, (t, 1), 0)
    u1 = pltpu.roll(u, 1, 0)
    u2 = pltpu.roll(u, 2, 0)
    for s in range(n_seg):
        h6 = halo[s * SUBLANES + 6:s * SUBLANES + 7]
        h7 = halo[s * SUBLANES + 7:s * SUBLANES + 8]
        u1 = jnp.where(row == s * seg, h7, u1)
        u2 = jnp.where(row == s * seg, h6, jnp.where(row == s * seg + 1, h7, u2))
    conv = cw_ref[0:1] * u2 + cw_ref[1:2] * u1 + cw_ref[2:3] * u
    y_b = _dot((cb_ref[...] * conv).astype(jnp.bfloat16), wco_ref[...])
    m = _sigmoid(gates[:, :D_MODEL]) * y_a + _sigmoid(gates[:, D_MODEL:]) * y_b
    a = _dot(m.astype(jnp.bfloat16), wo_ref[...])
    x1_ref[...] = x + _rms(a, g1_ref[...])


def _mix_out(x, o, u, halo, cb, g0, g1, cw, wg, wao, wco, wo):
    n = x.shape[0]
    tm = MIX_TILE if n % MIX_TILE == 0 else ROW_TILE
    halo = halo.reshape(n // tm, -1, CONV_DIM)
    n_seg = halo.shape[1] // SUBLANES
    full = lambda shape: pl.BlockSpec(shape, lambda i: (0,) * len(shape))
    resident = lambda shape: pl.BlockSpec(shape, lambda i: (0,) * len(shape), pipeline_mode=pl.Buffered(1))
    rows = lambda w: pl.BlockSpec((tm, w), lambda i: (i, 0))
    return pl.pallas_call(
        functools.partial(_mix_out_kernel, n_seg=n_seg),
        grid=(n // tm,),
        in_specs=[rows(D_MODEL), rows(Q_COLS), rows(CONV_DIM),
                  pl.BlockSpec((1, n_seg * SUBLANES, CONV_DIM), lambda i: (i, 0, 0)), rows(CONV_DIM),
                  full((1, D_MODEL)), full((1, D_MODEL)), full((CONV_WIDTH, CONV_DIM)),
                  resident((D_MODEL, 2 * D_MODEL)), resident((Q_COLS, D_MODEL)), resident((CONV_DIM, D_MODEL)),
                  resident((D_MODEL, D_MODEL))],
        out_specs=rows(D_MODEL),
        out_shape=jax.ShapeDtypeStruct((n, D_MODEL), jnp.float32),
        compiler_params=pltpu.CompilerParams(dimension_semantics=("arbitrary",), vmem_limit_bytes=VMEM_LIMIT),
        name="mix_out",
    )(x, o, u, halo, cb, g0, g1, cw, wg, wao, wco, wo)


def _mlp_kernel(x_ref, g2_ref, g3_ref, wup_ref, wdn_ref, x2_ref):
    x = x_ref[...]
    h = _rms(x, g2_ref[...]).astype(jnp.bfloat16)
    up = jnp.maximum(_dot(h, wup_ref[...]), 0.0)
    f = _dot((up * up).astype(jnp.bfloat16), wdn_ref[...])
    x2_ref[...] = x + _rms(f, g3_ref[...])


def _mlp(x, g2, g3, wup, wdn, drop_head_tiles_of=None):
    n = x.shape[0]
    tm = ROW_TILE if drop_head_tiles_of is not None or n % MLP_TILE else MLP_TILE
    full = lambda shape: pl.BlockSpec(shape, lambda i: (0,) * len(shape))
    resident = lambda shape: pl.BlockSpec(shape, lambda i: (0,) * len(shape), pipeline_mode=pl.Buffered(1))
    rows = lambda w: pl.BlockSpec((tm, w), lambda i: (i, 0))
    if drop_head_tiles_of is None:
        out_rows, out_spec = n, rows(D_MODEL)
    else:
        tps = drop_head_tiles_of
        out_rows = n // tps * (tps - 1)
        out_spec = pl.BlockSpec((tm, D_MODEL), lambda i: (i // tps * (tps - 1) + jnp.maximum(i % tps - 1, 0), 0))
    return pl.pallas_call(
        _mlp_kernel,
        grid=(n // tm,),
        in_specs=[rows(D_MODEL), full((1, D_MODEL)), full((1, D_MODEL)), resident((D_MODEL, D_FF)),
                  resident((D_FF, D_MODEL))],
        out_specs=out_spec,
        out_shape=jax.ShapeDtypeStruct((out_rows, D_MODEL), jnp.float32),
        compiler_params=pltpu.CompilerParams(dimension_semantics=("arbitrary",), vmem_limit_bytes=VMEM_LIMIT),
        name="mlp",
    )(x, g2, g3, wup, wdn)


def _prepare_weights(w_in, conv_w, w_attn_out, w_conv_out, w_o, w_up, w_down):
    bf16 = jnp.bfloat16
    wt = jnp.concatenate([
        w_in[:, :, :END_Q] * (HEAD_DIM ** -0.5),
        w_in[:, :, END_Q:END_V],
        w_in[:, :, END_V:END_IQ] * (IDX_DIM ** -0.5),
        w_in[:, :, END_IQ:END_IK],
        w_in[:, :, END_IK:END_IW] * (IDX_HEADS ** -0.5),
        jnp.zeros(w_in.shape[:2] + (IW_ROWS - IDX_HEADS,), w_in.dtype),
    ], axis=2).astype(bf16).transpose(0, 2, 1)
    return dict(
        wt=wt,
        wc=w_in[:, :, END_IW:END_CX].astype(bf16),
        wg=w_in[:, :, END_CX:].astype(bf16),
        cw=conv_w,
        wao=w_attn_out.astype(bf16), wco=w_conv_out.astype(bf16), wo=w_o.astype(bf16),
        wup=w_up.astype(bf16), wdn=w_down.astype(bf16))


def _rope_tables(pos):
    inv = jnp.exp(-math.log(ROPE_THETA) * jnp.arange(ROT_HALF, dtype=jnp.float32) * (2.0 / ROT_DIM))
    ang = inv[:, None] * pos.astype(jnp.float32)[None, :]
    return jnp.cos(ang), jnp.sin(ang)


def _halo(u, n_seq, past):
    n = u.shape[0]
    per_seq = n // n_seq
    first = jnp.concatenate([jnp.zeros((n_seq, SUBLANES - (CONV_WIDTH - 1), CONV_DIM), u.dtype), past], axis=1)
    if per_seq >= ROW_TILE:
        tiles = per_seq // ROW_TILE
        tails = u.reshape(n_seq, tiles, ROW_TILE, CONV_DIM)[:, :tiles - 1, ROW_TILE - SUBLANES:]
        return jnp.concatenate([first[:, None], tails], axis=1).reshape(n_seq * tiles, SUBLANES, CONV_DIM)
    n_seg = ROW_TILE // per_seq
    return first.reshape(n_seq // n_seg, n_seg * SUBLANES, CONV_DIM)


def _with_ones_column(k):
    tail = jnp.zeros(k.shape[:-1] + (KEY_LANES - HEAD_DIM,), k.dtype).at[..., 0].set(1)
    return jnp.concatenate([k, tail], axis=-1)


def _pad_lanes(a, n_seq, per_seq, width):
    a = a.reshape(a.shape[0], n_seq, per_seq)
    a = jnp.pad(a, ((0, 0), (0, 0), (0, width - per_seq)))
    return a.reshape(a.shape[0], n_seq * width)


def kernel(x_prompt, x_sample, cache_k, cache_v, cache_kidx, state_conv, meta_tokens, norm_gains, w_in, conv_w,
           w_attn_out, w_conv_out, w_o, w_up, w_down):
    f32, bf16 = jnp.float32, jnp.bfloat16
    depth = w_in.shape[0]
    bp, seq, _ = x_prompt.shape
    bs, s_len, _ = x_sample.shape
    past_len = cache_k.shape[2]
    assert seq % Q_TILE == 0 and Q_TILE % CHUNK == 0 and ROW_TILE == Q_TILE and N_META <= Q_TILE
    assert (bs * s_len) % ROW_TILE == 0 and ROW_TILE % s_len == 0
    assert CONV_WIDTH - 1 <= s_len <= LANES and (past_len + s_len) % SUBLANES == 0

    pad = Q_TILE - N_META
    rp = Q_TILE + seq
    xp = jnp.concatenate([
        jnp.zeros((bp, pad, D_MODEL), f32),
        jnp.broadcast_to(meta_tokens[None].astype(f32), (bp, N_META, D_MODEL)),
        x_prompt.astype(f32)], axis=1).reshape(bp * rp, D_MODEL)
    cos_p, sin_p = _rope_tables(jnp.tile(jnp.arange(rp, dtype=jnp.int32) - pad, bp))
    topk_p = min(TOPK_MAX, seq // 4)

    xs = x_sample.astype(f32).reshape(bs * s_len, D_MODEL)
    ls = past_len + s_len
    cos_s, sin_s = _rope_tables(jnp.tile(past_len + jnp.arange(s_len, dtype=jnp.int32), bs))
    topk_s = min(TOPK_MAX, ls // 4)
    conv_zero = jnp.zeros((bp, CONV_WIDTH - 1, CONV_DIM), f32)

    weights = _prepare_weights(w_in, conv_w, w_attn_out, w_conv_out, w_o, w_up, w_down)
    outs = [[] for _ in range(8)]
    for l in range(depth):
        w = {name: value[l] for name, value in weights.items()}
        g = [norm_gains[l, i][None, :] for i in range(4)]

        qt, qit, wit, vt, kb, kib, kt32, vt32, kit32, u, cb = _in_proj(xp, g[0], w["wt"], w["wc"], cos_p, sin_p)
        o = _attention(qt, qit, wit, kb, kib, vt, n_seq=bp, lq=rp, lk=rp, tq=Q_TILE, kb_rows=ROW_TILE,
                       causal=True, first_key=pad, topk=topk_p)
        x1 = _mix_out(xp, o, u, _halo(u, bp, conv_zero), cb, g[0], g[1], w["cw"], w["wg"], w["wao"], w["wco"],
                      w["wo"])
        xp = _mlp(x1, g[2], g[3], w["wup"], w["wdn"], drop_head_tiles_of=rp // ROW_TILE if l == depth - 1 else None)
        outs[0].append(kt32.reshape(N_KV_HEADS, HEAD_DIM, bp, rp)[..., pad:].transpose(2, 3, 0, 1))
        outs[1].append(vt32.reshape(N_KV_HEADS, HEAD_DIM, bp, rp)[..., pad:].transpose(2, 3, 0, 1))
        outs[2].append(kit32.reshape(IDX_DIM, bp, rp)[..., pad:].transpose(1, 2, 0))
        outs[3].append(u.reshape(bp, rp, CONV_DIM)[:, rp - (CONV_WIDTH - 1):])

        qt, qit, wit, vt, kb, kib, kt32, vt32, kit32, u, cb = _in_proj(xs, g[0], w["wt"], w["wc"], cos_s, sin_s)
        k_new = kt32.T.reshape(bs, s_len, N_KV_HEADS, HEAD_DIM)
        v_new = vt32.T.reshape(bs, s_len, N_KV_HEADS, HEAD_DIM)
        ki_new = kit32.T.reshape(bs, s_len, IDX_DIM)
        k_all = jnp.concatenate([cache_k[l], k_new], axis=1).astype(bf16)
        v_all = jnp.concatenate([cache_v[l], v_new], axis=1).astype(bf16)
        ki_all = jnp.concatenate([cache_kidx[l], ki_new], axis=1).astype(bf16)
        o = _attention(
            _pad_lanes(qt, bs, s_len, LANES), _pad_lanes(qit, bs, s_len, LANES), _pad_lanes(wit, bs, s_len, LANES),
            _with_ones_column(k_all.transpose(2, 0, 1, 3).reshape(N_KV_HEADS, bs * ls, HEAD_DIM)),
            ki_all.reshape(bs * ls, IDX_DIM),
            v_all.reshape(bs, ls, KV_COLS).transpose(0, 2, 1),
            n_seq=bs, lq=LANES, lk=ls, tq=LANES, kb_rows=ls, causal=False, first_key=0, topk=topk_s)
        o = o.reshape(bs, LANES, Q_COLS)[:, :s_len].reshape(bs * s_len, Q_COLS)
        x1 = _mix_out(xs, o, u, _halo(u, bs, state_conv[l].astype(f32)), cb, g[0], g[1], w["cw"], w["wg"],
                      w["wao"], w["wco"], w["wo"])
        xs = _mlp(x1, g[2], g[3], w["wup"], w["wdn"])
        outs[4].append(k_new)
        outs[5].append(v_new)
        outs[6].append(ki_new)
        outs[7].append(u.reshape(bs, s_len, CONV_DIM)[:, s_len - (CONV_WIDTH - 1):])

    y_prompt = xp.reshape(bp, seq, D_MODEL)
    y_sample = xs.reshape(bs, s_len, D_MODEL)
    return (y_prompt, y_sample) + tuple(jnp.stack(o) for o in outs)
```

```python
import functools
import math

import jax
import jax.numpy as jnp
from jax import lax
from jax.experimental import pallas as pl
from jax.experimental.pallas import tpu as pltpu

D_MODEL = 1024
CHUNK = 64
CHUNK_SHIFT = CHUNK.bit_length() - 1
N_META = 16
HEAD_DIM = 64
N_HEADS = 8
N_KV_HEADS = 2
N_GROUPS = N_HEADS // N_KV_HEADS
IDX_HEADS = 4
IDX_DIM = 64
ROT_DIM = HEAD_DIM // 4
ROT_HALF = ROT_DIM // 2
ROPE_THETA = 500000.0
CONV_DIM = D_MODEL // 2
CONV_WIDTH = 3
D_FF = 4 * D_MODEL
TOPK_MAX = 256
EPS = 1e-6

Q_COLS = N_HEADS * HEAD_DIM
KV_COLS = N_KV_HEADS * HEAD_DIM
IQ_COLS = IDX_HEADS * IDX_DIM
END_Q = Q_COLS
END_K = END_Q + KV_COLS
END_V = END_K + KV_COLS
END_IQ = END_V + IQ_COLS
END_IK = END_IQ + IDX_DIM
END_IW = END_IK + IDX_HEADS
END_CB = END_IW + CONV_DIM
END_CC = END_CB + CONV_DIM
END_CX = END_CC + CONV_DIM

SUBLANES = 8
LANES = 128
BF16_ROWS = 16
IW_ROWS = SUBLANES
ZT_Q = 0
ZT_K = ZT_Q + Q_COLS
ZT_V = ZT_K + KV_COLS
ZT_IQ = ZT_V + KV_COLS
ZT_IK = ZT_IQ + IQ_COLS
ZT_IW = ZT_IK + IDX_DIM
ZT_ROWS = ZT_IW + IW_ROWS

ROW_TILE = 256
MLP_TILE = 512
MIX_TILE = 512
PROJ_TILE = 512
Q_TILE = 256
NEG_BIAS = -1e30
LOG2_E = math.log2(math.e)
KEY_LANES = 128
BOUND_MARGIN = 1.01
MIN_DENOMINATOR = 2.0 ** -60
INT_MIN = -(2 ** 31)
NEG_INF_KEY = INT_MIN + 0x7FFFFF
MIN_FINITE_KEY = NEG_INF_KEY + 1
POS_INF_KEY = 0x7F800000
ZERO_KEY = 0
TAKE_ALL = 2 ** 30
INTERP_STEPS = 20
STALL_STEPS = 2
SUM_CHAINS = 4
SEARCH_CAP = INTERP_STEPS + 34
SEARCH_UNROLL = 2
V7X_VMEM_BYTES = 64 * 1024 * 1024
VMEM_LIMIT = V7X_VMEM_BYTES - 8 * 1024 * 1024


def _rms(x, g):
    return x * lax.rsqrt(jnp.mean(x * x, axis=-1, keepdims=True) + EPS) * g


def _sigmoid(x):
    return 1.0 / (1.0 + jnp.exp(-x))


def _dot(a, b):
    return jnp.dot(a, b, preferred_element_type=jnp.float32)


def _dot_nt(a, b):
    return lax.dot_general(a, b, (((1,), (1,)), ((), ())), preferred_element_type=jnp.float32)


def _rope_rows(zt, n_heads, cos, sin):
    pieces = []
    for h in range(n_heads):
        o = h * HEAD_DIM
        x1 = zt[o:o + ROT_HALF]
        x2 = zt[o + ROT_HALF:o + ROT_DIM]
        pieces.append(x1 * cos - x2 * sin)
        pieces.append(x2 * cos + x1 * sin)
        pieces.append(zt[o + ROT_DIM:o + HEAD_DIM])
    return jnp.concatenate(pieces, axis=0)


def _in_proj_kernel(x_ref, g_ref, wt_ref, wc_ref, cos_ref, sin_ref,
                    qt_ref, qit_ref, wit_ref, vt_ref, kb_ref, kib_ref,
                    kt_ref, vt32_ref, kit_ref, u_ref, cb_ref):
    h = _rms(x_ref[...], g_ref[...]).astype(jnp.bfloat16)
    zt = _dot_nt(wt_ref[...], h)
    cos = cos_ref[...]
    sin = sin_ref[...]
    qt_ref[...] = (_rope_rows(zt[ZT_Q:ZT_K], N_HEADS, cos, sin) * LOG2_E).astype(jnp.bfloat16)
    kt = _rope_rows(zt[ZT_K:ZT_V], N_KV_HEADS, cos, sin)
    kt_ref[...] = kt
    k = kt.T
    tail = jnp.where(lax.broadcasted_iota(jnp.int32, (k.shape[0], KEY_LANES - HEAD_DIM), 1) == 0, 1.0, 0.0)
    for g in range(N_KV_HEADS):
        kb_ref[g] = jnp.concatenate([k[:, g * HEAD_DIM:(g + 1) * HEAD_DIM], tail], axis=1).astype(jnp.bfloat16)
    vt = zt[ZT_V:ZT_IQ]
    for i in range(vt_ref.shape[0]):
        vt_ref[i] = vt[:, i * ROW_TILE:(i + 1) * ROW_TILE].astype(jnp.bfloat16)
    vt32_ref[...] = vt
    qit_ref[...] = _rope_rows(zt[ZT_IQ:ZT_IK], IDX_HEADS, cos, sin).astype(jnp.bfloat16)
    kit = _rope_rows(zt[ZT_IK:ZT_IW], 1, cos, sin)
    kit_ref[...] = kit
    kib_ref[...] = kit.T.astype(jnp.bfloat16)
    wit_ref[...] = zt[ZT_IW:ZT_ROWS]
    zc = _dot(h, wc_ref[...])
    cb_ref[...] = zc[:, :CONV_DIM]
    u_ref[...] = zc[:, CONV_DIM:2 * CONV_DIM] * zc[:, 2 * CONV_DIM:]


def _in_proj(x, g0, wt, wc, cos_t, sin_t):
    n = x.shape[0]
    tm = PROJ_TILE if n % PROJ_TILE == 0 else ROW_TILE
    nb = n // tm
    f32, bf16 = jnp.float32, jnp.bfloat16
    full = lambda shape: pl.BlockSpec(shape, lambda i: (0,) * len(shape))
    rows = lambda w: pl.BlockSpec((tm, w), lambda i: (i, 0))
    cols = lambda r: pl.BlockSpec((r, tm), lambda i: (0, i))
    out_shape = (
        jax.ShapeDtypeStruct((Q_COLS, n), bf16),
        jax.ShapeDtypeStruct((IQ_COLS, n), bf16),
        jax.ShapeDtypeStruct((IW_ROWS, n), f32),
        jax.ShapeDtypeStruct((n // ROW_TILE, KV_COLS, ROW_TILE), bf16),
        jax.ShapeDtypeStruct((N_KV_HEADS, n, KEY_LANES), bf16),
        jax.ShapeDtypeStruct((n, IDX_DIM), bf16),
        jax.ShapeDtypeStruct((KV_COLS, n), f32),
        jax.ShapeDtypeStruct((KV_COLS, n), f32),
        jax.ShapeDtypeStruct((IDX_DIM, n), f32),
        jax.ShapeDtypeStruct((n, CONV_DIM), f32),
        jax.ShapeDtypeStruct((n, CONV_DIM), f32),
    )
    out_specs = (
        cols(Q_COLS), cols(IQ_COLS), cols(IW_ROWS),
        pl.BlockSpec((tm // ROW_TILE, KV_COLS, ROW_TILE), lambda i: (i, 0, 0)),
        pl.BlockSpec((N_KV_HEADS, tm, KEY_LANES), lambda i: (0, i, 0)),
        rows(IDX_DIM), cols(KV_COLS), cols(KV_COLS), cols(IDX_DIM), rows(CONV_DIM), rows(CONV_DIM),
    )
    return pl.pallas_call(
        _in_proj_kernel,
        grid=(nb,),
        in_specs=[rows(D_MODEL), full((1, D_MODEL)), full((ZT_ROWS, D_MODEL)), full((D_MODEL, 3 * CONV_DIM)),
                  cols(ROT_HALF), cols(ROT_HALF)],
        out_specs=out_specs,
        out_shape=out_shape,
        compiler_params=pltpu.CompilerParams(dimension_semantics=("arbitrary",), vmem_limit_bytes=VMEM_LIMIT),
        name="in_proj",
    )(x, g0, wt, wc, cos_t, sin_t)


def _attn_kernel(qt_ref, qit_ref, wit_ref, kb_ref, kib_ref, vt_ref, tri_ref, o_ref,
                 skey_ref, ot_ref, qg_ref, acc_ref, lg_ref, p_ref, kmax_ref, *, tq, kb_rows, n_kblocks, count_blocks,
                 causal, first_key, topk):
    j = pl.program_id(1)
    nkb = (j + 1) if causal else n_kblocks
    i32 = jnp.int32

    q_row = j * tq + lax.broadcasted_iota(i32, (1, tq), 1)
    if causal:
        lim = jnp.where(q_row >= first_key, ((q_row >> CHUNK_SHIFT) << CHUNK_SHIFT) + CHUNK, 0)
    else:
        lim = jnp.full((1, tq), n_kblocks * kb_rows, i32)

    def key_rows(kb):
        return kb * kb_rows + lax.broadcasted_iota(i32, (kb_rows, 1), 0)

    def kslice(kb):
        return pl.ds(pl.multiple_of(kb * kb_rows, SUBLANES), kb_rows)

    def to_key(f):
        bits = lax.bitcast_convert_type(f, i32)
        return bits ^ ((bits >> 31) & 0x7FFFFFFF)

    def to_f32(k):
        return lax.bitcast_convert_type(k ^ ((k >> 31) & 0x7FFFFFFF), jnp.float32)

    def score_block(kb, edge, skip=0):
        rows_here = pl.ds(pl.multiple_of(kb * kb_rows, SUBLANES) + skip, kb_rows - skip)
        ki = kib_ref[rows_here, :]
        acc = jnp.zeros((kb_rows - skip, tq), jnp.float32)
        for h in range(IDX_HEADS):
            s = _dot(ki, qit_ref[h * IDX_DIM:(h + 1) * IDX_DIM, :])
            acc = acc + wit_ref[h:h + 1, :] * jnp.maximum(s, 0.0)
        acc = jnp.where(acc == 0.0, 0.0, acc)
        if edge:
            rows = key_rows(kb)[skip:]
            adm = jnp.logical_and(rows >= first_key, rows < lim)
            sk = to_key(jnp.where(adm, acc, -jnp.inf))
            masked = jnp.where(adm, acc, 0.0)
            scale = (jnp.sum(masked * masked, axis=0, keepdims=True),
                     jnp.sum(jnp.where(adm, 1.0, 0.0), axis=0, keepdims=True))
        else:
            sk = to_key(acc)
            scale = None
        skey_ref[rows_here, :] = sk
        return jnp.sum(jnp.where(sk > ZERO_KEY, 1, 0), axis=0, keepdims=True), scale

    zero = jnp.zeros((1, tq), jnp.float32)
    izero = jnp.zeros((1, tq), i32)
    head_skip = first_key // BF16_ROWS * BF16_ROWS
    if head_skip:
        @pl.when(j == 0)
        def _():
            skey_ref[0:head_skip, :] = jnp.full((head_skip, tq), INT_MIN, i32)
    c_pos, (s2, s0) = score_block(0, True, head_skip)
    n_inner_pairs = (nkb - 2) // 2
    c_pos = lax.fori_loop(
        0, n_inner_pairs,
        lambda p, c: c + score_block(2 * p + 1, False)[0] + score_block(2 * p + 2, False)[0], c_pos)
    n_left = nkb - 1 - 2 * jnp.maximum(n_inner_pairs, 0)

    def last_block(c):
        c_l, (s2_l, s0_l) = score_block(nkb - 1, True)
        return c + c_l, s2 + s2_l, s0 + s0_l

    c_pos, s2, s0 = lax.cond(
        n_left == 2, lambda: last_block(c_pos + score_block(nkb - 2, False)[0]),
        lambda: lax.cond(n_left == 1, lambda: last_block(c_pos), lambda: (c_pos, s2, s0)))

    cslice_rows = count_blocks * kb_rows
    n_steps = (nkb + count_blocks - 1) // count_blocks
    if count_blocks > 1:
        skey_ref[kslice(nkb), :] = jnp.full((kb_rows, tq), INT_MIN, i32)

    assert cslice_rows % (SUM_CHAINS * SUBLANES) == 0

    def count_ge(cand):
        def body(c, acc):
            sk = skey_ref[pl.ds(pl.multiple_of(c * cslice_rows, SUBLANES), cslice_rows), :]
            ind = jnp.where(sk >= cand, 1, 0)
            return acc + jnp.sum(ind.reshape(SUM_CHAINS, -1, SUBLANES, tq), axis=1)
        acc = lax.fori_loop(0, n_steps, body, jnp.zeros((SUM_CHAINS, SUBLANES, tq), i32))
        return jnp.sum(acc.reshape(SUM_CHAINS * SUBLANES, tq), axis=0, keepdims=True)

    n_adm = jnp.maximum(lim - first_key, 0)
    n_f = jnp.maximum(n_adm, 1).astype(jnp.float32)
    sigma = jnp.sqrt(s2 / jnp.maximum(s0, 1.0))
    tail = jnp.clip(topk / n_f, 1e-6, 1.0 - 1e-6)
    tq_ = jnp.sqrt(-2.0 * jnp.log(jnp.minimum(tail, 1.0 - tail)))
    zq = tq_ - (2.515517 + 0.802853 * tq_ + 0.010328 * tq_ * tq_) / (
        1.0 + 1.432788 * tq_ + 0.189269 * tq_ * tq_ + 0.001308 * tq_ * tq_ * tq_)
    zq = jnp.where(tail < 0.5, zq, -zq)
    step_scale = 1.5 * sigma / jnp.maximum(jnp.abs(zq), 0.5)
    log_k = math.log(topk)

    def search_cond(st):
        return jnp.logical_and(st[0] < SEARCH_CAP, st[1] > 0)

    def search_step(st):
        it, lo, hi, c_lo, c_hi, lo_real, hi_real, w_lo, w_hi, last, stall, done = st
        f_lo, f_hi = to_f32(lo), to_f32(hi)
        g_lo = jnp.log(c_lo.astype(jnp.float32)) - log_k
        g_hi = log_k - jnp.log(jnp.maximum(c_hi.astype(jnp.float32), 0.5))
        t_in = f_lo + (f_hi - f_lo) * (w_lo * g_lo / (w_lo * g_lo + w_hi * g_hi))
        t_up = f_lo + step_scale * g_lo
        t_dn = f_hi - step_scale * g_hi
        t = jnp.where(lo_real > 0, jnp.where(hi_real > 0, t_in, t_up), t_dn)
        cand = jnp.minimum(jnp.maximum(to_key(t), lo + 1), hi - 1)
        mid = (lo >> 1) + (hi >> 1) + (lo & hi & 1)
        cand = jnp.where(stall >= STALL_STEPS, mid, jnp.where(it >= INTERP_STEPS, mid, cand))
        cand = jnp.where(it == 0, jnp.where(below, ZERO_KEY, cand), cand)
        cnt = count_ge(cand)
        up = jnp.where(done > 0, 0, jnp.where(cnt >= topk, 1, 0))
        dn = jnp.where(done > 0, 0, jnp.where(cnt >= topk, 0, 1))
        both = lo_real * hi_real
        stall = stall + both * jnp.where(cnt == c_lo, 1, jnp.where(cnt == c_hi, 1, 0))
        lo = jnp.where(up > 0, cand, lo)
        c_lo = jnp.where(up > 0, cnt, c_lo)
        lo_real = jnp.maximum(lo_real, up)
        hi = jnp.where(dn > 0, cand, hi)
        c_hi = jnp.where(dn > 0, cnt, c_hi)
        hi_real = jnp.maximum(hi_real, dn)
        w_hi = jnp.where(up > 0, jnp.where(last > 0, 0.5 * w_hi, 1.0), jnp.where(dn > 0, 1.0, w_hi))
        w_lo = jnp.where(dn > 0, jnp.where(last < 0, 0.5 * w_lo, 1.0), jnp.where(up > 0, 1.0, w_lo))
        last = up - dn + (1 - up - dn) * last
        done = jnp.where(c_lo == topk, 1, jnp.where(hi == lo + 1, 1, done))
        return it + 1, lo, hi, c_lo, c_hi, lo_real, hi_real, w_lo, w_hi, last, stall, done

    def search_body(st):
        inner = (st[0],) + st[2:]
        for _ in range(SEARCH_UNROLL):
            inner = search_step(inner)
        return (inner[0], jnp.sum(1 - inner[-1])) + inner[1:]

    small = n_adm <= topk
    above = c_pos >= topk
    below = jnp.where(small, 0, jnp.where(above, 0, 1)) > 0
    lo0 = jnp.where(small, NEG_INF_KEY, jnp.where(above, ZERO_KEY + 1, MIN_FINITE_KEY))
    hi0 = jnp.where(small, NEG_INF_KEY + 1, jnp.where(above, POS_INF_KEY, ZERO_KEY + 1))
    c_lo0 = jnp.where(small, topk, jnp.where(above, c_pos, n_adm))
    c_hi0 = jnp.where(small, 0, jnp.where(above, 0, c_pos))
    lo_real0 = jnp.where(small, 0, jnp.where(above, 1, 0))
    hi_real0 = jnp.where(small, 0, jnp.where(above, 0, 1))
    done0 = jnp.where(c_lo0 == topk, 1, jnp.where(hi0 == lo0 + 1, 1, 0))
    one = jnp.ones((1, tq), jnp.float32)
    st = lax.while_loop(search_cond, search_body,
                        (i32(0), jnp.sum(1 - done0), lo0, hi0, c_lo0, c_hi0, lo_real0, hi_real0, one, one,
                         izero, izero, done0))
    thr, hi, c_hi = st[2], st[3], st[5]
    need = jnp.where(small, 0, jnp.where(hi == thr + 1, topk - c_hi, TAKE_ALL)).astype(jnp.float32)

    def block_bias(kb, run):
        sk = skey_ref[kslice(kb), :]
        tie = jnp.where(sk == thr, 1.0, 0.0)
        rank = _dot(tri_ref[...], tie.astype(jnp.bfloat16)) + run
        tie_bias = jnp.where(rank <= need, 0.0, NEG_BIAS)
        bias = jnp.where(sk > thr, 0.0, jnp.where(sk == thr, tie_bias, NEG_BIAS))
        return bias, rank[kb_rows - 1:kb_rows, :]

    gw = N_GROUPS * tq
    groups = range(N_KV_HEADS)
    lane = lax.broadcasted_iota(i32, (1, LANES), 1)

    @pl.when(j == 0)
    def _():
        def norm_body(kb, best):
            out = []
            for g in groups:
                k = jnp.where(lane < HEAD_DIM, kb_ref[g, kslice(kb), :].astype(jnp.float32), 0.0)
                n2 = jnp.max(jnp.sum(k * k, axis=1, keepdims=True), axis=0, keepdims=True)
                out.append(jnp.maximum(best[g], n2))
            return tuple(out)
        best = lax.fori_loop(0, n_kblocks, norm_body, (jnp.zeros((1, 1), jnp.float32),) * N_KV_HEADS)
        for g in groups:
            kmax_ref[g] = jnp.broadcast_to(best[g], (SUBLANES, LANES))

    pad_row = lax.broadcasted_iota(i32, (KEY_LANES - HEAD_DIM, 1), 0)
    for g in groups:
        q = jnp.concatenate([qt_ref[h * HEAD_DIM:(h + 1) * HEAD_DIM, :]
                             for h in range(g * N_GROUPS, (g + 1) * N_GROUPS)], axis=1)
        qf = q.astype(jnp.float32)
        bound = jnp.sqrt(jnp.sum(qf * qf, axis=0, keepdims=True) * kmax_ref[g][0:1, 0:1]) * BOUND_MARGIN
        pad = jnp.where(pad_row == 0, -bound, 0.0).astype(jnp.bfloat16)
        qg_ref[g] = jnp.concatenate([q, pad], axis=0)

    def value_rows(kb, g):
        ones = jnp.ones((SUBLANES, kb_rows), jnp.bfloat16)
        return jnp.concatenate([vt_ref[kb, g * HEAD_DIM:(g + 1) * HEAD_DIM, :], ones], axis=0)

    def pipelined(logits_stage, value_stage, ms):
        def step(kb, slot, ms, staged, prefetch):
            extra, run = staged
            nxt = logits_stage(kb + 1, 1 - slot, run) if prefetch else None
            return tuple(value_stage(kb, g, slot, ms[g], extra[g]) for g in groups), nxt

        def pair_body(i, carry):
            ms, staged = carry
            ms, staged = step(2 * i, 0, ms, staged, True)
            return step(2 * i + 1, 1, ms, staged, True)

        n_pairs = (nkb - 1) // 2
        ms, staged = lax.fori_loop(0, n_pairs, pair_body, (ms, logits_stage(0, 0, zero)))
        last = 2 * n_pairs

        def two_left():
            ms1, staged1 = step(last, 0, ms, staged, True)
            return step(last + 1, 1, ms1, staged1, False)[0]

        return lax.cond(nkb - last == 2, two_left, lambda: step(last, 0, ms, staged, False)[0])

    def logits(kb, g, bias):
        return _dot(kb_ref[g, kslice(kb), :], qg_ref[g]) + bias

    def bounded_logits_stage(kb, slot, run):
        bias, run = block_bias(kb, run)
        bias = jnp.concatenate([bias] * N_GROUPS, axis=1)
        for g in groups:
            p_ref[slot, g] = jnp.exp2(logits(kb, g, bias)).astype(jnp.bfloat16)
        return (izero,) * N_KV_HEADS, run

    def bounded_value_stage(kb, g, slot, m, unused):
        acc_ref[g] = acc_ref[g] + _dot(value_rows(kb, g), p_ref[slot, g])
        return m

    def online_logits_stage(kb, slot, run):
        bias, run = block_bias(kb, run)
        bias = jnp.concatenate([bias] * N_GROUPS, axis=1)
        m_blks = []
        for g in groups:
            lg = logits(kb, g, bias)
            lg_ref[slot, g] = lg
            m_blks.append(jnp.max(lg, axis=0, keepdims=True))
        return tuple(m_blks), run

    def online_value_stage(kb, g, slot, m, m_blk):
        m_new = jnp.maximum(m, m_blk)
        p = jnp.exp2(lg_ref[slot, g] - m_new).astype(jnp.bfloat16)
        acc_ref[g] = jnp.exp2(m - m_new) * acc_ref[g] + _dot(value_rows(kb, g), p)
        return m_new

    has_key = jnp.concatenate([n_adm] * N_GROUPS, axis=1) > 0
    acc_ref[...] = jnp.zeros_like(acc_ref)
    pipelined(bounded_logits_stage, bounded_value_stage, (izero,) * N_KV_HEADS)
    underflow = izero[:, 0:1]
    for g in groups:
        denom = acc_ref[g][HEAD_DIM:HEAD_DIM + 1, :]
        underflow = underflow + jnp.sum(jnp.where(has_key, jnp.where(denom < MIN_DENOMINATOR, 1, 0), 0),
                                        axis=1, keepdims=True)

    @pl.when(underflow[0, 0] > 0)
    def _():
        acc_ref[...] = jnp.zeros_like(acc_ref)
        for g in groups:
            qg_ref[g, HEAD_DIM:, :] = jnp.zeros((KEY_LANES - HEAD_DIM, gw), jnp.bfloat16)
        pipelined(online_logits_stage, online_value_stage,
                  (jnp.full((1, gw), NEG_BIAS, jnp.float32),) * N_KV_HEADS)

    for g in groups:
        acc = acc_ref[g]
        og = jnp.where(has_key, acc[:HEAD_DIM] / acc[HEAD_DIM:HEAD_DIM + 1], 0.0)
        for hh in range(N_GROUPS):
            h = g * N_GROUPS + hh
            ot_ref[h * HEAD_DIM:(h + 1) * HEAD_DIM, :] = og[:, hh * tq:(hh + 1) * tq]

    o_ref[...] = ot_ref[...].T.astype(o_ref.dtype)


def _attention(qt, qit, wit, kb, kib, vt, *, n_seq, lq, lk, tq, kb_rows, causal, first_key, topk):
    nq = lq // tq
    n_kblocks = lk // kb_rows
    tri = jnp.tril(jnp.ones((kb_rows, kb_rows), jnp.bfloat16))
    count_blocks = 2 if causal else 1
    kernel = functools.partial(_attn_kernel, tq=tq, kb_rows=kb_rows, n_kblocks=n_kblocks,
                               count_blocks=count_blocks, causal=causal, first_key=first_key, topk=topk)
    return pl.pallas_call(
        kernel,
        grid=(n_seq, nq),
        in_specs=[
            pl.BlockSpec((Q_COLS, tq), lambda b, j: (0, b * nq + j)),
            pl.BlockSpec((IQ_COLS, tq), lambda b, j: (0, b * nq + j)),
            pl.BlockSpec((IW_ROWS, tq), lambda b, j: (0, b * nq + j)),
            pl.BlockSpec((N_KV_HEADS, lk, KEY_LANES), lambda b, j: (0, b, 0)),
            pl.BlockSpec((lk, IDX_DIM), lambda b, j: (b, 0)),
            pl.BlockSpec((n_kblocks, KV_COLS, kb_rows), lambda b, j: (b, 0, 0)),
            pl.BlockSpec((kb_rows, kb_rows), lambda b, j: (0, 0)),
        ],
        out_specs=pl.BlockSpec((tq, Q_COLS), lambda b, j: (b * nq + j, 0)),
        out_shape=jax.ShapeDtypeStruct((n_seq * lq, Q_COLS), jnp.bfloat16),
        scratch_shapes=[
            pltpu.VMEM((lk + (count_blocks - 1) * kb_rows, tq), jnp.int32),
            pltpu.VMEM((Q_COLS, tq), jnp.float32),
            pltpu.VMEM((N_KV_HEADS, KEY_LANES, N_GROUPS * tq), jnp.bfloat16),
            pltpu.VMEM((N_KV_HEADS, HEAD_DIM + SUBLANES, N_GROUPS * tq), jnp.float32),
            pltpu.VMEM((2, N_KV_HEADS, kb_rows, N_GROUPS * tq), jnp.float32),
            pltpu.VMEM((2, N_KV_HEADS, kb_rows, N_GROUPS * tq), jnp.bfloat16),
            pltpu.VMEM((N_KV_HEADS, SUBLANES, LANES), jnp.float32),
        ],
        compiler_params=pltpu.CompilerParams(dimension_semantics=("arbitrary", "arbitrary"),
                                             vmem_limit_bytes=VMEM_LIMIT),
        name="dsa_attention_causal" if causal else "dsa_attention_full",
    )(qt, qit, wit, kb, kib, vt, tri)


def _mix_out_kernel(x_ref, o_ref, u_ref, halo_ref, cb_ref, g0_ref, g1_ref, cw_ref,
                    wg_ref, wao_ref, wco_ref, wo_ref, x1_ref, *, n_seg):
    x = x_ref[...]
    h = _rms(x, g0_ref[...]).astype(jnp.bfloat16)
    gates = _dot(h, wg_ref[...])
    y_a = _dot(o_ref[...], wao_ref[...])
    u = u_ref[...]
    halo = halo_ref[0]
    t = u.shape[0]
    seg = t // n_seg
    row = lax.broadcasted_iota(jnp.int32, (t, 1), 0)
    u1 = pltpu.roll(u, 1, 0)
    u2 = pltpu.roll(u, 2, 0)
    for s in range(n_seg):
        h6 = halo[s * SUBLANES + 6:s * SUBLANES + 7]
        h7 = halo[s * SUBLANES + 7:s * SUBLANES + 8]
        u1 = jnp.where(row == s * seg, h7, u1)
        u2 = jnp.where(row == s * seg, h6, jnp.where(row == s * seg + 1, h7, u2))
    conv = cw_ref[0:1] * u2 + cw_ref[1:2] * u1 + cw_ref[2:3] * u
    y_b = _dot((cb_ref[...] * conv).astype(jnp.bfloat16), wco_ref[...])
    m = _sigmoid(gates[:, :D_MODEL]) * y_a + _sigmoid(gates[:, D_MODEL:]) * y_b
    a = _dot(m.astype(jnp.bfloat16), wo_ref[...])
    x1_ref[...] = x + _rms(a, g1_ref[...])


def _mix_out(x, o, u, halo, cb, g0, g1, cw, wg, wao, wco, wo):
    n = x.shape[0]
    tm = MIX_TILE if n % MIX_TILE == 0 else ROW_TILE
    halo = halo.reshape(n // tm, -1, CONV_DIM)
    n_seg = halo.shape[1] // SUBLANES
    full = lambda shape: pl.BlockSpec(shape, lambda i: (0,) * len(shape))
    resident = lambda shape: pl.BlockSpec(shape, lambda i: (0,) * len(shape), pipeline_mode=pl.Buffered(1))
    rows = lambda w: pl.BlockSpec((tm, w), lambda i: (i, 0))
    return pl.pallas_call(
        functools.partial(_mix_out_kernel, n_seg=n_seg),
        grid=(n // tm,),
        in_specs=[rows(D_MODEL), rows(Q_COLS), rows(CONV_DIM),
                  pl.BlockSpec((1, n_seg * SUBLANES, CONV_DIM), lambda i: (i, 0, 0)), rows(CONV_DIM),
                  full((1, D_MODEL)), full((1, D_MODEL)), full((CONV_WIDTH, CONV_DIM)),
                  resident((D_MODEL, 2 * D_MODEL)), resident((Q_COLS, D_MODEL)), resident((CONV_DIM, D_MODEL)),
                  resident((D_MODEL, D_MODEL))],
        out_specs=rows(D_MODEL),
        out_shape=jax.ShapeDtypeStruct((n, D_MODEL), jnp.float32),
        compiler_params=pltpu.CompilerParams(dimension_semantics=("arbitrary",), vmem_limit_bytes=VMEM_LIMIT),
        name="mix_out",
    )(x, o, u, halo, cb, g0, g1, cw, wg, wao, wco, wo)


def _mlp_kernel(x_ref, g2_ref, g3_ref, wup_ref, wdn_ref, x2_ref):
    x = x_ref[...]
    h = _rms(x, g2_ref[...]).astype(jnp.bfloat16)
    up = jnp.maximum(_dot(h, wup_ref[...]), 0.0)
    f = _dot((up * up).astype(jnp.bfloat16), wdn_ref[...])
    x2_ref[...] = x + _rms(f, g3_ref[...])


def _mlp(x, g2, g3, wup, wdn, drop_head_tiles_of=None):
    n = x.shape[0]
    tm = ROW_TILE if drop_head_tiles_of is not None or n % MLP_TILE else MLP_TILE
    full = lambda shape: pl.BlockSpec(shape, lambda i: (0,) * len(shape))
    resident = lambda shape: pl.BlockSpec(shape, lambda i: (0,) * len(shape), pipeline_mode=pl.Buffered(1))
    rows = lambda w: pl.BlockSpec((tm, w), lambda i: (i, 0))
    if drop_head_tiles_of is None:
        out_rows, out_spec = n, rows(D_MODEL)
    else:
        tps = drop_head_tiles_of
        out_rows = n // tps * (tps - 1)
        out_spec = pl.BlockSpec((tm, D_MODEL), lambda i: (i // tps * (tps - 1) + jnp.maximum(i % tps - 1, 0), 0))
    return pl.pallas_call(
        _mlp_kernel,
        grid=(n // tm,),
        in_specs=[rows(D_MODEL), full((1, D_MODEL)), full((1, D_MODEL)), resident((D_MODEL, D_FF)),
                  resident((D_FF, D_MODEL))],
        out_specs=out_spec,
        out_shape=jax.ShapeDtypeStruct((out_rows, D_MODEL), jnp.float32),
        compiler_params=pltpu.CompilerParams(dimension_semantics=("arbitrary",), vmem_limit_bytes=VMEM_LIMIT),
        name="mlp",
    )(x, g2, g3, wup, wdn)


def _prepare_weights(w_in, conv_w, w_attn_out, w_conv_out, w_o, w_up, w_down):
    bf16 = jnp.bfloat16
    wt = jnp.concatenate([
        w_in[:, :, :END_Q] * (HEAD_DIM ** -0.5),
        w_in[:, :, END_Q:END_V],
        w_in[:, :, END_V:END_IQ] * (IDX_DIM ** -0.5),
        w_in[:, :, END_IQ:END_IK],
        w_in[:, :, END_IK:END_IW] * (IDX_HEADS ** -0.5),
        jnp.zeros(w_in.shape[:2] + (IW_ROWS - IDX_HEADS,), w_in.dtype),
    ], axis=2).astype(bf16).transpose(0, 2, 1)
    return dict(
        wt=wt,
        wc=w_in[:, :, END_IW:END_CX].astype(bf16),
        wg=w_in[:, :, END_CX:].astype(bf16),
        cw=conv_w,
        wao=w_attn_out.astype(bf16), wco=w_conv_out.astype(bf16), wo=w_o.astype(bf16),
        wup=w_up.astype(bf16), wdn=w_down.astype(bf16))


def _rope_tables(pos):
    inv = jnp.exp(-math.log(ROPE_THETA) * jnp.arange(ROT_HALF, dtype=jnp.float32) * (2.0 / ROT_DIM))
    ang = inv[:, None] * pos.astype(jnp.float32)[None, :]
    return jnp.cos(ang), jnp.sin(ang)


def _halo(u, n_seq, past):
    n = u.shape[0]
    per_seq = n // n_seq
    first = jnp.concatenate([jnp.zeros((n_seq, SUBLANES - (CONV_WIDTH - 1), CONV_DIM), u.dtype), past], axis=1)
    if per_seq >= ROW_TILE:
        tiles = per_seq // ROW_TILE
        tails = u.reshape(n_seq, tiles, ROW_TILE, CONV_DIM)[:, :tiles - 1, ROW_TILE - SUBLANES:]
        return jnp.concatenate([first[:, None], tails], axis=1).reshape(n_seq * tiles, SUBLANES, CONV_DIM)
    n_seg = ROW_TILE // per_seq
    return first.reshape(n_seq // n_seg, n_seg * SUBLANES, CONV_DIM)


def _with_ones_column(k):
    tail = jnp.zeros(k.shape[:-1] + (KEY_LANES - HEAD_DIM,), k.dtype).at[..., 0].set(1)
    return jnp.concatenate([k, tail], axis=-1)


def _pad_lanes(a, n_seq, per_seq, width):
    a = a.reshape(a.shape[0], n_seq, per_seq)
    a = jnp.pad(a, ((0, 0), (0, 0), (0, width - per_seq)))
    return a.reshape(a.shape[0], n_seq * width)


def kernel(x_prompt, x_sample, cache_k, cache_v, cache_kidx, state_conv, meta_tokens, norm_gains, w_in, conv_w,
           w_attn_out, w_conv_out, w_o, w_up, w_down):
    f32, bf16 = jnp.float32, jnp.bfloat16
    depth = w_in.shape[0]
    bp, seq, _ = x_prompt.shape
    bs, s_len, _ = x_sample.shape
    past_len = cache_k.shape[2]
    assert seq % Q_TILE == 0 and Q_TILE % CHUNK == 0 and ROW_TILE == Q_TILE and N_META <= Q_TILE
    assert (bs * s_len) % ROW_TILE == 0 and ROW_TILE % s_len == 0
    assert CONV_WIDTH - 1 <= s_len <= LANES and (past_len + s_len) % SUBLANES == 0

    pad = Q_TILE - N_META
    rp = Q_TILE + seq
    xp = jnp.concatenate([
        jnp.zeros((bp, pad, D_MODEL), f32),
        jnp.broadcast_to(meta_tokens[None].astype(f32), (bp, N_META, D_MODEL)),
        x_prompt.astype(f32)], axis=1).reshape(bp * rp, D_MODEL)
    cos_p, sin_p = _rope_tables(jnp.tile(jnp.arange(rp, dtype=jnp.int32) - pad, bp))
    topk_p = min(TOPK_MAX, seq // 4)

    xs = x_sample.astype(f32).reshape(bs * s_len, D_MODEL)
    ls = past_len + s_len
    cos_s, sin_s = _rope_tables(jnp.tile(past_len + jnp.arange(s_len, dtype=jnp.int32), bs))
    topk_s = min(TOPK_MAX, ls // 4)
    conv_zero = jnp.zeros((bp, CONV_WIDTH - 1, CONV_DIM), f32)

    weights = _prepare_weights(w_in, conv_w, w_attn_out, w_conv_out, w_o, w_up, w_down)
    outs = [[] for _ in range(8)]
    for l in range(depth):
        w = {name: value[l] for name, value in weights.items()}
        g = [norm_gains[l, i][None, :] for i in range(4)]

        qt, qit, wit, vt, kb, kib, kt32, vt32, kit32, u, cb = _in_proj(xp, g[0], w["wt"], w["wc"], cos_p, sin_p)
        o = _attention(qt, qit, wit, kb, kib, vt, n_seq=bp, lq=rp, lk=rp, tq=Q_TILE, kb_rows=ROW_TILE,
                       causal=True, first_key=pad, topk=topk_p)
        x1 = _mix_out(xp, o, u, _halo(u, bp, conv_zero), cb, g[0], g[1], w["cw"], w["wg"], w["wao"], w["wco"],
                      w["wo"])
        xp = _mlp(x1, g[2], g[3], w["wup"], w["wdn"], drop_head_tiles_of=rp // ROW_TILE if l == depth - 1 else None)
        outs[0].append(kt32.reshape(N_KV_HEADS, HEAD_DIM, bp, rp)[..., pad:].transpose(2, 3, 0, 1))
        outs[1].append(vt32.reshape(N_KV_HEADS, HEAD_DIM, bp, rp)[..., pad:].transpose(2, 3, 0, 1))
        outs[2].append(kit32.reshape(IDX_DIM, bp, rp)[..., pad:].transpose(1, 2, 0))
        outs[3].append(u.reshape(bp, rp, CONV_DIM)[:, rp - (CONV_WIDTH - 1):])

        qt, qit, wit, vt, kb, kib, kt32, vt32, kit32, u, cb = _in_proj(xs, g[0], w["wt"], w["wc"], cos_s, sin_s)
        k_new = kt32.T.reshape(bs, s_len, N_KV_HEADS, HEAD_DIM)
        v_new = vt32.T.reshape(bs, s_len, N_KV_HEADS, HEAD_DIM)
        ki_new = kit32.T.reshape(bs, s_len, IDX_DIM)
        k_all = jnp.concatenate([cache_k[l], k_new], axis=1).astype(bf16)
        v_all = jnp.concatenate([cache_v[l], v_new], axis=1).astype(bf16)
        ki_all = jnp.concatenate([cache_kidx[l], ki_new], axis=1).astype(bf16)
        o = _attention(
            _pad_lanes(qt, bs, s_len, LANES), _pad_lanes(qit, bs, s_len, LANES), _pad_lanes(wit, bs, s_len, LANES),
            _with_ones_column(k_all.transpose(2, 0, 1, 3).reshape(N_KV_HEADS, bs * ls, HEAD_DIM)),
            ki_all.reshape(bs * ls, IDX_DIM),
            v_all.reshape(bs, ls, KV_COLS).transpose(0, 2, 1),
            n_seq=bs, lq=LANES, lk=ls, tq=LANES, kb_rows=ls, causal=False, first_key=0, topk=topk_s)
        o = o.reshape(bs, LANES, Q_COLS)[:, :s_len].reshape(bs * s_len, Q_COLS)
        x1 = _mix_out(xs, o, u, _halo(u, bs, state_conv[l].astype(f32)), cb, g[0], g[1], w["cw"], w["wg"],
                      w["wao"], w["wco"], w["wo"])
        xs = _mlp(x1, g[2], g[3], w["wup"], w["wdn"])
        outs[4].append(k_new)
        outs[5].append(v_new)
        outs[6].append(ki_new)
        outs[7].append(u.reshape(bs, s_len, CONV_DIM)[:, s_len - (CONV_WIDTH - 1):])

    y_prompt = xp.reshape(bp, seq, D_MODEL)
    y_sample = xs.reshape(bs, s_len, D_MODEL)
    return (y_prompt, y_sample) + tuple(jnp.stack(o) for o in outs)
```
